```python
import math
import jax, jax.numpy as jnp
from jax import lax
import numpy as np

D_MODEL = 1024
BATCH = 32
SEQ = 2048
DEPTH = 2

CTX_LEN = 256
GRID_W = 64
HEAD_DIM = 64
A_HEADS = 8
A_KV_HEADS = 2
A_GROUP = A_HEADS // A_KV_HEADS
WINDOW = 128
A_BLOCK = 128
B_HEADS = 8
NA_ROWS = 8
NA_COLS = 16
C_WIDTH = 512
HY_ORDER = 2
HY_DIRS = 2
HY_BANDS = 16
HY_EMB = 1 + 2 * HY_BANDS
HY_HIDDEN = 64
HY_TARGET = 1e-2
HY_FAST = 0.3
HY_SLOW = 1.5
N_BRANCH = 3
BRANCH_W = 512
D_FF = 2816
N_EXPERTS = 8
TOP_K = 2
D_FF_E = 3584
N_DENSE = (DEPTH + 1) // 2
N_MOE = DEPTH // 2

ROPE_BASE = 10000.0
EPS = 1e-6
NEG = -1e30

A_KV_W = A_KV_HEADS * HEAD_DIM
A_Q_W = A_HEADS * HEAD_DIM
B_W = B_HEADS * HEAD_DIM
KV_COLS = 2 * A_KV_W + 2 * B_W
IN_COLS = KV_COLS + A_Q_W + B_W + 3 * C_WIDTH + N_BRANCH * D_MODEL
SPLITS = (A_KV_W, 2 * A_KV_W, 2 * A_KV_W + B_W, KV_COLS,
          KV_COLS + A_Q_W, KV_COLS + A_Q_W + B_W,
          KV_COLS + A_Q_W + B_W + 3 * C_WIDTH)

kernel_name = "hybrid_gated_mixer_diffusion_block"


def rmsnorm(x, g):
    xf = x.astype(jnp.float32)
    y = xf * lax.rsqrt(jnp.mean(xf * xf, axis=-1, keepdims=True) + EPS)
    return (y * g.astype(jnp.float32)).astype(x.dtype)


def rope_axis(x, pos):
    half = x.shape[-1] // 2
    inv = ROPE_BASE ** (-jnp.arange(half, dtype=jnp.float32) / half)
    ang = pos.astype(jnp.float32)[:, None] * inv
    cos = jnp.cos(ang)[None, :, None, :].astype(x.dtype)
    sin = jnp.sin(ang)[None, :, None, :].astype(x.dtype)
    a, b = x[..., :half], x[..., half:]
    return jnp.concatenate([a * cos - b * sin, a * sin + b * cos], axis=-1)


def rope_2d(x, rows, cols):
    h = x.shape[-1] // 2
    return jnp.concatenate([rope_axis(x[..., :h], rows), rope_axis(x[..., h:], cols)], axis=-1)


def softmax_with_sink(s, sink):
    m = jnp.maximum(jnp.max(s, axis=-1, keepdims=True), sink)
    p = jnp.exp(s - m)
    return p / (jnp.sum(p, axis=-1, keepdims=True) + jnp.exp(sink - m))


def window_attn_latent(q, k, v, kc, vc, sink):
    B, L = q.shape[0], q.shape[1]
    nb = L // A_BLOCK
    span = A_BLOCK + 2 * WINDOW
    scale = HEAD_DIM ** -0.5
    pad = ((0, 0), (WINDOW, WINDOW), (0, 0), (0, 0))
    kp, vp = jnp.pad(k, pad), jnp.pad(v, pad)
    qg = q.reshape(B, L, A_KV_HEADS, A_GROUP, HEAD_DIM)
    sink_g = sink.astype(jnp.float32).reshape(1, A_KV_HEADS, A_GROUP, 1, 1)

    def block(i):
        q0 = i * A_BLOCK
        qb = lax.dynamic_slice_in_dim(qg, q0, A_BLOCK, axis=1)
        kb = lax.dynamic_slice_in_dim(kp, q0, span, axis=1)
        vb = lax.dynamic_slice_in_dim(vp, q0, span, axis=1)
        s_loc = jnp.einsum('bqgrd,bkgd->bgrqk', qb, kb).astype(jnp.float32) * scale
        s_ctx = jnp.einsum('bqgrd,bcgd->bgrqc', qb, kc).astype(jnp.float32) * scale
        qpos = q0 + jnp.arange(A_BLOCK)
        kpos = q0 - WINDOW + jnp.arange(span)
        valid = ((kpos[None, :] >= 0) & (kpos[None, :] < L)
                 & (jnp.abs(qpos[:, None] - kpos[None, :]) <= WINDOW))
        s_loc = jnp.where(valid, s_loc, NEG)
        p = softmax_with_sink(jnp.concatenate([s_loc, s_ctx], axis=-1), sink_g).astype(v.dtype)
        o = (jnp.einsum('bgrqk,bkgd->bqgrd', p[..., :span], vb)
             + jnp.einsum('bgrqc,bcgd->bqgrd', p[..., span:], vc))
        return o.reshape(B, A_BLOCK, A_Q_W)

    out = lax.map(block, jnp.arange(nb))
    return out.transpose(1, 0, 2, 3).reshape(B, L, A_Q_W)


def ctx_attn_a(qc, kc, vc, sink):
    B, Lc = qc.shape[0], qc.shape[1]
    qg = qc.reshape(B, Lc, A_KV_HEADS, A_GROUP, HEAD_DIM)
    s = jnp.einsum('bqgrd,bkgd->bgrqk', qg, kc).astype(jnp.float32) * HEAD_DIM ** -0.5
    sink_g = sink.astype(jnp.float32).reshape(1, A_KV_HEADS, A_GROUP, 1, 1)
    p = softmax_with_sink(s, sink_g).astype(vc.dtype)
    return jnp.einsum('bgrqk,bkgd->bqgrd', p, vc).reshape(B, Lc, A_Q_W)


def neighbourhood_attn_latent(q, k, v, kc, vc, rpb):
    B, L = q.shape[0], q.shape[1]
    rows = L // GRID_W
    wr = min(NA_ROWS, rows)
    scale = HEAD_DIM ** -0.5
    qg = q.reshape(B, rows, GRID_W, B_HEADS, HEAD_DIM)
    kg = k.reshape(B, rows, GRID_W, B_HEADS, HEAD_DIM)
    vg = v.reshape(B, rows, GRID_W, B_HEADS, HEAD_DIM)
    col = jnp.arange(GRID_W)
    col_start = jnp.clip(col - NA_COLS // 2, 0, GRID_W - NA_COLS)
    col_ok = (col[None, :] >= col_start[:, None]) & (col[None, :] < col_start[:, None] + NA_COLS)
    dc_idx = jnp.clip(col[None, :] - col[:, None] + NA_COLS - 1, 0, 2 * NA_COLS - 2)
    rpb_f = rpb.astype(jnp.float32)
    nloc = wr * GRID_W

    def row(r):
        rs = jnp.clip(r - wr // 2, 0, rows - wr)
        qr = lax.dynamic_index_in_dim(qg, r, axis=1, keepdims=False)
        kr = lax.dynamic_slice_in_dim(kg, rs, wr, axis=1)
        vr = lax.dynamic_slice_in_dim(vg, rs, wr, axis=1)
        dr_idx = rs + jnp.arange(wr) - r + NA_ROWS - 1
        bias = rpb_f[:, dr_idx[None, :, None], dc_idx[:, None, :]]
        s_loc = jnp.einsum('bqhd,bjkhd->bhqjk', qr, kr).astype(jnp.float32) * scale + bias
        s_loc = jnp.where(col_ok[:, None, :], s_loc, NEG).reshape(B, B_HEADS, GRID_W, nloc)
        s_ctx = jnp.einsum('bqhd,bchd->bhqc', qr, kc).astype(jnp.float32) * scale
        p = jax.nn.softmax(jnp.concatenate([s_loc, s_ctx], axis=-1), axis=-1).astype(v.dtype)
        p_loc = p[..., :nloc].reshape(B, B_HEADS, GRID_W, wr, GRID_W)
        o = (jnp.einsum('bhqjk,bjkhd->bqhd', p_loc, vr)
             + jnp.einsum('bhqc,bchd->bqhd', p[..., nloc:], vc))
        return o.reshape(B, GRID_W, B_W)

    out = lax.map(row, jnp.arange(rows))
    return out.transpose(1, 0, 2, 3).reshape(B, L, B_W)


def ctx_attn_b(qc, kc, vc):
    B, Lc = qc.shape[0], qc.shape[1]
    s = jnp.einsum('bqhd,bkhd->bhqk', qc, kc).astype(jnp.float32) * HEAD_DIM ** -0.5
    p = jax.nn.softmax(s, axis=-1).astype(vc.dtype)
    return jnp.einsum('bhqk,bkhd->bqhd', p, vc).reshape(B, Lc, B_W)


def short_conv(u, w, b):
    L = u.shape[1]
    up = jnp.pad(u, ((0, 0), (1, 1), (0, 0)))
    return up[:, :L] * w[0] + up[:, 1:L + 1] * w[1] + up[:, 2:] * w[2] + b


def hyena_filters(L, w1, b1, w2, b2, w3, freq):
    t = jnp.linspace(0.0, 1.0, L, dtype=jnp.float32)[:, None]
    omega = 2.0 * math.pi * jnp.arange(L, dtype=jnp.float32)[:, None] / L
    f = jnp.linspace(1e-4, HY_BANDS - 1, HY_BANDS, dtype=jnp.float32)[None, :]
    z = jnp.concatenate([t, jnp.cos(f * omega), -jnp.sin(f * omega)], axis=-1).astype(w1.dtype)
    h = jnp.sin(freq * (z @ w1 + b1))
    h = jnp.sin(freq * (h @ w2 + b2))
    h = (h @ w3).astype(jnp.float32).reshape(L, HY_DIRS, HY_ORDER, C_WIDTH)
    deltas = jnp.abs(jnp.linspace(math.log(HY_TARGET) / HY_SLOW, math.log(HY_TARGET) / HY_FAST,
                                  C_WIDTH, dtype=jnp.float32))
    h = h * jnp.exp(-t[:, :, None, None] * deltas)
    return h / jnp.sqrt(jnp.sum(h * h, axis=0, keepdims=True) + EPS)


def fft_conv(u, h):
    L = u.shape[1]
    n = 2 * L
    U = jnp.fft.rfft(u, n=n, axis=1)
    H = jnp.fft.rfft(h, n=n, axis=0)
    return jnp.fft.irfft(U * H[None], n=n, axis=1)[:, :L]


def bidir_long_conv(u, h, bias):
    return fft_conv(u, h[:, 0]) + fft_conv(u[:, ::-1], h[:, 1])[:, ::-1] + u * bias


def hyena(p, conv_w, conv_b, filt, hy_bias):
    v, x1, x2 = jnp.split(short_conv(p, conv_w, conv_b), 3, axis=-1)
    z = v.astype(jnp.float32)
    for o, gate in enumerate((x1, x2)):
        z = gate.astype(jnp.float32) * bidir_long_conv(z, filt[:, :, o], hy_bias[o].astype(jnp.float32))
    return z.astype(p.dtype)


def merge_branches(ya, yb, yc, gates, w_branch, w_out):
    g = jax.nn.sigmoid(gates.astype(jnp.float32)).astype(ya.dtype)
    ga, gb, gc = jnp.split(g, N_BRANCH, axis=-1)
    m = ga * (ya @ w_branch[0]) + gb * (yb @ w_branch[1]) + gc * (yc @ w_branch[2])
    return m @ w_out


def mixer_block(h, hc, lp, last):
    B, L, _ = h.shape
    Lc = hc.shape[1]
    t = jnp.arange(L)
    pos_r, pos_c = t // GRID_W, t % GRID_W
    ak, av, bk, bv, aq, bq, cin, gates = jnp.split(h @ lp['w_in'], SPLITS, axis=-1)
    aq = rope_2d(aq.reshape(B, L, A_HEADS, HEAD_DIM), pos_r, pos_c)
    ak = rope_2d(ak.reshape(B, L, A_KV_HEADS, HEAD_DIM), pos_r, pos_c)
    av = av.reshape(B, L, A_KV_HEADS, HEAD_DIM)
    bq = bq.reshape(B, L, B_HEADS, HEAD_DIM)
    bk = bk.reshape(B, L, B_HEADS, HEAD_DIM)
    bv = bv.reshape(B, L, B_HEADS, HEAD_DIM)
    w_in_c = lp['w_in'][:, :KV_COLS] if last else lp['w_in']
    parts_c = jnp.split(hc @ w_in_c, SPLITS[:3] if last else SPLITS, axis=-1)
    akc = parts_c[0].reshape(B, Lc, A_KV_HEADS, HEAD_DIM)
    avc = parts_c[1].reshape(B, Lc, A_KV_HEADS, HEAD_DIM)
    bkc = parts_c[2].reshape(B, Lc, B_HEADS, HEAD_DIM)
    bvc = parts_c[3].reshape(B, Lc, B_HEADS, HEAD_DIM)

    ya = window_attn_latent(aq, ak, av, akc, avc, lp['sink_a'])
    yb = neighbourhood_attn_latent(bq, bk, bv, bkc, bvc, lp['rpb_b'])
    filt = hyena_filters(L, lp['hy_w1'], lp['hy_b1'], lp['hy_w2'], lp['hy_b2'], lp['hy_w3'], lp['hy_freq'])
    yc = hyena(cin, lp['conv_c_w'], lp['conv_c_b'], filt, lp['hy_bias'])
    y = merge_branches(ya, yb, yc, gates, lp['w_branch'], lp['w_out'])
    if last:
        return y, None
    aqc = parts_c[4].reshape(B, Lc, A_HEADS, HEAD_DIM)
    bqc = parts_c[5].reshape(B, Lc, B_HEADS, HEAD_DIM)
    yac = ctx_attn_a(aqc, akc, avc, lp['sink_a'])
    ybc = ctx_attn_b(bqc, bkc, bvc)
    filt_c = hyena_filters(Lc, lp['hy_w1'], lp['hy_b1'], lp['hy_w2'], lp['hy_b2'], lp['hy_w3'], lp['hy_freq'])
    ycc = hyena(parts_c[6], lp['conv_c_w'], lp['conv_c_b'], filt_c, lp['hy_bias'])
    yctx = merge_branches(yac, ybc, ycc, parts_c[7], lp['w_branch'], lp['w_out'])
    return y, yctx


def swiglu(h, w1, w3, w2):
    return (jax.nn.silu(h @ w1) * (h @ w3)) @ w2


def moe_top2(h, router, w1, w3, w2):
    logits = (h @ router).astype(jnp.float32)
    top_v, top_i = lax.top_k(logits, TOP_K)
    top_w = jax.nn.softmax(top_v, axis=-1)
    combine = jnp.sum(jax.nn.one_hot(top_i, N_EXPERTS, dtype=jnp.float32) * top_w[..., None],
                      axis=-2).astype(h.dtype)

    def per_sample(args):
        hs, cs = args
        a = jnp.einsum('ld,edf->lef', hs, w1)
        b = jnp.einsum('ld,edf->lef', hs, w3)
        return jnp.einsum('lef,efd->ld', jax.nn.silu(a) * b * cs[:, :, None], w2)

    return lax.map(per_sample, (h, combine))


def setup_inputs(seed: int = 0) -> dict:
    key = jax.random.key(seed)
    ks = iter(jax.random.split(key, 40))
    f32 = jnp.float32

    def nrm(shape, scale):
        return jax.random.normal(next(ks), shape, f32) * scale

    def gain(shape):
        return 1.0 + nrm(shape, 0.02)

    D = D_MODEL
    return {
        "x": nrm((BATCH, SEQ, D), 1.0),
        "c": nrm((BATCH, D), 1.0),
        "ctx": nrm((BATCH, CTX_LEN, D), 1.0),
        "c_ctx": nrm((D,), 1.0),
        "w_mod": nrm((DEPTH, D, 6 * D), 0.5 * D ** -0.5),
        "b_mod": nrm((DEPTH, 6 * D), 0.02),
        "g_mix_pre": gain((DEPTH, D)),
        "g_mix_post": gain((DEPTH, D)),
        "g_ffn_pre": gain((DEPTH, D)),
        "g_ffn_post": gain((DEPTH, D)),
        "w_in": nrm((DEPTH, D, IN_COLS), D ** -0.5),
        "sink_a": nrm((DEPTH, A_HEADS), 0.5),
        "rpb_b": nrm((DEPTH, B_HEADS, 2 * NA_ROWS - 1, 2 * NA_COLS - 1), 0.5),
        "conv_c_w": nrm((DEPTH, 3, 3 * C_WIDTH), 3 ** -0.5),
        "conv_c_b": nrm((DEPTH, 3 * C_WIDTH), 0.02),
        "hy_w1": nrm((DEPTH, HY_EMB, HY_HIDDEN), HY_EMB ** -0.5),
        "hy_b1": nrm((DEPTH, HY_HIDDEN), 0.02),
        "hy_w2": nrm((DEPTH, HY_HIDDEN, HY_HIDDEN), HY_HIDDEN ** -0.5),
        "hy_b2": nrm((DEPTH, HY_HIDDEN), 0.02),
        "hy_w3": nrm((DEPTH, HY_HIDDEN, HY_DIRS * HY_ORDER * C_WIDTH), HY_HIDDEN ** -0.5),
        "hy_freq": gain((DEPTH, HY_HIDDEN)),
        "hy_bias": nrm((DEPTH, HY_ORDER, C_WIDTH), 1.0),
        "w_branch": nrm((DEPTH, N_BRANCH, BRANCH_W, D), BRANCH_W ** -0.5),
        "w_out": nrm((DEPTH, D, D), D ** -0.5),
        "ffn_w1": nrm((N_DENSE, D, D_FF), D ** -0.5),
        "ffn_w3": nrm((N_DENSE, D, D_FF), D ** -0.5),
        "ffn_w2": nrm((N_DENSE, D_FF, D), D_FF ** -0.5),
        "moe_router": nrm((N_MOE, D, N_EXPERTS), D ** -0.5),
        "moe_w1": nrm((N_MOE, N_EXPERTS, D, D_FF_E), D ** -0.5),
        "moe_w3": nrm((N_MOE, N_EXPERTS, D, D_FF_E), D ** -0.5),
        "moe_w2": nrm((N_MOE, N_EXPERTS, D_FF_E, D), D_FF_E ** -0.5),
    }


def reference(x, c, ctx, c_ctx, w_mod, b_mod, g_mix_pre, g_mix_post, g_ffn_pre, g_ffn_post,
              w_in, sink_a, rpb_b, conv_c_w, conv_c_b, hy_w1, hy_b1, hy_w2, hy_b2, hy_w3,
              hy_freq, hy_bias, w_branch, w_out, ffn_w1, ffn_w3, ffn_w2,
              moe_router, moe_w1, moe_w3, moe_w2):
    silu_c = jax.nn.silu(c)
    silu_cc = jax.nn.silu(c_ctx)
    for l in range(DEPTH):
        last = l == DEPTH - 1
        mod = (silu_c @ w_mod[l] + b_mod[l])[:, None, :]
        modc = (silu_cc @ w_mod[l] + b_mod[l])[None, None, :]
        sh_m, sc_m, gt_m, sh_f, sc_f, gt_f = jnp.split(mod, 6, axis=-1)
        shc_m, scc_m, gtc_m, shc_f, scc_f, gtc_f = jnp.split(modc, 6, axis=-1)
        lp = dict(w_in=w_in[l], sink_a=sink_a[l], rpb_b=rpb_b[l], conv_c_w=conv_c_w[l],
                  conv_c_b=conv_c_b[l], hy_w1=hy_w1[l], hy_b1=hy_b1[l], hy_w2=hy_w2[l],
                  hy_b2=hy_b2[l], hy_w3=hy_w3[l], hy_freq=hy_freq[l], hy_bias=hy_bias[l],
                  w_branch=w_branch[l], w_out=w_out[l])
        h = rmsnorm(x, g_mix_pre[l]) * (1.0 + sc_m) + sh_m
        hc = rmsnorm(ctx, g_mix_pre[l]) * (1.0 + scc_m) + shc_m
        y, yctx = mixer_block(h, hc, lp, last)
        x = x + gt_m * rmsnorm(y, g_mix_post[l])
        if not last:
            ctx = ctx + gtc_m * rmsnorm(yctx, g_mix_post[l])
        h = rmsnorm(x, g_ffn_pre[l]) * (1.0 + sc_f) + sh_f
        j = l // 2
        if l % 2 == 0:
            y = swiglu(h, ffn_w1[j], ffn_w3[j], ffn_w2[j])
        else:
            y = moe_top2(h, moe_router[j], moe_w1[j], moe_w3[j], moe_w2[j])
        x = x + gt_f * rmsnorm(y, g_ffn_post[l])
        if not last:
            hc = rmsnorm(ctx, g_ffn_pre[l]) * (1.0 + scc_f) + shc_f
            if l % 2 == 0:
                yc = swiglu(hc, ffn_w1[j], ffn_w3[j], ffn_w2[j])
            else:
                yc = moe_top2(hc, moe_router[j], moe_w1[j], moe_w3[j], moe_w2[j])
            ctx = ctx + gtc_f * rmsnorm(yc, g_ffn_post[l])
    return x
```

```python
import functools
import math

import jax
import jax.numpy as jnp
import numpy as np
from jax import lax
from jax.experimental import pallas as pl
from jax.experimental.pallas import tpu as pltpu

F32 = jnp.float32
BF16 = jnp.bfloat16

D_MODEL = 1024
DEPTH = 2
GRID_W = 64
HEAD_DIM = 64
A_HEADS = 8
A_KV_HEADS = 2
A_GROUP = A_HEADS // A_KV_HEADS
WINDOW = 128
A_BLOCK = 128
B_HEADS = 8
NA_ROWS = 8
NA_COLS = 16
C_WIDTH = 512
HY_ORDER = 2
HY_DIRS = 2
HY_BANDS = 16
HY_TARGET = 1e-2
HY_FAST = 0.3
HY_SLOW = 1.5
N_BRANCH = 3
N_EXPERTS = 8
TOP_K = 2
ROPE_BASE = 10000.0
EPS = 1e-6
NEG = -1e30

A_KV_W = A_KV_HEADS * HEAD_DIM
A_Q_W = A_HEADS * HEAD_DIM
B_W = B_HEADS * HEAD_DIM
KV_COLS = 2 * A_KV_W + 2 * B_W
SPLITS = (A_KV_W, 2 * A_KV_W, 2 * A_KV_W + B_W, KV_COLS,
          KV_COLS + A_Q_W, KV_COLS + A_Q_W + B_W,
          KV_COLS + A_Q_W + B_W + 3 * C_WIDTH)

OFF_GATES = 0
OFF_CIN = 3072
OFF_AQ = 4608
OFF_AQR = 5120
OFF_BQ = 5632
OFF_BK = 6144
OFF_BV = 6656
OFF_AK = 7168
OFF_AKR = 7296
OFF_AV = 7424
P_COLS = 7680
KV_OFF = OFF_BK
KV_W = P_COLS - KV_OFF

VMEM_LIMIT = 48 * 1024 * 1024


def _cparams(sem):
    return pltpu.CompilerParams(dimension_semantics=sem, vmem_limit_bytes=VMEM_LIMIT)


def _rms(y, g):
    return y * lax.rsqrt(jnp.mean(y * y, axis=-1, keepdims=True) + EPS) * g


def _norm_mod(x, g, sc, sh):
    return _rms(x, g) * (1.0 + sc) + sh


def _mod_kernel(c_ref, w_ref, b_ref, o_ref):
    c = c_ref[...]
    s = c * jax.nn.sigmoid(c)
    o_ref[...] = jnp.dot(s, w_ref[...], preferred_element_type=F32,
                         precision=lax.Precision.HIGHEST) + b_ref[...]


def modulation(cs, w, b, tn=512):
    m, d = cs.shape
    n = w.shape[1]
    return pl.pallas_call(
        _mod_kernel,
        grid=(n // tn,),
        in_specs=[pl.BlockSpec((m, d), lambda j: (0, 0)),
                  pl.BlockSpec((d, tn), lambda j: (0, j)),
                  pl.BlockSpec((1, tn), lambda j: (0, j))],
        out_specs=pl.BlockSpec((m, tn), lambda j: (0, j)),
        out_shape=jax.ShapeDtypeStruct((m, n), F32),
        compiler_params=_cparams(("parallel",)),
        name="modulation",
    )(cs, w, b.reshape(1, n))


def _proj_kernel(x_ref, g_ref, sc_ref, sh_ref, w_ref, o_ref, h_ref):
    @pl.when(pl.program_id(1) == 0)
    def _():
        h_ref[...] = _norm_mod(x_ref[...], g_ref[...], sc_ref[0], sh_ref[0]).astype(BF16)

    o_ref[...] = jnp.dot(h_ref[...], w_ref[...], preferred_element_type=F32).astype(o_ref.dtype)


def proj(x2, g, sc, sh, w, rows_per_mod, tm=1024, tn=512):
    m, d = x2.shape
    n = w.shape[1]
    tm = min(tm, m)
    tpb = rows_per_mod // tm
    return pl.pallas_call(
        _proj_kernel,
        grid=(m // tm, n // tn),
        in_specs=[pl.BlockSpec((tm, d), lambda i, j: (i, 0)),
                  pl.BlockSpec((1, d), lambda i, j: (0, 0)),
                  pl.BlockSpec((1, 1, d), lambda i, j: (i // tpb, 0, 0)),
                  pl.BlockSpec((1, 1, d), lambda i, j: (i // tpb, 0, 0)),
                  pl.BlockSpec((d, tn), lambda i, j: (0, j))],
        out_specs=pl.BlockSpec((tm, tn), lambda i, j: (i, j)),
        out_shape=jax.ShapeDtypeStruct((m, n), BF16),
        scratch_shapes=[pltpu.VMEM((tm, d), BF16)],
        compiler_params=_cparams(("parallel", "arbitrary")),
        name="proj",
    )(x2, g.reshape(1, d), sc, sh, w)


def _merge_kernel(ya_ref, yb_ref, yc_ref, ga_ref, gb_ref, gc_ref, wb_ref, wo_ref,
                  x_ref, gp_ref, gt_ref, o_ref):
    def branch(y_ref, g_ref, k):
        t = jnp.dot(y_ref[...], wb_ref[k], preferred_element_type=F32)
        return jax.nn.sigmoid(g_ref[...].astype(F32)) * t

    m = branch(ya_ref, ga_ref, 0) + branch(yb_ref, gb_ref, 1) + branch(yc_ref, gc_ref, 2)
    y = jnp.dot(m.astype(BF16), wo_ref[...], preferred_element_type=F32)
    o_ref[...] = x_ref[...] + gt_ref[0] * _rms(y, gp_ref[...])


def merge(ya, yb, yc, p, wb, wo, x2, gpost, gt, rows_per_mod, tm=512):
    m, d = x2.shape
    bw = ya.shape[1]
    tm = min(tm, m)
    tpb = rows_per_mod // tm
    gblk = OFF_GATES // d
    return pl.pallas_call(
        _merge_kernel,
        grid=(m // tm,),
        in_specs=[pl.BlockSpec((tm, bw), lambda i: (i, 0)),
                  pl.BlockSpec((tm, bw), lambda i: (i, 0)),
                  pl.BlockSpec((tm, bw), lambda i: (i, 0)),
                  pl.BlockSpec((tm, d), lambda i: (i, gblk)),
                  pl.BlockSpec((tm, d), lambda i: (i, gblk + 1)),
                  pl.BlockSpec((tm, d), lambda i: (i, gblk + 2)),
                  pl.BlockSpec((N_BRANCH, bw, d), lambda i: (0, 0, 0)),
                  pl.BlockSpec((d, d), lambda i: (0, 0)),
                  pl.BlockSpec((tm, d), lambda i: (i, 0)),
                  pl.BlockSpec((1, d), lambda i: (0, 0)),
                  pl.BlockSpec((1, 1, d), lambda i: (i // tpb, 0, 0))],
        out_specs=pl.BlockSpec((tm, d), lambda i: (i, 0)),
        out_shape=jax.ShapeDtypeStruct((m, d), F32),
        compiler_params=_cparams(("parallel",)),
        name="merge",
    )(ya, yb, yc, p, p, p, wb, wo, x2, gpost.reshape(1, d), gt)


def _ffn_kernel(x_ref, g_ref, sc_ref, sh_ref, w1_ref, w3_ref, w2_ref, gp_ref, gt_ref,
                o_ref, h_ref, acc_ref):
    f = pl.program_id(1)

    @pl.when(f == 0)
    def _():
        h_ref[...] = _norm_mod(x_ref[...], g_ref[...], sc_ref[0], sh_ref[0]).astype(BF16)
        acc_ref[...] = jnp.zeros_like(acc_ref)

    h = h_ref[...]
    a = jnp.dot(h, w1_ref[...], preferred_element_type=F32)
    b = jnp.dot(h, w3_ref[...], preferred_element_type=F32)
    act = (a * jax.nn.sigmoid(a) * b).astype(BF16)
    acc_ref[...] += jnp.dot(act, w2_ref[...], preferred_element_type=F32)

    @pl.when(f == pl.num_programs(1) - 1)
    def _():
        o_ref[...] = x_ref[...] + gt_ref[0] * _rms(acc_ref[...], gp_ref[...])


def ffn(x2, g, sc, sh, w1, w3, w2, gpost, gt, rows_per_mod, tm=512, tf=1408):
    m, d = x2.shape
    ff = w1.shape[1]
    tm = min(tm, m)
    tpb = rows_per_mod // tm
    return pl.pallas_call(
        _ffn_kernel,
        grid=(m // tm, ff // tf),
        in_specs=[pl.BlockSpec((tm, d), lambda i, f: (i, 0)),
                  pl.BlockSpec((1, d), lambda i, f: (0, 0)),
                  pl.BlockSpec((1, 1, d), lambda i, f: (i // tpb, 0, 0)),
                  pl.BlockSpec((1, 1, d), lambda i, f: (i // tpb, 0, 0)),
                  pl.BlockSpec((d, tf), lambda i, f: (0, f)),
                  pl.BlockSpec((d, tf), lambda i, f: (0, f)),
                  pl.BlockSpec((tf, d), lambda i, f: (f, 0)),
                  pl.BlockSpec((1, d), lambda i, f: (0, 0)),
                  pl.BlockSpec((1, 1, d), lambda i, f: (i // tpb, 0, 0))],
        out_specs=pl.BlockSpec((tm, d), lambda i, f: (i, 0)),
        out_shape=jax.ShapeDtypeStruct((m, d), F32),
        scratch_shapes=[pltpu.VMEM((tm, d), BF16), pltpu.VMEM((tm, d), F32)],
        compiler_params=_cparams(("parallel", "arbitrary")),
        name="ffn",
    )(x2, g.reshape(1, d), sc, sh, w1, w3, w2, gpost.reshape(1, d), gt)


def _router_kernel(x_ref, g_ref, sc_ref, sh_ref, rh_ref, rl_ref, h_ref, lg_ref):
    h = _norm_mod(x_ref[...], g_ref[...], sc_ref[0], sh_ref[0])
    hb = h.astype(BF16)
    h_ref[...] = hb
    hl = (h - hb.astype(F32)).astype(BF16)
    rh = rh_ref[...]
    lg_ref[...] = (jnp.dot(hb, rh, preferred_element_type=F32)
                   + jnp.dot(hl, rh, preferred_element_type=F32)
                   + jnp.dot(hb, rl_ref[...], preferred_element_type=F32))


def router(x2, g, sc, sh, r, rows_per_mod, tm=1024):
    m, d = x2.shape
    e = r.shape[1]
    tm = min(tm, m)
    tpb = rows_per_mod // tm
    rh = r.astype(BF16)
    rl = (r - rh.astype(F32)).astype(BF16)
    return pl.pallas_call(
        _router_kernel,
        grid=(m // tm,),
        in_specs=[pl.BlockSpec((tm, d), lambda i: (i, 0)),
                  pl.BlockSpec((1, d), lambda i: (0, 0)),
                  pl.BlockSpec((1, 1, d), lambda i: (i // tpb, 0, 0)),
                  pl.BlockSpec((1, 1, d), lambda i: (i // tpb, 0, 0)),
                  pl.BlockSpec((d, e), lambda i: (0, 0)),
                  pl.BlockSpec((d, e), lambda i: (0, 0))],
        out_specs=[pl.BlockSpec((tm, d), lambda i: (i, 0)),
                   pl.BlockSpec((tm, e), lambda i: (i, 0))],
        out_shape=[jax.ShapeDtypeStruct((m, d), BF16), jax.ShapeDtypeStruct((m, e), F32)],
        compiler_params=_cparams(("parallel",)),
        name="router",
    )(x2, g.reshape(1, d), sc, sh, rh, rl)


def _expert_kernel(te_ref, hs_ref, w1_ref, w3_ref, w2_ref, o_ref, acc_ref):
    i = pl.program_id(0)
    f = pl.program_id(1)
    nf = pl.num_programs(1)
    valid = te_ref[i] >= 0

    @pl.when(f == 0)
    def _():
        acc_ref[...] = jnp.zeros_like(acc_ref)

    @pl.when(valid)
    def _():
        h = hs_ref[...]
        a = jnp.dot(h, w1_ref[0], preferred_element_type=F32)
        b = jnp.dot(h, w3_ref[0], preferred_element_type=F32)
        act = (a * jax.nn.sigmoid(a) * b).astype(BF16)
        acc_ref[...] += jnp.dot(act, w2_ref[0], preferred_element_type=F32)

    @pl.when(f == nf - 1)
    def _():
        o_ref[...] = acc_ref[...].astype(o_ref.dtype)


def experts(tile_expert, hs, w1, w3, w2, tm=512, tf=1792):
    mp, d = hs.shape
    ff = w1.shape[2]

    def wmap(i, f, te):
        return (jnp.maximum(te[i], 0), 0, f)

    def w2map(i, f, te):
        return (jnp.maximum(te[i], 0), f, 0)

    return pl.pallas_call(
        _expert_kernel,
        grid_spec=pltpu.PrefetchScalarGridSpec(
            num_scalar_prefetch=1,
            grid=(mp // tm, ff // tf),
            in_specs=[pl.BlockSpec((tm, d), lambda i, f, te: (i, 0)),
                      pl.BlockSpec((1, d, tf), wmap),
                      pl.BlockSpec((1, d, tf), wmap),
                      pl.BlockSpec((1, tf, d), w2map)],
            out_specs=pl.BlockSpec((tm, d), lambda i, f, te: (i, 0)),
            scratch_shapes=[pltpu.VMEM((tm, d), F32)]),
        out_shape=jax.ShapeDtypeStruct((mp, d), BF16),
        compiler_params=_cparams(("parallel", "arbitrary")),
        name="experts",
    )(tile_expert, hs, w1, w3, w2)


def _combine_kernel(y1_ref, y2_ref, cw_ref, x_ref, gp_ref, gt_ref, o_ref):
    cw = cw_ref[...]
    y = cw[:, 0:1] * y1_ref[...].astype(F32) + cw[:, 1:2] * y2_ref[...].astype(F32)
    o_ref[...] = x_ref[...] + gt_ref[0] * _rms(y, gp_ref[...])


def combine(y1, y2, cw, x2, gpost, gt, rows_per_mod, tm=1024):
    m, d = x2.shape
    tm = min(tm, m)
    tpb = rows_per_mod // tm
    return pl.pallas_call(
        _combine_kernel,
        grid=(m // tm,),
        in_specs=[pl.BlockSpec((tm, d), lambda i: (i, 0)),
                  pl.BlockSpec((tm, d), lambda i: (i, 0)),
                  pl.BlockSpec((tm, TOP_K), lambda i: (i, 0)),
                  pl.BlockSpec((tm, d), lambda i: (i, 0)),
                  pl.BlockSpec((1, d), lambda i: (0, 0)),
                  pl.BlockSpec((1, 1, d), lambda i: (i // tpb, 0, 0))],
        out_specs=pl.BlockSpec((tm, d), lambda i: (i, 0)),
        out_shape=jax.ShapeDtypeStruct((m, d), F32),
        compiler_params=_cparams(("parallel",)),
        name="combine",
    )(y1, y2, cw, x2, gpost.reshape(1, d), gt)


def moe(x2, g, sc, sh, r, w1, w3, w2, gpost, gt, rows_per_mod, tm=512):
    m, d = x2.shape
    e = r.shape[1]
    h, logits = router(x2, g, sc, sh, r, rows_per_mod)
    top_v, top_i = lax.top_k(logits, TOP_K)
    top_w = jax.nn.softmax(top_v, axis=-1)
    flat_e = top_i.reshape(-1)
    onehot = (flat_e[:, None] == jnp.arange(e)[None, :]).astype(jnp.int32)
    rank = jnp.cumsum(onehot, axis=0) - onehot
    counts = jnp.sum(onehot, axis=0)
    padded = ((counts + tm - 1) // tm) * tm
    pad_end = jnp.cumsum(padded)
    pad_start = pad_end - padded
    slot = pad_start[flat_e] + jnp.sum(rank * onehot, axis=1)
    mp = m * TOP_K + e * tm
    token = jnp.arange(m * TOP_K, dtype=jnp.int32) // TOP_K
    slot_token = jnp.zeros((mp,), jnp.int32).at[slot].set(token)
    n_tiles = mp // tm
    tile_start = jnp.arange(n_tiles, dtype=jnp.int32) * tm
    tile_e = jnp.sum((tile_start[:, None] >= pad_end[None, :]).astype(jnp.int32), axis=1)
    tile_e = jnp.where(tile_start < pad_end[-1], tile_e, -1).astype(jnp.int32)
    hs = jnp.take(h, slot_token, axis=0)
    ys = experts(tile_e, hs, w1, w3, w2, tm=tm)
    slot2 = slot.reshape(m, TOP_K)
    y1 = jnp.take(ys, slot2[:, 0], axis=0)
    y2 = jnp.take(ys, slot2[:, 1], axis=0)
    return combine(y1, y2, top_w, x2, gpost, gt, rows_per_mod)


def rope_axis(x, pos):
    half = x.shape[-1] // 2
    inv = ROPE_BASE ** (-jnp.arange(half, dtype=jnp.float32) / half)
    ang = pos.astype(jnp.float32)[:, None] * inv
    cos = jnp.cos(ang)[None, :, None, :].astype(x.dtype)
    sin = jnp.sin(ang)[None, :, None, :].astype(x.dtype)
    a, b = x[..., :half], x[..., half:]
    return jnp.concatenate([a * cos - b * sin, a * sin + b * cos], axis=-1)


def rope_2d(x, rows, cols):
    h = x.shape[-1] // 2
    return jnp.concatenate([rope_axis(x[..., :h], rows), rope_axis(x[..., h:], cols)], axis=-1)


def softmax_with_sink(s, sink):
    m = jnp.maximum(jnp.max(s, axis=-1, keepdims=True), sink)
    p = jnp.exp(s - m)
    return p / (jnp.sum(p, axis=-1, keepdims=True) + jnp.exp(sink - m))


def window_attn_latent(q, k, v, kc, vc, sink):
    B, L = q.shape[0], q.shape[1]
    nb = L // A_BLOCK
    span = A_BLOCK + 2 * WINDOW
    scale = HEAD_DIM ** -0.5
    pad = ((0, 0), (WINDOW, WINDOW), (0, 0), (0, 0))
    kp, vp = jnp.pad(k, pad), jnp.pad(v, pad)
    qg = q.reshape(B, L, A_KV_HEADS, A_GROUP, HEAD_DIM)
    sink_g = sink.astype(jnp.float32).reshape(1, A_KV_HEADS, A_GROUP, 1, 1)

    def block(i):
        q0 = i * A_BLOCK
        qb = lax.dynamic_slice_in_dim(qg, q0, A_BLOCK, axis=1)
        kb = lax.dynamic_slice_in_dim(kp, q0, span, axis=1)
        vb = lax.dynamic_slice_in_dim(vp, q0, span, axis=1)
        s_loc = jnp.einsum('bqgrd,bkgd->bgrqk', qb, kb).astype(jnp.float32) * scale
        s_ctx = jnp.einsum('bqgrd,bcgd->bgrqc', qb, kc).astype(jnp.float32) * scale
        qpos = q0 + jnp.arange(A_BLOCK)
        kpos = q0 - WINDOW + jnp.arange(span)
        valid = ((kpos[None, :] >= 0) & (kpos[None, :] < L)
                 & (jnp.abs(qpos[:, None] - kpos[None, :]) <= WINDOW))
        s_loc = jnp.where(valid, s_loc, NEG)
        p = softmax_with_sink(jnp.concatenate([s_loc, s_ctx], axis=-1), sink_g).astype(v.dtype)
        o = (jnp.einsum('bgrqk,bkgd->bqgrd', p[..., :span], vb)
             + jnp.einsum('bgrqc,bcgd->bqgrd', p[..., span:], vc))
        return o.reshape(B, A_BLOCK, A_Q_W)

    out = lax.map(block, jnp.arange(nb))
    return out.transpose(1, 0, 2, 3).reshape(B, L, A_Q_W)


def ctx_attn_a(qc, kc, vc, sink):
    B, Lc = qc.shape[0], qc.shape[1]
    qg = qc.reshape(B, Lc, A_KV_HEADS, A_GROUP, HEAD_DIM)
    s = jnp.einsum('bqgrd,bkgd->bgrqk', qg, kc).astype(jnp.float32) * HEAD_DIM ** -0.5
    sink_g = sink.astype(jnp.float32).reshape(1, A_KV_HEADS, A_GROUP, 1, 1)
    p = softmax_with_sink(s, sink_g).astype(vc.dtype)
    return jnp.einsum('bgrqk,bkgd->bqgrd', p, vc).reshape(B, Lc, A_Q_W)


def neighbourhood_attn_latent(q, k, v, kc, vc, rpb):
    B, L = q.shape[0], q.shape[1]
    rows = L // GRID_W
    wr = min(NA_ROWS, rows)
    scale = HEAD_DIM ** -0.5
    qg = q.reshape(B, rows, GRID_W, B_HEADS, HEAD_DIM)
    kg = k.reshape(B, rows, GRID_W, B_HEADS, HEAD_DIM)
    vg = v.reshape(B, rows, GRID_W, B_HEADS, HEAD_DIM)
    col = jnp.arange(GRID_W)
    col_start = jnp.clip(col - NA_COLS // 2, 0, GRID_W - NA_COLS)
    col_ok = (col[None, :] >= col_start[:, None]) & (col[None, :] < col_start[:, None] + NA_COLS)
    dc_idx = jnp.clip(col[None, :] - col[:, None] + NA_COLS - 1, 0, 2 * NA_COLS - 2)
    rpb_f = rpb.astype(jnp.float32)
    nloc = wr * GRID_W

    def row(r):
        rs = jnp.clip(r - wr // 2, 0, rows - wr)
        qr = lax.dynamic_index_in_dim(qg, r, axis=1, keepdims=False)
        kr = lax.dynamic_slice_in_dim(kg, rs, wr, axis=1)
        vr = lax.dynamic_slice_in_dim(vg, rs, wr, axis=1)
        dr_idx = rs + jnp.arange(wr) - r + NA_ROWS - 1
        bias = rpb_f[:, dr_idx[None, :, None], dc_idx[:, None, :]]
        s_loc = jnp.einsum('bqhd,bjkhd->bhqjk', qr, kr).astype(jnp.float32) * scale + bias
        s_loc = jnp.where(col_ok[:, None, :], s_loc, NEG).reshape(B, B_HEADS, GRID_W, nloc)
        s_ctx = jnp.einsum('bqhd,bchd->bhqc', qr, kc).astype(jnp.float32) * scale
        p = jax.nn.softmax(jnp.concatenate([s_loc, s_ctx], axis=-1), axis=-1).astype(v.dtype)
        p_loc = p[..., :nloc].reshape(B, B_HEADS, GRID_W, wr, GRID_W)
        o = (jnp.einsum('bhqjk,bjkhd->bqhd', p_loc, vr)
             + jnp.einsum('bhqc,bchd->bqhd', p[..., nloc:], vc))
        return o.reshape(B, GRID_W, B_W)

    out = lax.map(row, jnp.arange(rows))
    return out.transpose(1, 0, 2, 3).reshape(B, L, B_W)


def ctx_attn_b(qc, kc, vc):
    B, Lc = qc.shape[0], qc.shape[1]
    s = jnp.einsum('bqhd,bkhd->bhqk', qc, kc).astype(jnp.float32) * HEAD_DIM ** -0.5
    p = jax.nn.softmax(s, axis=-1).astype(vc.dtype)
    return jnp.einsum('bhqk,bkhd->bqhd', p, vc).reshape(B, Lc, B_W)


def short_conv(u, w, b):
    L = u.shape[1]
    up = jnp.pad(u, ((0, 0), (1, 1), (0, 0)))
    return up[:, :L] * w[0] + up[:, 1:L + 1] * w[1] + up[:, 2:] * w[2] + b


def hyena_filters(L, w1, b1, w2, b2, w3, freq):
    t = jnp.linspace(0.0, 1.0, L, dtype=jnp.float32)[:, None]
    omega = 2.0 * math.pi * jnp.arange(L, dtype=jnp.float32)[:, None] / L
    f = jnp.linspace(1e-4, HY_BANDS - 1, HY_BANDS, dtype=jnp.float32)[None, :]
    z = jnp.concatenate([t, jnp.cos(f * omega), -jnp.sin(f * omega)], axis=-1).astype(w1.dtype)
    h = jnp.sin(freq * (z @ w1 + b1))
    h = jnp.sin(freq * (h @ w2 + b2))
    h = (h @ w3).astype(jnp.float32).reshape(L, HY_DIRS, HY_ORDER, C_WIDTH)
    deltas = jnp.abs(jnp.linspace(math.log(HY_TARGET) / HY_SLOW, math.log(HY_TARGET) / HY_FAST,
                                  C_WIDTH, dtype=jnp.float32))
    h = h * jnp.exp(-t[:, :, None, None] * deltas)
    return h / jnp.sqrt(jnp.sum(h * h, axis=0, keepdims=True) + EPS)


def fft_conv(u, h):
    L = u.shape[1]
    n = 2 * L
    U = jnp.fft.rfft(u, n=n, axis=1)
    H = jnp.fft.rfft(h, n=n, axis=0)
    return jnp.fft.irfft(U * H[None], n=n, axis=1)[:, :L]


def bidir_long_conv(u, h, bias):
    return fft_conv(u, h[:, 0]) + fft_conv(u[:, ::-1], h[:, 1])[:, ::-1] + u * bias


def hyena(p, conv_w, conv_b, filt, hy_bias):
    v, x1, x2 = jnp.split(short_conv(p, conv_w, conv_b), 3, axis=-1)
    z = v.astype(jnp.float32)
    for o, gate in enumerate((x1, x2)):
        z = gate.astype(jnp.float32) * bidir_long_conv(z, filt[:, :, o], hy_bias[o].astype(jnp.float32))
    return z.astype(p.dtype)


def _rope_partner(width):
    j = np.arange(width)
    return np.where((j % 32) < 16, j + 16, j - 16)


def prep_w_in(w):
    ak, av, bk, bv, aq, bq, cin, gates = jnp.split(w, SPLITS, axis=-1)
    aqr = aq[:, _rope_partner(A_Q_W)]
    akr = ak[:, _rope_partner(A_KV_W)]
    pad = jnp.zeros((w.shape[0], P_COLS - OFF_AV - A_KV_W), w.dtype)
    return jnp.concatenate([gates, cin, aq, aqr, bq, bk, bv, ak, akr, av, pad], axis=-1).astype(BF16)


def kernel(x, c, ctx, c_ctx, w_mod, b_mod, g_mix_pre, g_mix_post, g_ffn_pre, g_ffn_post,
           w_in, sink_a, rpb_b, conv_c_w, conv_c_b, hy_w1, hy_b1, hy_w2, hy_b2, hy_w3,
           hy_freq, hy_bias, w_branch, w_out, ffn_w1, ffn_w3, ffn_w2,
           moe_router, moe_w1, moe_w3, moe_w2):
    B, L, D = x.shape
    Lc = ctx.shape[1]
    N, Nc = B * L, B * Lc
    x2 = x.reshape(N, D)
    ctx2 = ctx.reshape(Nc, D)
    cs = jnp.concatenate([c, c_ctx[None, :], jnp.zeros((7, D), F32)], axis=0)
    t = jnp.arange(L)
    pos_r, pos_c = t // GRID_W, t % GRID_W

    for l in range(DEPTH):
        last = l == DEPTH - 1
        mod = modulation(cs, w_mod[l], b_mod[l])
        mods = [mod[:B, k * D:(k + 1) * D].reshape(B, 1, D) for k in range(6)]
        modc = [mod[B:B + 1, k * D:(k + 1) * D].reshape(1, 1, D) for k in range(6)]
        sh_m, sc_m, gt_m, sh_f, sc_f, gt_f = mods
        shc_m, scc_m, gtc_m, shc_f, scc_f, gtc_f = modc

        w_all = prep_w_in(w_in[l])
        p = proj(x2, g_mix_pre[l], sc_m, sh_m, w_all, L)
        if last:
            pc = proj(ctx2, g_mix_pre[l], scc_m, shc_m, w_all[:, KV_OFF:], Nc)
            coff = KV_OFF
        else:
            pc = proj(ctx2, g_mix_pre[l], scc_m, shc_m, w_all, Nc)
            coff = 0

        def col(a, off, width, base=0):
            return a[:, off - base:off - base + width].astype(F32)

        aq = rope_2d(col(p, OFF_AQ, A_Q_W).reshape(B, L, A_HEADS, HEAD_DIM), pos_r, pos_c)
        ak = rope_2d(col(p, OFF_AK, A_KV_W).reshape(B, L, A_KV_HEADS, HEAD_DIM), pos_r, pos_c)
        av = col(p, OFF_AV, A_KV_W).reshape(B, L, A_KV_HEADS, HEAD_DIM)
        bq = col(p, OFF_BQ, B_W).reshape(B, L, B_HEADS, HEAD_DIM)
        bk = col(p, OFF_BK, B_W).reshape(B, L, B_HEADS, HEAD_DIM)
        bv = col(p, OFF_BV, B_W).reshape(B, L, B_HEADS, HEAD_DIM)
        cin = col(p, OFF_CIN, 3 * C_WIDTH).reshape(B, L, 3 * C_WIDTH)
        akc = col(pc, OFF_AK, A_KV_W, coff).reshape(B, Lc, A_KV_HEADS, HEAD_DIM)
        avc = col(pc, OFF_AV, A_KV_W, coff).reshape(B, Lc, A_KV_HEADS, HEAD_DIM)
        bkc = col(pc, OFF_BK, B_W, coff).reshape(B, Lc, B_HEADS, HEAD_DIM)
        bvc = col(pc, OFF_BV, B_W, coff).reshape(B, Lc, B_HEADS, HEAD_DIM)

        ya = window_attn_latent(aq, ak, av, akc, avc, sink_a[l])
        yb = neighbourhood_attn_latent(bq, bk, bv, bkc, bvc, rpb_b[l])
        filt = hyena_filters(L, hy_w1[l], hy_b1[l], hy_w2[l], hy_b2[l], hy_w3[l], hy_freq[l])
        yc = hyena(cin, conv_c_w[l], conv_c_b[l], filt, hy_bias[l])
        wb = w_branch[l].astype(BF16)
        wo = w_out[l].astype(BF16)
        x2 = merge(ya.reshape(N, -1).astype(BF16), yb.reshape(N, -1).astype(BF16),
                   yc.reshape(N, -1).astype(BF16), p, wb, wo, x2, g_mix_post[l], gt_m, L)
        if not last:
            aqc = col(pc, OFF_AQ, A_Q_W).reshape(B, Lc, A_HEADS, HEAD_DIM)
            bqc = col(pc, OFF_BQ, B_W).reshape(B, Lc, B_HEADS, HEAD_DIM)
            cinc = col(pc, OFF_CIN, 3 * C_WIDTH).reshape(B, Lc, 3 * C_WIDTH)
            yac = ctx_attn_a(aqc, akc, avc, sink_a[l])
            ybc = ctx_attn_b(bqc, bkc, bvc)
            filt_c = hyena_filters(Lc, hy_w1[l], hy_b1[l], hy_w2[l], hy_b2[l], hy_w3[l], hy_freq[l])
            ycc = hyena(cinc, conv_c_w[l], conv_c_b[l], filt_c, hy_bias[l])
            ctx2 = merge(yac.reshape(Nc, -1).astype(BF16), ybc.reshape(Nc, -1).astype(BF16),
                         ycc.reshape(Nc, -1).astype(BF16), pc, wb, wo, ctx2, g_mix_post[l], gtc_m, Nc)

        j = l // 2
        if l % 2 == 0:
            w1, w3, w2 = ffn_w1[j].astype(BF16), ffn_w3[j].astype(BF16), ffn_w2[j].astype(BF16)
            x2 = ffn(x2, g_ffn_pre[l], sc_f, sh_f, w1, w3, w2, g_ffn_post[l], gt_f, L)
            if not last:
                ctx2 = ffn(ctx2, g_ffn_pre[l], scc_f, shc_f, w1, w3, w2, g_ffn_post[l], gtc_f, Nc)
        else:
            w1, w3, w2 = moe_w1[j].astype(BF16), moe_w3[j].astype(BF16), moe_w2[j].astype(BF16)
            x2 = moe(x2, g_ffn_pre[l], sc_f, sh_f, moe_router[j], w1, w3, w2, g_ffn_post[l], gt_f, L)
            if not last:
                ctx2 = moe(ctx2, g_ffn_pre[l], scc_f, shc_f, moe_router[j], w1, w3, w2,
                           g_ffn_post[l], gtc_f, Nc)
    return x2.reshape(B, L, D)
```

```python
import functools
import math

import jax
import jax.numpy as jnp
import numpy as np
from jax import lax
from jax.experimental import pallas as pl
from jax.experimental.pallas import tpu as pltpu

F32 = jnp.float32
BF16 = jnp.bfloat16

D_MODEL = 1024
DEPTH = 2
GRID_W = 64
HEAD_DIM = 64
A_HEADS = 8
A_KV_HEADS = 2
A_GROUP = A_HEADS // A_KV_HEADS
WINDOW = 128
A_BLOCK = 128
B_HEADS = 8
NA_ROWS = 8
NA_COLS = 16
C_WIDTH = 512
HY_ORDER = 2
HY_DIRS = 2
HY_BANDS = 16
HY_TARGET = 1e-2
HY_FAST = 0.3
HY_SLOW = 1.5
N_BRANCH = 3
N_EXPERTS = 8
TOP_K = 2
ROPE_BASE = 10000.0
EPS = 1e-6
NEG = -1e30

A_KV_W = A_KV_HEADS * HEAD_DIM
A_Q_W = A_HEADS * HEAD_DIM
B_W = B_HEADS * HEAD_DIM
KV_COLS = 2 * A_KV_W + 2 * B_W
SPLITS = (A_KV_W, 2 * A_KV_W, 2 * A_KV_W + B_W, KV_COLS,
          KV_COLS + A_Q_W, KV_COLS + A_Q_W + B_W,
          KV_COLS + A_Q_W + B_W + 3 * C_WIDTH)

OFF_GATES = 0
OFF_CIN = 3072
OFF_AQ = 4608
OFF_AQR = 5120
OFF_BQ = 5632
OFF_BK = 6144
OFF_BV = 6656
OFF_AK = 7168
OFF_AKR = 7296
OFF_AV = 7424
P_COLS = 7680
KV_OFF = OFF_BK
KV_W = P_COLS - KV_OFF

VMEM_LIMIT = 48 * 1024 * 1024


def _cparams(sem):
    return pltpu.CompilerParams(dimension_semantics=sem, vmem_limit_bytes=VMEM_LIMIT)


def _rms(y, g):
    return y * lax.rsqrt(jnp.mean(y * y, axis=-1, keepdims=True) + EPS) * g


def _norm_mod(x, g, sc, sh):
    return _rms(x, g) * (1.0 + sc) + sh


def _mod_kernel(c_ref, w_ref, b_ref, o_ref):
    c = c_ref[...]
    s = c * jax.nn.sigmoid(c)
    o_ref[...] = jnp.dot(s, w_ref[...], preferred_element_type=F32,
                         precision=lax.Precision.HIGHEST) + b_ref[...]


def modulation(cs, w, b, tn=512):
    m, d = cs.shape
    n = w.shape[1]
    return pl.pallas_call(
        _mod_kernel,
        grid=(n // tn,),
        in_specs=[pl.BlockSpec((m, d), lambda j: (0, 0)),
                  pl.BlockSpec((d, tn), lambda j: (0, j)),
                  pl.BlockSpec((1, tn), lambda j: (0, j))],
        out_specs=pl.BlockSpec((m, tn), lambda j: (0, j)),
        out_shape=jax.ShapeDtypeStruct((m, n), F32),
        compiler_params=_cparams(("parallel",)),
        name="modulation",
    )(cs, w, b.reshape(1, n))


def _proj_kernel(x_ref, g_ref, sc_ref, sh_ref, w_ref, o_ref, h_ref):
    @pl.when(pl.program_id(1) == 0)
    def _():
        h_ref[...] = _norm_mod(x_ref[...], g_ref[...], sc_ref[0], sh_ref[0]).astype(BF16)

    o_ref[...] = jnp.dot(h_ref[...], w_ref[...], preferred_element_type=F32).astype(o_ref.dtype)


def proj(x2, g, sc, sh, w, rows_per_mod, tm=1024, tn=512):
    m, d = x2.shape
    n = w.shape[1]
    tm = min(tm, m)
    tpb = rows_per_mod // tm
    return pl.pallas_call(
        _proj_kernel,
        grid=(m // tm, n // tn),
        in_specs=[pl.BlockSpec((tm, d), lambda i, j: (i, 0)),
                  pl.BlockSpec((1, d), lambda i, j: (0, 0)),
                  pl.BlockSpec((1, 1, d), lambda i, j: (i // tpb, 0, 0)),
                  pl.BlockSpec((1, 1, d), lambda i, j: (i // tpb, 0, 0)),
                  pl.BlockSpec((d, tn), lambda i, j: (0, j))],
        out_specs=pl.BlockSpec((tm, tn), lambda i, j: (i, j)),
        out_shape=jax.ShapeDtypeStruct((m, n), BF16),
        scratch_shapes=[pltpu.VMEM((tm, d), BF16)],
        compiler_params=_cparams(("parallel", "arbitrary")),
        name="proj",
    )(x2, g.reshape(1, d), sc, sh, w)


def _merge_kernel(ya_ref, yb_ref, yc_ref, ga_ref, gb_ref, gc_ref, wb_ref, wo_ref,
                  x_ref, gp_ref, gt_ref, o_ref):
    def branch(y_ref, g_ref, k):
        t = jnp.dot(y_ref[...], wb_ref[k], preferred_element_type=F32)
        return jax.nn.sigmoid(g_ref[...].astype(F32)) * t

    m = branch(ya_ref, ga_ref, 0) + branch(yb_ref, gb_ref, 1) + branch(yc_ref, gc_ref, 2)
    y = jnp.dot(m.astype(BF16), wo_ref[...], preferred_element_type=F32)
    o_ref[...] = x_ref[...] + gt_ref[0] * _rms(y, gp_ref[...])


def merge(ya, yb, yc, p, wb, wo, x2, gpost, gt, rows_per_mod, tm=512):
    m, d = x2.shape
    bw = ya.shape[1]
    tm = min(tm, m)
    tpb = rows_per_mod // tm
    gblk = OFF_GATES // d
    return pl.pallas_call(
        _merge_kernel,
        grid=(m // tm,),
        in_specs=[pl.BlockSpec((tm, bw), lambda i: (i, 0)),
                  pl.BlockSpec((tm, bw), lambda i: (i, 0)),
                  pl.BlockSpec((tm, bw), lambda i: (i, 0)),
                  pl.BlockSpec((tm, d), lambda i: (i, gblk)),
                  pl.BlockSpec((tm, d), lambda i: (i, gblk + 1)),
                  pl.BlockSpec((tm, d), lambda i: (i, gblk + 2)),
                  pl.BlockSpec((N_BRANCH, bw, d), lambda i: (0, 0, 0)),
                  pl.BlockSpec((d, d), lambda i: (0, 0)),
                  pl.BlockSpec((tm, d), lambda i: (i, 0)),
                  pl.BlockSpec((1, d), lambda i: (0, 0)),
                  pl.BlockSpec((1, 1, d), lambda i: (i // tpb, 0, 0))],
        out_specs=pl.BlockSpec((tm, d), lambda i: (i, 0)),
        out_shape=jax.ShapeDtypeStruct((m, d), F32),
        compiler_params=_cparams(("parallel",)),
        name="merge",
    )(ya, yb, yc, p, p, p, wb, wo, x2, gpost.reshape(1, d), gt)


def _ffn_kernel(x_ref, g_ref, sc_ref, sh_ref, w1_ref, w3_ref, w2_ref, gp_ref, gt_ref,
                o_ref, h_ref, acc_ref):
    f = pl.program_id(1)

    @pl.when(f == 0)
    def _():
        h_ref[...] = _norm_mod(x_ref[...], g_ref[...], sc_ref[0], sh_ref[0]).astype(BF16)
        acc_ref[...] = jnp.zeros_like(acc_ref)

    h = h_ref[...]
    a = jnp.dot(h, w1_ref[...], preferred_element_type=F32)
    b = jnp.dot(h, w3_ref[...], preferred_element_type=F32)
    act = (a * jax.nn.sigmoid(a) * b).astype(BF16)
    acc_ref[...] += jnp.dot(act, w2_ref[...], preferred_element_type=F32)

    @pl.when(f == pl.num_programs(1) - 1)
    def _():
        o_ref[...] = x_ref[...] + gt_ref[0] * _rms(acc_ref[...], gp_ref[...])


def ffn(x2, g, sc, sh, w1, w3, w2, gpost, gt, rows_per_mod, tm=512, tf=1408):
    m, d = x2.shape
    ff = w1.shape[1]
    tm = min(tm, m)
    tpb = rows_per_mod // tm
    return pl.pallas_call(
        _ffn_kernel,
        grid=(m // tm, ff // tf),
        in_specs=[pl.BlockSpec((tm, d), lambda i, f: (i, 0)),
                  pl.BlockSpec((1, d), lambda i, f: (0, 0)),
                  pl.BlockSpec((1, 1, d), lambda i, f: (i // tpb, 0, 0)),
                  pl.BlockSpec((1, 1, d), lambda i, f: (i // tpb, 0, 0)),
                  pl.BlockSpec((d, tf), lambda i, f: (0, f)),
                  pl.BlockSpec((d, tf), lambda i, f: (0, f)),
                  pl.BlockSpec((tf, d), lambda i, f: (f, 0)),
                  pl.BlockSpec((1, d), lambda i, f: (0, 0)),
                  pl.BlockSpec((1, 1, d), lambda i, f: (i // tpb, 0, 0))],
        out_specs=pl.BlockSpec((tm, d), lambda i, f: (i, 0)),
        out_shape=jax.ShapeDtypeStruct((m, d), F32),
        scratch_shapes=[pltpu.VMEM((tm, d), BF16), pltpu.VMEM((tm, d), F32)],
        compiler_params=_cparams(("parallel", "arbitrary")),
        name="ffn",
    )(x2, g.reshape(1, d), sc, sh, w1, w3, w2, gpost.reshape(1, d), gt)


def _router_kernel(x_ref, g_ref, sc_ref, sh_ref, rh_ref, rl_ref, h_ref, lg_ref):
    h = _norm_mod(x_ref[...], g_ref[...], sc_ref[0], sh_ref[0])
    hb = h.astype(BF16)
    h_ref[...] = hb
    hl = (h - hb.astype(F32)).astype(BF16)
    rh = rh_ref[...]
    lg_ref[...] = (jnp.dot(hb, rh, preferred_element_type=F32)
                   + jnp.dot(hl, rh, preferred_element_type=F32)
                   + jnp.dot(hb, rl_ref[...], preferred_element_type=F32))


def router(x2, g, sc, sh, r, rows_per_mod, tm=1024):
    m, d = x2.shape
    e = r.shape[1]
    tm = min(tm, m)
    tpb = rows_per_mod // tm
    rh = r.astype(BF16)
    rl = (r - rh.astype(F32)).astype(BF16)
    return pl.pallas_call(
        _router_kernel,
        grid=(m // tm,),
        in_specs=[pl.BlockSpec((tm, d), lambda i: (i, 0)),
                  pl.BlockSpec((1, d), lambda i: (0, 0)),
                  pl.BlockSpec((1, 1, d), lambda i: (i // tpb, 0, 0)),
                  pl.BlockSpec((1, 1, d), lambda i: (i // tpb, 0, 0)),
                  pl.BlockSpec((d, e), lambda i: (0, 0)),
                  pl.BlockSpec((d, e), lambda i: (0, 0))],
        out_specs=[pl.BlockSpec((tm, d), lambda i: (i, 0)),
                   pl.BlockSpec((tm, e), lambda i: (i, 0))],
        out_shape=[jax.ShapeDtypeStruct((m, d), BF16), jax.ShapeDtypeStruct((m, e), F32)],
        compiler_params=_cparams(("parallel",)),
        name="router",
    )(x2, g.reshape(1, d), sc, sh, rh, rl)


def _expert_kernel(te_ref, hs_ref, w1_ref, w3_ref, w2_ref, o_ref, acc_ref):
    i = pl.program_id(0)
    f = pl.program_id(1)
    nf = pl.num_programs(1)
    valid = te_ref[i] >= 0

    @pl.when(f == 0)
    def _():
        acc_ref[...] = jnp.zeros_like(acc_ref)

    @pl.when(valid)
    def _():
        h = hs_ref[...]
        a = jnp.dot(h, w1_ref[0], preferred_element_type=F32)
        b = jnp.dot(h, w3_ref[0], preferred_element_type=F32)
        act = (a * jax.nn.sigmoid(a) * b).astype(BF16)
        acc_ref[...] += jnp.dot(act, w2_ref[0], preferred_element_type=F32)

    @pl.when(f == nf - 1)
    def _():
        o_ref[...] = acc_ref[...].astype(o_ref.dtype)


def experts(tile_expert, hs, w1, w3, w2, tm=512, tf=1792):
    mp, d = hs.shape
    ff = w1.shape[2]

    def wmap(i, f, te):
        return (jnp.maximum(te[i], 0), 0, f)

    def w2map(i, f, te):
        return (jnp.maximum(te[i], 0), f, 0)

    return pl.pallas_call(
        _expert_kernel,
        grid_spec=pltpu.PrefetchScalarGridSpec(
            num_scalar_prefetch=1,
            grid=(mp // tm, ff // tf),
            in_specs=[pl.BlockSpec((tm, d), lambda i, f, te: (i, 0)),
                      pl.BlockSpec((1, d, tf), wmap),
                      pl.BlockSpec((1, d, tf), wmap),
                      pl.BlockSpec((1, tf, d), w2map)],
            out_specs=pl.BlockSpec((tm, d), lambda i, f, te: (i, 0)),
            scratch_shapes=[pltpu.VMEM((tm, d), F32)]),
        out_shape=jax.ShapeDtypeStruct((mp, d), BF16),
        compiler_params=_cparams(("parallel", "arbitrary")),
        name="experts",
    )(tile_expert, hs, w1, w3, w2)


def _combine_kernel(y1_ref, y2_ref, cw_ref, x_ref, gp_ref, gt_ref, o_ref):
    cw = cw_ref[...]
    y = cw[:, 0:1] * y1_ref[...].astype(F32) + cw[:, 1:2] * y2_ref[...].astype(F32)
    o_ref[...] = x_ref[...] + gt_ref[0] * _rms(y, gp_ref[...])


def combine(y1, y2, cw, x2, gpost, gt, rows_per_mod, tm=1024):
    m, d = x2.shape
    tm = min(tm, m)
    tpb = rows_per_mod // tm
    return pl.pallas_call(
        _combine_kernel,
        grid=(m // tm,),
        in_specs=[pl.BlockSpec((tm, d), lambda i: (i, 0)),
                  pl.BlockSpec((tm, d), lambda i: (i, 0)),
                  pl.BlockSpec((tm, TOP_K), lambda i: (i, 0)),
                  pl.BlockSpec((tm, d), lambda i: (i, 0)),
                  pl.BlockSpec((1, d), lambda i: (0, 0)),
                  pl.BlockSpec((1, 1, d), lambda i: (i // tpb, 0, 0))],
        out_specs=pl.BlockSpec((tm, d), lambda i: (i, 0)),
        out_shape=jax.ShapeDtypeStruct((m, d), F32),
        compiler_params=_cparams(("parallel",)),
        name="combine",
    )(y1, y2, cw, x2, gpost.reshape(1, d), gt)


def moe(x2, g, sc, sh, r, w1, w3, w2, gpost, gt, rows_per_mod, tm=512):
    m, d = x2.shape
    e = r.shape[1]
    h, logits = router(x2, g, sc, sh, r, rows_per_mod)
    top_v, top_i = lax.top_k(logits, TOP_K)
    top_w = jax.nn.softmax(top_v, axis=-1)
    flat_e = top_i.reshape(-1)
    onehot = (flat_e[:, None] == jnp.arange(e)[None, :]).astype(jnp.int32)
    rank = jnp.cumsum(onehot, axis=0) - onehot
    counts = jnp.sum(onehot, axis=0)
    padded = ((counts + tm - 1) // tm) * tm
    pad_end = jnp.cumsum(padded)
    pad_start = pad_end - padded
    slot = pad_start[flat_e] + jnp.sum(rank * onehot, axis=1)
    mp = m * TOP_K + e * tm
    token = jnp.arange(m * TOP_K, dtype=jnp.int32) // TOP_K
    slot_token = jnp.zeros((mp,), jnp.int32).at[slot].set(token)
    n_tiles = mp // tm
    tile_start = jnp.arange(n_tiles, dtype=jnp.int32) * tm
    tile_e = jnp.sum((tile_start[:, None] >= pad_end[None, :]).astype(jnp.int32), axis=1)
    tile_e = jnp.where(tile_start < pad_end[-1], tile_e, -1).astype(jnp.int32)
    hs = jnp.take(h, slot_token, axis=0)
    ys = experts(tile_e, hs, w1, w3, w2, tm=tm)
    slot2 = slot.reshape(m, TOP_K)
    y1 = jnp.take(ys, slot2[:, 0], axis=0)
    y2 = jnp.take(ys, slot2[:, 1], axis=0)
    return combine(y1, y2, top_w, x2, gpost, gt, rows_per_mod)


def rope_tables(L, width):
    t = np.arange(L)
    j = np.arange(width) % HEAD_DIM
    pos = np.where(j[None, :] < HEAD_DIM // 2, (t // GRID_W)[:, None], (t % GRID_W)[:, None])
    quarter = HEAD_DIM // 4
    inv = ROPE_BASE ** (-(j % quarter).astype(np.float32) / quarter)
    ang = jnp.asarray(pos.astype(np.float32)) * jnp.asarray(inv.astype(np.float32))[None, :]
    sign = np.where((j % (2 * quarter)) < quarter, -1.0, 1.0).astype(np.float32)
    return jnp.cos(ang), jnp.sin(ang) * jnp.asarray(sign)[None, :]


def _rope_kernel(aq_ref, aqr_ref, ak_ref, akr_ref, cq_ref, sq_ref, ck_ref, sk_ref, q_ref, k_ref):
    scale = HEAD_DIM ** -0.5
    q = aq_ref[...].astype(F32) * cq_ref[...] + aqr_ref[...].astype(F32) * sq_ref[...]
    q_ref[...] = (q * scale).astype(BF16)
    k = ak_ref[...].astype(F32) * ck_ref[...] + akr_ref[...].astype(F32) * sk_ref[...]
    k_ref[...] = k.astype(BF16)


def rope_qk(p, L, tm=1024):
    m = p.shape[0]
    tm = min(tm, L)
    nt = L // tm
    cq, sq = rope_tables(L, A_Q_W)
    ck, sk = rope_tables(L, A_KV_W)
    return pl.pallas_call(
        _rope_kernel,
        grid=(m // tm,),
        in_specs=[pl.BlockSpec((tm, A_Q_W), lambda i: (i, OFF_AQ // A_Q_W)),
                  pl.BlockSpec((tm, A_Q_W), lambda i: (i, OFF_AQR // A_Q_W)),
                  pl.BlockSpec((tm, A_KV_W), lambda i: (i, OFF_AK // A_KV_W)),
                  pl.BlockSpec((tm, A_KV_W), lambda i: (i, OFF_AKR // A_KV_W)),
                  pl.BlockSpec((tm, A_Q_W), lambda i: (i % nt, 0)),
                  pl.BlockSpec((tm, A_Q_W), lambda i: (i % nt, 0)),
                  pl.BlockSpec((tm, A_KV_W), lambda i: (i % nt, 0)),
                  pl.BlockSpec((tm, A_KV_W), lambda i: (i % nt, 0))],
        out_specs=[pl.BlockSpec((tm, A_Q_W), lambda i: (i, 0)),
                   pl.BlockSpec((tm, A_KV_W), lambda i: (i, 0))],
        out_shape=[jax.ShapeDtypeStruct((m, A_Q_W), BF16), jax.ShapeDtypeStruct((m, A_KV_W), BF16)],
        compiler_params=_cparams(("parallel",)),
        name="rope_qk",
    )(p, p, p, p, cq, sq, ck, sk)


def _dot_nt(a, b):
    return lax.dot_general(a, b, (((1,), (1,)), ((), ())), preferred_element_type=F32)


A_SPAN = A_BLOCK + 2 * WINDOW


def _wa_kernel(sink_ref, q_ref, k_ref, v_ref, kc_ref, vc_ref, o_ref):
    i = pl.program_id(1)
    L = k_ref.shape[1]
    start = pl.multiple_of(jnp.clip(i * A_BLOCK - WINDOW, 0, L - A_SPAN), A_BLOCK)
    qpos = i * A_BLOCK + lax.broadcasted_iota(jnp.int32, (A_BLOCK, A_SPAN), 0)
    kpos = start + lax.broadcasted_iota(jnp.int32, (A_BLOCK, A_SPAN), 1)
    valid = jnp.abs(qpos - kpos) <= WINDOW
    q = q_ref[0]
    for g in range(A_KV_HEADS):
        ks = slice(g * HEAD_DIM, (g + 1) * HEAD_DIM)
        kg = k_ref[0, pl.ds(start, A_SPAN), ks]
        vg = v_ref[0, pl.ds(start, A_SPAN), ks]
        kcg = kc_ref[0, :, ks]
        vcg = vc_ref[0, :, ks]
        for r in range(A_GROUP):
            h = g * A_GROUP + r
            hs = slice(h * HEAD_DIM, (h + 1) * HEAD_DIM)
            qh = q[:, hs]
            s = jnp.where(valid, _dot_nt(qh, kg), NEG)
            sc = _dot_nt(qh, kcg)
            sink = sink_ref[h]
            m = jnp.maximum(jnp.maximum(jnp.max(s, axis=-1, keepdims=True),
                                        jnp.max(sc, axis=-1, keepdims=True)), sink)
            p = jnp.exp(s - m)
            pc = jnp.exp(sc - m)
            den = (jnp.sum(p, axis=-1, keepdims=True) + jnp.sum(pc, axis=-1, keepdims=True)
                   + jnp.exp(sink - m))
            o = (jnp.dot(p.astype(BF16), vg, preferred_element_type=F32)
                 + jnp.dot(pc.astype(BF16), vcg, preferred_element_type=F32))
            o_ref[0, :, hs] = (o / den).astype(BF16)


def window_attention(qr, kr, p3, pc3, coff, sink):
    B, L, _ = qr.shape
    Lc = pc3.shape[1]
    return pl.pallas_call(
        _wa_kernel,
        grid=(B, L // A_BLOCK),
        in_specs=[pl.BlockSpec(memory_space=pltpu.SMEM),
                  pl.BlockSpec((1, A_BLOCK, A_Q_W), lambda b, i: (b, i, 0)),
                  pl.BlockSpec((1, L, A_KV_W), lambda b, i: (b, 0, 0)),
                  pl.BlockSpec((1, L, A_KV_W), lambda b, i: (b, 0, OFF_AV // A_KV_W)),
                  pl.BlockSpec((1, Lc, A_KV_W), lambda b, i: (b, 0, (OFF_AK - coff) // A_KV_W)),
                  pl.BlockSpec((1, Lc, A_KV_W), lambda b, i: (b, 0, (OFF_AV - coff) // A_KV_W))],
        out_specs=pl.BlockSpec((1, A_BLOCK, A_Q_W), lambda b, i: (b, i, 0)),
        out_shape=jax.ShapeDtypeStruct((B, L, A_Q_W), BF16),
        compiler_params=_cparams(("parallel", "arbitrary")),
        name="window_attention",
    )(sink.astype(F32), qr, kr, p3, pc3, pc3)


NA_QROWS = 4
NA_KROWS = NA_ROWS + NA_QROWS
NA_TQ = NA_QROWS * GRID_W
NA_TK = NA_KROWS * GRID_W


def na_bias_table(rpb, rows):
    col = np.arange(GRID_W)
    col_start = np.clip(col - NA_COLS // 2, 0, GRID_W - NA_COLS)
    col_ok = (col[None, :] >= col_start[:, None]) & (col[None, :] < col_start[:, None] + NA_COLS)
    dc = np.clip(col[None, :] - col[:, None] + NA_COLS - 1, 0, 2 * NA_COLS - 2)
    a = np.arange(NA_QROWS)
    j = np.arange(NA_KROWS)
    dr_all, ok_all = [], []
    for r0, u in ((0, 0), (NA_QROWS, 0), (rows - NA_QROWS, rows - NA_KROWS)):
        r = r0 + a
        rs = np.clip(r - NA_ROWS // 2, 0, rows - NA_ROWS)
        kr = u + j
        ok = (kr[None, :] >= rs[:, None]) & (kr[None, :] < rs[:, None] + NA_ROWS)
        dr = np.clip(kr[None, :] - r[:, None] + NA_ROWS - 1, 0, 2 * NA_ROWS - 2)
        dr_all.append(dr)
        ok_all.append(ok)
    dr = np.stack(dr_all)[:, :, None, :, None]
    ok = np.stack(ok_all)[:, :, None, :, None] & col_ok[None, None, :, None, :]
    dr = np.broadcast_to(dr, ok.shape)
    dcb = np.broadcast_to(dc[None, None, :, None, :], ok.shape)
    vals = rpb.astype(F32)[:, dr, dcb]
    vals = jnp.where(jnp.asarray(ok)[None], vals, NEG)
    return vals.transpose(1, 0, 2, 3, 4, 5).reshape(3, rpb.shape[0], NA_TQ, NA_TK)


def _na_kernel(q_ref, k_ref, v_ref, kc_ref, vc_ref, bias_ref, o_ref):
    blk = pl.program_id(1)
    rows = k_ref.shape[1] // GRID_W
    u = jnp.clip(blk * NA_QROWS - NA_ROWS // 2, 0, rows - NA_KROWS)
    start = pl.multiple_of(u * GRID_W, GRID_W)
    q = q_ref[0] * (HEAD_DIM ** -0.5)
    for h in range(B_HEADS):
        hs = slice(h * HEAD_DIM, (h + 1) * HEAD_DIM)
        qh = q[:, hs]
        kh = k_ref[0, pl.ds(start, NA_TK), hs]
        vh = v_ref[0, pl.ds(start, NA_TK), hs]
        s = _dot_nt(qh, kh) + bias_ref[0, h]
        sc = _dot_nt(qh, kc_ref[0, :, hs])
        m = jnp.maximum(jnp.max(s, axis=-1, keepdims=True), jnp.max(sc, axis=-1, keepdims=True))
        p = jnp.exp(s - m)
        pc = jnp.exp(sc - m)
        den = jnp.sum(p, axis=-1, keepdims=True) + jnp.sum(pc, axis=-1, keepdims=True)
        o = (jnp.dot(p.astype(BF16), vh, preferred_element_type=F32)
             + jnp.dot(pc.astype(BF16), vc_ref[0, :, hs], preferred_element_type=F32))
        o_ref[0, :, hs] = (o / den).astype(BF16)


def neighbourhood_attention(p3, pc3, coff, rpb):
    B, L, _ = p3.shape
    Lc = pc3.shape[1]
    nblk = L // NA_TQ
    bias = na_bias_table(rpb, L // GRID_W)

    def bias_map(b, i):
        return (jnp.where(i == 0, 0, jnp.where(i == nblk - 1, 2, 1)), 0, 0, 0)

    return pl.pallas_call(
        _na_kernel,
        grid=(B, nblk),
        in_specs=[pl.BlockSpec((1, NA_TQ, B_W), lambda b, i: (b, i, OFF_BQ // B_W)),
                  pl.BlockSpec((1, L, B_W), lambda b, i: (b, 0, OFF_BK // B_W)),
                  pl.BlockSpec((1, L, B_W), lambda b, i: (b, 0, OFF_BV // B_W)),
                  pl.BlockSpec((1, Lc, B_W), lambda b, i: (b, 0, (OFF_BK - coff) // B_W)),
                  pl.BlockSpec((1, Lc, B_W), lambda b, i: (b, 0, (OFF_BV - coff) // B_W)),
                  pl.BlockSpec((1, B_HEADS, NA_TQ, NA_TK), bias_map)],
        out_specs=pl.BlockSpec((1, NA_TQ, B_W), lambda b, i: (b, i, 0)),
        out_shape=jax.ShapeDtypeStruct((B, L, B_W), BF16),
        compiler_params=_cparams(("parallel", "arbitrary")),
        name="neighbourhood_attention",
    )(p3, p3, p3, pc3, pc3, bias)


def _ctx_attn_kernel(sink_ref, aq_ref, ak_ref, av_ref, bq_ref, bk_ref, bv_ref, oa_ref, ob_ref):
    scale = HEAD_DIM ** -0.5
    aq = aq_ref[0] * scale
    bq = bq_ref[0] * scale
    for h in range(A_HEADS):
        hs = slice(h * HEAD_DIM, (h + 1) * HEAD_DIM)
        g = h // A_GROUP
        ks = slice(g * HEAD_DIM, (g + 1) * HEAD_DIM)
        s = _dot_nt(aq[:, hs], ak_ref[0, :, ks])
        sink = sink_ref[h]
        m = jnp.maximum(jnp.max(s, axis=-1, keepdims=True), sink)
        p = jnp.exp(s - m)
        den = jnp.sum(p, axis=-1, keepdims=True) + jnp.exp(sink - m)
        o = jnp.dot(p.astype(BF16), av_ref[0, :, ks], preferred_element_type=F32)
        oa_ref[0, :, hs] = (o / den).astype(BF16)
    for h in range(B_HEADS):
        hs = slice(h * HEAD_DIM, (h + 1) * HEAD_DIM)
        s = _dot_nt(bq[:, hs], bk_ref[0, :, hs])
        m = jnp.max(s, axis=-1, keepdims=True)
        p = jnp.exp(s - m)
        den = jnp.sum(p, axis=-1, keepdims=True)
        o = jnp.dot(p.astype(BF16), bv_ref[0, :, hs], preferred_element_type=F32)
        ob_ref[0, :, hs] = (o / den).astype(BF16)


def ctx_attention(pc3, sink):
    B, Lc, _ = pc3.shape
    return pl.pallas_call(
        _ctx_attn_kernel,
        grid=(B,),
        in_specs=[pl.BlockSpec(memory_space=pltpu.SMEM),
                  pl.BlockSpec((1, Lc, A_Q_W), lambda b: (b, 0, OFF_AQ // A_Q_W)),
                  pl.BlockSpec((1, Lc, A_KV_W), lambda b: (b, 0, OFF_AK // A_KV_W)),
                  pl.BlockSpec((1, Lc, A_KV_W), lambda b: (b, 0, OFF_AV // A_KV_W)),
                  pl.BlockSpec((1, Lc, B_W), lambda b: (b, 0, OFF_BQ // B_W)),
                  pl.BlockSpec((1, Lc, B_W), lambda b: (b, 0, OFF_BK // B_W)),
                  pl.BlockSpec((1, Lc, B_W), lambda b: (b, 0, OFF_BV // B_W))],
        out_specs=[pl.BlockSpec((1, Lc, A_Q_W), lambda b: (b, 0, 0)),
                   pl.BlockSpec((1, Lc, B_W), lambda b: (b, 0, 0))],
        out_shape=[jax.ShapeDtypeStruct((B, Lc, A_Q_W), BF16), jax.ShapeDtypeStruct((B, Lc, B_W), BF16)],
        compiler_params=_cparams(("parallel",)),
        name="ctx_attention",
    )(sink.astype(F32), pc3, pc3, pc3, pc3, pc3, pc3)


HY_N1 = 128
HY_SUB = 8
HY_CB = 256


def hyena_mats(L):
    nf = 2 * L
    n2f = nf // HY_N1
    n2h = n2f // 2
    eye = np.eye(HY_SUB)

    def real_block(m):
        return np.block([[m.real, -m.imag], [m.imag, m.real]])

    f = np.exp(-2j * np.pi * np.outer(np.arange(n2f), np.arange(n2h)) / n2f)
    ms1 = real_block(np.kron(f, eye))
    fi = np.exp(2j * np.pi * np.outer(np.arange(n2h), np.arange(n2f)) / n2f) / nf
    ms2 = real_block(np.kron(fi, eye))
    k1 = np.arange(HY_N1)
    n1 = np.arange(HY_N1)
    e1, e2 = [], []
    for k2 in range(n2f):
        e = np.exp(-2j * np.pi * np.outer(n2f * k1 + k2, n1) / nf)
        e1.append(real_block(e))
        e2.append(real_block(np.conj(e).T))
    to = lambda m: jnp.asarray(np.asarray(m, np.float32), dtype=BF16)
    return to(ms1), to(np.stack(e1)), to(np.stack(e2)), to(ms2)


def hyena_spectrum(filt, hy_bias):
    L = filt.shape[0]
    nf = 2 * L
    n2f = nf // HY_N1
    out = []
    for o in range(HY_ORDER):
        hf, hb = filt[:, 0, o], filt[:, 1, o]
        g0 = (hf[0] + hb[0] + hy_bias[o].astype(F32))[None]
        g = jnp.concatenate([g0, hf[1:], jnp.zeros((1, hf.shape[1]), F32), hb[1:][::-1]], axis=0)
        gk = jnp.fft.fft(g, axis=0).reshape(HY_N1, n2f, -1).transpose(1, 0, 2)
        out.append(jnp.concatenate([jnp.real(gk), jnp.imag(gk)], axis=1).astype(F32))
    return out


def _short_conv(u, w_ref, b_ref):
    L = u.shape[0]
    row = lax.broadcasted_iota(jnp.int32, u.shape, 0)
    prev = jnp.where(row == 0, 0.0, pltpu.roll(u, 1, axis=0))
    nxt = jnp.where(row == L - 1, 0.0, pltpu.roll(u, L - 1, axis=0))
    return prev * w_ref[0:1, :] + u * w_ref[1:2, :] + nxt * w_ref[2:3, :] + b_ref[...]


def _hyena_kernel(u_ref, gate_ref, cwu_ref, cbu_ref, cwg_ref, cbg_ref,
                  g_ref, ms1_ref, e1_ref, e2_ref, ms2_ref, o_ref, in_ref, buf_ref, *, conv_u):
    n2h, _, C = u_ref.shape[1:]
    n2f = 2 * n2h
    L = n2h * HY_N1
    half = HY_N1

    for s in range(2):
        u = u_ref[s].reshape(L, C).astype(F32)
        if conv_u:
            u = _short_conv(u, cwu_ref, cbu_ref)
        in_ref[s] = u.reshape(n2h, HY_N1, C)

    def s1_body(j, carry):
        r0 = pl.multiple_of(j * HY_SUB, HY_SUB)
        sre = in_ref[0, :, pl.ds(r0, HY_SUB), :].reshape(n2h * HY_SUB, C)
        sim = in_ref[1, :, pl.ds(r0, HY_SUB), :].reshape(n2h * HY_SUB, C)
        slab = jnp.concatenate([sre, sim], axis=0).astype(BF16)
        out = jnp.dot(ms1_ref[...], slab, preferred_element_type=F32)
        buf_ref[:, pl.ds(r0, HY_SUB), :] = out[:n2f * HY_SUB].reshape(n2f, HY_SUB, C)
        buf_ref[:, pl.ds(half + r0, HY_SUB), :] = out[n2f * HY_SUB:].reshape(n2f, HY_SUB, C)
        return carry

    lax.fori_loop(0, HY_N1 // HY_SUB, s1_body, 0)

    def e_body(k2, carry):
        blk = buf_ref[k2].astype(BF16)
        x = jnp.dot(e1_ref[k2], blk, preferred_element_type=F32)
        xr, xi = x[:half], x[half:]
        gr = g_ref[k2, :half, :]
        gi = g_ref[k2, half:, :]
        y = jnp.concatenate([xr * gr - xi * gi, xr * gi + xi * gr], axis=0).astype(BF16)
        buf_ref[k2] = jnp.dot(e2_ref[k2], y, preferred_element_type=F32)
        return carry

    lax.fori_loop(0, n2f, e_body, 0)

    def s2_body(j, carry):
        r0 = pl.multiple_of(j * HY_SUB, HY_SUB)
        sre = buf_ref[:, pl.ds(r0, HY_SUB), :].reshape(n2f * HY_SUB, C)
        sim = buf_ref[:, pl.ds(half + r0, HY_SUB), :].reshape(n2f * HY_SUB, C)
        slab = jnp.concatenate([sre, sim], axis=0).astype(BF16)
        out = jnp.dot(ms2_ref[...], slab, preferred_element_type=F32)
        in_ref[0, :, pl.ds(r0, HY_SUB), :] = out[:n2h * HY_SUB].reshape(n2h, HY_SUB, C)
        in_ref[1, :, pl.ds(r0, HY_SUB), :] = out[n2h * HY_SUB:].reshape(n2h, HY_SUB, C)
        return carry

    lax.fori_loop(0, HY_N1 // HY_SUB, s2_body, 0)

    for s in range(2):
        gate = _short_conv(gate_ref[s].reshape(L, C).astype(F32), cwg_ref, cbg_ref)
        o_ref[s] = (gate * in_ref[s].reshape(L, C)).reshape(n2h, HY_N1, C).astype(o_ref.dtype)


def _hyena_order(u4, u_blk, p4, gate_blk, cw, cb, g, mats, conv_u):
    B, n2h = u4.shape[0], u4.shape[1]
    n2f = 2 * n2h
    ncb = C_WIDTH // HY_CB
    ms1, e1, e2, ms2 = mats
    cin_blk = OFF_CIN // HY_CB
    one = pl.Buffered(1)

    def tok(col0):
        return pl.BlockSpec((2, n2h, HY_N1, HY_CB), lambda c, i: (i, 0, 0, col0 + c))

    kern = functools.partial(_hyena_kernel, conv_u=conv_u)
    return pl.pallas_call(
        kern,
        grid=(ncb, B // 2),
        in_specs=[tok(u_blk), tok(gate_blk),
                  pl.BlockSpec((3, HY_CB), lambda c, i: (0, c)),
                  pl.BlockSpec((1, HY_CB), lambda c, i: (0, c)),
                  pl.BlockSpec((3, HY_CB), lambda c, i: (0, (gate_blk - cin_blk) + c)),
                  pl.BlockSpec((1, HY_CB), lambda c, i: (0, (gate_blk - cin_blk) + c)),
                  pl.BlockSpec((n2f, 2 * HY_N1, HY_CB), lambda c, i: (0, 0, c), pipeline_mode=one),
                  pl.BlockSpec(ms1.shape, lambda c, i: (0, 0), pipeline_mode=one),
                  pl.BlockSpec(e1.shape, lambda c, i: (0, 0, 0), pipeline_mode=one),
                  pl.BlockSpec(e2.shape, lambda c, i: (0, 0, 0), pipeline_mode=one),
                  pl.BlockSpec(ms2.shape, lambda c, i: (0, 0), pipeline_mode=one)],
        out_specs=pl.BlockSpec((2, n2h, HY_N1, HY_CB), lambda c, i: (i, 0, 0, c)),
        out_shape=jax.ShapeDtypeStruct((B, n2h, HY_N1, C_WIDTH), BF16),
        scratch_shapes=[pltpu.VMEM((2, n2h, HY_N1, HY_CB), F32),
                        pltpu.VMEM((n2f, 2 * HY_N1, HY_CB), F32)],
        compiler_params=_cparams(("arbitrary", "arbitrary")),
        name="hyena_order",
    )(u4, p4, cw, cb.reshape(1, -1), cw, cb.reshape(1, -1), g, ms1, e1, e2, ms2)


def hyena_conv(p3, conv_w, conv_b, g0, g1):
    B, L, W = p3.shape
    n2h = L // HY_N1
    p4 = p3.reshape(B, n2h, HY_N1, W)
    mats = hyena_mats(L)
    cin_blk = OFF_CIN // HY_CB
    ncb = C_WIDTH // HY_CB
    cw, cb = conv_w.astype(F32), conv_b.astype(F32)
    z1 = _hyena_order(p4, cin_blk, p4, cin_blk + ncb, cw, cb, g0, mats, True)
    z2 = _hyena_order(z1, 0, p4, cin_blk + 2 * ncb, cw, cb, g1, mats, False)
    return z2.reshape(B, L, C_WIDTH)


def rope_axis(x, pos):
    half = x.shape[-1] // 2
    inv = ROPE_BASE ** (-jnp.arange(half, dtype=jnp.float32) / half)
    ang = pos.astype(jnp.float32)[:, None] * inv
    cos = jnp.cos(ang)[None, :, None, :].astype(x.dtype)
    sin = jnp.sin(ang)[None, :, None, :].astype(x.dtype)
    a, b = x[..., :half], x[..., half:]
    return jnp.concatenate([a * cos - b * sin, a * sin + b * cos], axis=-1)


def rope_2d(x, rows, cols):
    h = x.shape[-1] // 2
    return jnp.concatenate([rope_axis(x[..., :h], rows), rope_axis(x[..., h:], cols)], axis=-1)


def softmax_with_sink(s, sink):
    m = jnp.maximum(jnp.max(s, axis=-1, keepdims=True), sink)
    p = jnp.exp(s - m)
    return p / (jnp.sum(p, axis=-1, keepdims=True) + jnp.exp(sink - m))


def window_attn_latent(q, k, v, kc, vc, sink):
    B, L = q.shape[0], q.shape[1]
    nb = L // A_BLOCK
    span = A_BLOCK + 2 * WINDOW
    scale = HEAD_DIM ** -0.5
    pad = ((0, 0), (WINDOW, WINDOW), (0, 0), (0, 0))
    kp, vp = jnp.pad(k, pad), jnp.pad(v, pad)
    qg = q.reshape(B, L, A_KV_HEADS, A_GROUP, HEAD_DIM)
    sink_g = sink.astype(jnp.float32).reshape(1, A_KV_HEADS, A_GROUP, 1, 1)

    def block(i):
        q0 = i * A_BLOCK
        qb = lax.dynamic_slice_in_dim(qg, q0, A_BLOCK, axis=1)
        kb = lax.dynamic_slice_in_dim(kp, q0, span, axis=1)
        vb = lax.dynamic_slice_in_dim(vp, q0, span, axis=1)
        s_loc = jnp.einsum('bqgrd,bkgd->bgrqk', qb, kb).astype(jnp.float32) * scale
        s_ctx = jnp.einsum('bqgrd,bcgd->bgrqc', qb, kc).astype(jnp.float32) * scale
        qpos = q0 + jnp.arange(A_BLOCK)
        kpos = q0 - WINDOW + jnp.arange(span)
        valid = ((kpos[None, :] >= 0) & (kpos[None, :] < L)
                 & (jnp.abs(qpos[:, None] - kpos[None, :]) <= WINDOW))
        s_loc = jnp.where(valid, s_loc, NEG)
        p = softmax_with_sink(jnp.concatenate([s_loc, s_ctx], axis=-1), sink_g).astype(v.dtype)
        o = (jnp.einsum('bgrqk,bkgd->bqgrd', p[..., :span], vb)
             + jnp.einsum('bgrqc,bcgd->bqgrd', p[..., span:], vc))
        return o.reshape(B, A_BLOCK, A_Q_W)

    out = lax.map(block, jnp.arange(nb))
    return out.transpose(1, 0, 2, 3).reshape(B, L, A_Q_W)


def ctx_attn_a(qc, kc, vc, sink):
    B, Lc = qc.shape[0], qc.shape[1]
    qg = qc.reshape(B, Lc, A_KV_HEADS, A_GROUP, HEAD_DIM)
    s = jnp.einsum('bqgrd,bkgd->bgrqk', qg, kc).astype(jnp.float32) * HEAD_DIM ** -0.5
    sink_g = sink.astype(jnp.float32).reshape(1, A_KV_HEADS, A_GROUP, 1, 1)
    p = softmax_with_sink(s, sink_g).astype(vc.dtype)
    return jnp.einsum('bgrqk,bkgd->bqgrd', p, vc).reshape(B, Lc, A_Q_W)


def neighbourhood_attn_latent(q, k, v, kc, vc, rpb):
    B, L = q.shape[0], q.shape[1]
    rows = L // GRID_W
    wr = min(NA_ROWS, rows)
    scale = HEAD_DIM ** -0.5
    qg = q.reshape(B, rows, GRID_W, B_HEADS, HEAD_DIM)
    kg = k.reshape(B, rows, GRID_W, B_HEADS, HEAD_DIM)
    vg = v.reshape(B, rows, GRID_W, B_HEADS, HEAD_DIM)
    col = jnp.arange(GRID_W)
    col_start = jnp.clip(col - NA_COLS // 2, 0, GRID_W - NA_COLS)
    col_ok = (col[None, :] >= col_start[:, None]) & (col[None, :] < col_start[:, None] + NA_COLS)
    dc_idx = jnp.clip(col[None, :] - col[:, None] + NA_COLS - 1, 0, 2 * NA_COLS - 2)
    rpb_f = rpb.astype(jnp.float32)
    nloc = wr * GRID_W

    def row(r):
        rs = jnp.clip(r - wr // 2, 0, rows - wr)
        qr = lax.dynamic_index_in_dim(qg, r, axis=1, keepdims=False)
        kr = lax.dynamic_slice_in_dim(kg, rs, wr, axis=1)
        vr = lax.dynamic_slice_in_dim(vg, rs, wr, axis=1)
        dr_idx = rs + jnp.arange(wr) - r + NA_ROWS - 1
        bias = rpb_f[:, dr_idx[None, :, None], dc_idx[:, None, :]]
        s_loc = jnp.einsum('bqhd,bjkhd->bhqjk', qr, kr).astype(jnp.float32) * scale + bias
        s_loc = jnp.where(col_ok[:, None, :], s_loc, NEG).reshape(B, B_HEADS, GRID_W, nloc)
        s_ctx = jnp.einsum('bqhd,bchd->bhqc', qr, kc).astype(jnp.float32) * scale
        p = jax.nn.softmax(jnp.concatenate([s_loc, s_ctx], axis=-1), axis=-1).astype(v.dtype)
        p_loc = p[..., :nloc].reshape(B, B_HEADS, GRID_W, wr, GRID_W)
        o = (jnp.einsum('bhqjk,bjkhd->bqhd', p_loc, vr)
             + jnp.einsum('bhqc,bchd->bqhd', p[..., nloc:], vc))
        return o.reshape(B, GRID_W, B_W)

    out = lax.map(row, jnp.arange(rows))
    return out.transpose(1, 0, 2, 3).reshape(B, L, B_W)


def ctx_attn_b(qc, kc, vc):
    B, Lc = qc.shape[0], qc.shape[1]
    s = jnp.einsum('bqhd,bkhd->bhqk', qc, kc).astype(jnp.float32) * HEAD_DIM ** -0.5
    p = jax.nn.softmax(s, axis=-1).astype(vc.dtype)
    return jnp.einsum('bhqk,bkhd->bqhd', p, vc).reshape(B, Lc, B_W)


def short_conv(u, w, b):
    L = u.shape[1]
    up = jnp.pad(u, ((0, 0), (1, 1), (0, 0)))
    return up[:, :L] * w[0] + up[:, 1:L + 1] * w[1] + up[:, 2:] * w[2] + b


def hyena_filters(L, w1, b1, w2, b2, w3, freq):
    t = jnp.linspace(0.0, 1.0, L, dtype=jnp.float32)[:, None]
    omega = 2.0 * math.pi * jnp.arange(L, dtype=jnp.float32)[:, None] / L
    f = jnp.linspace(1e-4, HY_BANDS - 1, HY_BANDS, dtype=jnp.float32)[None, :]
    z = jnp.concatenate([t, jnp.cos(f * omega), -jnp.sin(f * omega)], axis=-1).astype(w1.dtype)
    h = jnp.sin(freq * (z @ w1 + b1))
    h = jnp.sin(freq * (h @ w2 + b2))
    h = (h @ w3).astype(jnp.float32).reshape(L, HY_DIRS, HY_ORDER, C_WIDTH)
    deltas = jnp.abs(jnp.linspace(math.log(HY_TARGET) / HY_SLOW, math.log(HY_TARGET) / HY_FAST,
                                  C_WIDTH, dtype=jnp.float32))
    h = h * jnp.exp(-t[:, :, None, None] * deltas)
    return h / jnp.sqrt(jnp.sum(h * h, axis=0, keepdims=True) + EPS)


def fft_conv(u, h):
    L = u.shape[1]
    n = 2 * L
    U = jnp.fft.rfft(u, n=n, axis=1)
    H = jnp.fft.rfft(h, n=n, axis=0)
    return jnp.fft.irfft(U * H[None], n=n, axis=1)[:, :L]


def bidir_long_conv(u, h, bias):
    return fft_conv(u, h[:, 0]) + fft_conv(u[:, ::-1], h[:, 1])[:, ::-1] + u * bias


def hyena(p, conv_w, conv_b, filt, hy_bias):
    v, x1, x2 = jnp.split(short_conv(p, conv_w, conv_b), 3, axis=-1)
    z = v.astype(jnp.float32)
    for o, gate in enumerate((x1, x2)):
        z = gate.astype(jnp.float32) * bidir_long_conv(z, filt[:, :, o], hy_bias[o].astype(jnp.float32))
    return z.astype(p.dtype)


def _rope_partner(width):
    j = np.arange(width)
    return np.where((j % 32) < 16, j + 16, j - 16)


def prep_w_in(w):
    ak, av, bk, bv, aq, bq, cin, gates = jnp.split(w, SPLITS, axis=-1)
    aqr = aq[:, _rope_partner(A_Q_W)]
    akr = ak[:, _rope_partner(A_KV_W)]
    pad = jnp.zeros((w.shape[0], P_COLS - OFF_AV - A_KV_W), w.dtype)
    return jnp.concatenate([gates, cin, aq, aqr, bq, bk, bv, ak, akr, av, pad], axis=-1).astype(BF16)


def kernel(x, c, ctx, c_ctx, w_mod, b_mod, g_mix_pre, g_mix_post, g_ffn_pre, g_ffn_post,
           w_in, sink_a, rpb_b, conv_c_w, conv_c_b, hy_w1, hy_b1, hy_w2, hy_b2, hy_w3,
           hy_freq, hy_bias, w_branch, w_out, ffn_w1, ffn_w3, ffn_w2,
           moe_router, moe_w1, moe_w3, moe_w2):
    B, L, D = x.shape
    Lc = ctx.shape[1]
    N, Nc = B * L, B * Lc
    x2 = x.reshape(N, D)
    ctx2 = ctx.reshape(Nc, D)
    cs = jnp.concatenate([c, c_ctx[None, :], jnp.zeros((7, D), F32)], axis=0)

    for l in range(DEPTH):
        last = l == DEPTH - 1
        mod = modulation(cs, w_mod[l], b_mod[l])
        mods = [mod[:B, k * D:(k + 1) * D].reshape(B, 1, D) for k in range(6)]
        modc = [mod[B:B + 1, k * D:(k + 1) * D].reshape(1, 1, D) for k in range(6)]
        sh_m, sc_m, gt_m, sh_f, sc_f, gt_f = mods
        shc_m, scc_m, gtc_m, shc_f, scc_f, gtc_f = modc

        w_all = prep_w_in(w_in[l])
        p = proj(x2, g_mix_pre[l], sc_m, sh_m, w_all, L)
        if last:
            pc = proj(ctx2, g_mix_pre[l], scc_m, shc_m, w_all[:, KV_OFF:], Nc)
            coff = KV_OFF
        else:
            pc = proj(ctx2, g_mix_pre[l], scc_m, shc_m, w_all, Nc)
            coff = 0

        p3 = p.reshape(B, L, -1)
        pc3 = pc.reshape(B, Lc, -1)
        qr, kr = rope_qk(p, L)
        ya = window_attention(qr.reshape(B, L, -1), kr.reshape(B, L, -1), p3, pc3, coff, sink_a[l])
        yb = neighbourhood_attention(p3, pc3, coff, rpb_b[l])
        filt = hyena_filters(L, hy_w1[l], hy_b1[l], hy_w2[l], hy_b2[l], hy_w3[l], hy_freq[l])
        g0, g1 = hyena_spectrum(filt, hy_bias[l])
        yc = hyena_conv(p3, conv_c_w[l], conv_c_b[l], g0, g1)
        wb = w_branch[l].astype(BF16)
        wo = w_out[l].astype(BF16)
        x2 = merge(ya.reshape(N, -1).astype(BF16), yb.reshape(N, -1).astype(BF16),
                   yc.reshape(N, -1).astype(BF16), p, wb, wo, x2, g_mix_post[l], gt_m, L)
        if not last:
            yac, ybc = ctx_attention(pc3, sink_a[l])
            filt_c = hyena_filters(Lc, hy_w1[l], hy_b1[l], hy_w2[l], hy_b2[l], hy_w3[l], hy_freq[l])
            gc0, gc1 = hyena_spectrum(filt_c, hy_bias[l])
            ycc = hyena_conv(pc3, conv_c_w[l], conv_c_b[l], gc0, gc1)
            ctx2 = merge(yac.reshape(Nc, -1).astype(BF16), ybc.reshape(Nc, -1).astype(BF16),
                         ycc.reshape(Nc, -1).astype(BF16), pc, wb, wo, ctx2, g_mix_post[l], gtc_m, Nc)

        j = l // 2
        if l % 2 == 0:
            w1, w3, w2 = ffn_w1[j].astype(BF16), ffn_w3[j].astype(BF16), ffn_w2[j].astype(BF16)
            x2 = ffn(x2, g_ffn_pre[l], sc_f, sh_f, w1, w3, w2, g_ffn_post[l], gt_f, L)
            if not last:
                ctx2 = ffn(ctx2, g_ffn_pre[l], scc_f, shc_f, w1, w3, w2, g_ffn_post[l], gtc_f, Nc)
        else:
            w1, w3, w2 = moe_w1[j].astype(BF16), moe_w3[j].astype(BF16), moe_w2[j].astype(BF16)
            x2 = moe(x2, g_ffn_pre[l], sc_f, sh_f, moe_router[j], w1, w3, w2, g_ffn_post[l], gt_f, L)
            if not last:
                ctx2 = moe(ctx2, g_ffn_pre[l], scc_f, shc_f, moe_router[j], w1, w3, w2,
                           g_ffn_post[l], gtc_f, Nc)
    return x2.reshape(B, L, D)
```

```python
import functools
import math

import jax
import jax.numpy as jnp
import numpy as np
from jax import lax
from jax.experimental import pallas as pl
from jax.experimental.pallas import tpu as pltpu

F32 = jnp.float32
BF16 = jnp.bfloat16

D_MODEL = 1024
DEPTH = 2
GRID_W = 64
HEAD_DIM = 64
A_HEADS = 8
A_KV_HEADS = 2
A_GROUP = A_HEADS // A_KV_HEADS
WINDOW = 128
A_BLOCK = 128
B_HEADS = 8
NA_ROWS = 8
NA_COLS = 16
C_WIDTH = 512
HY_ORDER = 2
HY_DIRS = 2
HY_BANDS = 16
HY_TARGET = 1e-2
HY_FAST = 0.3
HY_SLOW = 1.5
N_BRANCH = 3
N_EXPERTS = 8
TOP_K = 2
ROPE_BASE = 10000.0
EPS = 1e-6
NEG = -1e30

A_KV_W = A_KV_HEADS * HEAD_DIM
A_Q_W = A_HEADS * HEAD_DIM
B_W = B_HEADS * HEAD_DIM
KV_COLS = 2 * A_KV_W + 2 * B_W
SPLITS = (A_KV_W, 2 * A_KV_W, 2 * A_KV_W + B_W, KV_COLS,
          KV_COLS + A_Q_W, KV_COLS + A_Q_W + B_W,
          KV_COLS + A_Q_W + B_W + 3 * C_WIDTH)

OFF_GATES = 0
OFF_CIN = 3072
OFF_AQ = 4608
OFF_AQR = 5120
OFF_BQ = 5632
OFF_BK = 6144
OFF_BV = 6656
OFF_AK = 7168
OFF_AKR = 7296
OFF_AV = 7424
P_COLS = 7680
KV_OFF = OFF_BK
KV_W = P_COLS - KV_OFF

VMEM_LIMIT = 48 * 1024 * 1024


def _cparams(sem):
    return pltpu.CompilerParams(dimension_semantics=sem, vmem_limit_bytes=VMEM_LIMIT)


def _rms(y, g):
    return y * lax.rsqrt(jnp.mean(y * y, axis=-1, keepdims=True) + EPS) * g


def _norm_mod(x, g, sc, sh):
    return _rms(x, g) * (1.0 + sc) + sh


def _mod_kernel(c_ref, w_ref, b_ref, o_ref):
    c = c_ref[...]
    s = c * jax.nn.sigmoid(c)
    o_ref[...] = jnp.dot(s, w_ref[...], preferred_element_type=F32,
                         precision=lax.Precision.HIGHEST) + b_ref[...]


def modulation(cs, w, b, tn=512):
    m, d = cs.shape
    n = w.shape[1]
    return pl.pallas_call(
        _mod_kernel,
        grid=(n // tn,),
        in_specs=[pl.BlockSpec((m, d), lambda j: (0, 0)),
                  pl.BlockSpec((d, tn), lambda j: (0, j)),
                  pl.BlockSpec((1, tn), lambda j: (0, j))],
        out_specs=pl.BlockSpec((m, tn), lambda j: (0, j)),
        out_shape=jax.ShapeDtypeStruct((m, n), F32),
        compiler_params=_cparams(("parallel",)),
        name="modulation",
    )(cs, w, b.reshape(1, n))


def _proj_kernel(x_ref, g_ref, sc_ref, sh_ref, w_ref, o_ref, h_ref):
    @pl.when(pl.program_id(1) == 0)
    def _():
        h_ref[...] = _norm_mod(x_ref[...], g_ref[...], sc_ref[0], sh_ref[0]).astype(BF16)

    o_ref[...] = jnp.dot(h_ref[...], w_ref[...], preferred_element_type=F32).astype(o_ref.dtype)


def proj(x2, g, sc, sh, w, rows_per_mod, tm=1024, tn=1536):
    m, d = x2.shape
    n = w.shape[1]
    tm = min(tm, m)
    tpb = rows_per_mod // tm
    return pl.pallas_call(
        _proj_kernel,
        grid=(m // tm, n // tn),
        in_specs=[pl.BlockSpec((tm, d), lambda i, j: (i, 0)),
                  pl.BlockSpec((1, d), lambda i, j: (0, 0)),
                  pl.BlockSpec((1, 1, d), lambda i, j: (i // tpb, 0, 0)),
                  pl.BlockSpec((1, 1, d), lambda i, j: (i // tpb, 0, 0)),
                  pl.BlockSpec((d, tn), lambda i, j: (0, j))],
        out_specs=pl.BlockSpec((tm, tn), lambda i, j: (i, j)),
        out_shape=jax.ShapeDtypeStruct((m, n), BF16),
        scratch_shapes=[pltpu.VMEM((tm, d), BF16)],
        compiler_params=_cparams(("parallel", "arbitrary")),
        name="proj",
    )(x2, g.reshape(1, d), sc, sh, w)


def _merge_kernel(ya_ref, yb_ref, yc_ref, ga_ref, gb_ref, gc_ref, wb_ref, wo_ref,
                  x_ref, gp_ref, gt_ref, o_ref):
    def branch(y_ref, g_ref, k):
        t = jnp.dot(y_ref[...], wb_ref[k], preferred_element_type=F32)
        return jax.nn.sigmoid(g_ref[...].astype(F32)) * t

    m = branch(ya_ref, ga_ref, 0) + branch(yb_ref, gb_ref, 1) + branch(yc_ref, gc_ref, 2)
    y = jnp.dot(m.astype(BF16), wo_ref[...], preferred_element_type=F32)
    o_ref[...] = x_ref[...] + gt_ref[0] * _rms(y, gp_ref[...])


def merge(ya, yb, yc, p, wb, wo, x2, gpost, gt, rows_per_mod, tm=512):
    m, d = x2.shape
    bw = ya.shape[1]
    tm = min(tm, m)
    tpb = rows_per_mod // tm
    gblk = OFF_GATES // d
    return pl.pallas_call(
        _merge_kernel,
        grid=(m // tm,),
        in_specs=[pl.BlockSpec((tm, bw), lambda i: (i, 0)),
                  pl.BlockSpec((tm, bw), lambda i: (i, 0)),
                  pl.BlockSpec((tm, bw), lambda i: (i, 0)),
                  pl.BlockSpec((tm, d), lambda i: (i, gblk)),
                  pl.BlockSpec((tm, d), lambda i: (i, gblk + 1)),
                  pl.BlockSpec((tm, d), lambda i: (i, gblk + 2)),
                  pl.BlockSpec((N_BRANCH, bw, d), lambda i: (0, 0, 0)),
                  pl.BlockSpec((d, d), lambda i: (0, 0)),
                  pl.BlockSpec((tm, d), lambda i: (i, 0)),
                  pl.BlockSpec((1, d), lambda i: (0, 0)),
                  pl.BlockSpec((1, 1, d), lambda i: (i // tpb, 0, 0))],
        out_specs=pl.BlockSpec((tm, d), lambda i: (i, 0)),
        out_shape=jax.ShapeDtypeStruct((m, d), F32),
        compiler_params=_cparams(("parallel",)),
        name="merge",
    )(ya, yb, yc, p, p, p, wb, wo, x2, gpost.reshape(1, d), gt)


def _ffn_kernel(x_ref, g_ref, sc_ref, sh_ref, w1_ref, w3_ref, w2_ref, gp_ref, gt_ref,
                o_ref, h_ref, acc_ref):
    f = pl.program_id(1)

    @pl.when(f == 0)
    def _():
        h_ref[...] = _norm_mod(x_ref[...], g_ref[...], sc_ref[0], sh_ref[0]).astype(BF16)
        acc_ref[...] = jnp.zeros_like(acc_ref)

    h = h_ref[...]
    a = jnp.dot(h, w1_ref[...], preferred_element_type=F32)
    b = jnp.dot(h, w3_ref[...], preferred_element_type=F32)
    act = (a * jax.nn.sigmoid(a) * b).astype(BF16)
    acc_ref[...] += jnp.dot(act, w2_ref[...], preferred_element_type=F32)

    @pl.when(f == pl.num_programs(1) - 1)
    def _():
        o_ref[...] = x_ref[...] + gt_ref[0] * _rms(acc_ref[...], gp_ref[...])


def ffn(x2, g, sc, sh, w1, w3, w2, gpost, gt, rows_per_mod, tm=512, tf=1408):
    m, d = x2.shape
    ff = w1.shape[1]
    tm = min(tm, m)
    tpb = rows_per_mod // tm
    return pl.pallas_call(
        _ffn_kernel,
        grid=(m // tm, ff // tf),
        in_specs=[pl.BlockSpec((tm, d), lambda i, f: (i, 0)),
                  pl.BlockSpec((1, d), lambda i, f: (0, 0)),
                  pl.BlockSpec((1, 1, d), lambda i, f: (i // tpb, 0, 0)),
                  pl.BlockSpec((1, 1, d), lambda i, f: (i // tpb, 0, 0)),
                  pl.BlockSpec((d, tf), lambda i, f: (0, f)),
                  pl.BlockSpec((d, tf), lambda i, f: (0, f)),
                  pl.BlockSpec((tf, d), lambda i, f: (f, 0)),
                  pl.BlockSpec((1, d), lambda i, f: (0, 0)),
                  pl.BlockSpec((1, 1, d), lambda i, f: (i // tpb, 0, 0))],
        out_specs=pl.BlockSpec((tm, d), lambda i, f: (i, 0)),
        out_shape=jax.ShapeDtypeStruct((m, d), F32),
        scratch_shapes=[pltpu.VMEM((tm, d), BF16), pltpu.VMEM((tm, d), F32)],
        compiler_params=_cparams(("parallel", "arbitrary")),
        name="ffn",
    )(x2, g.reshape(1, d), sc, sh, w1, w3, w2, gpost.reshape(1, d), gt)


def _router_kernel(x_ref, g_ref, sc_ref, sh_ref, rh_ref, rl_ref, h_ref, lg_ref):
    h = _norm_mod(x_ref[...], g_ref[...], sc_ref[0], sh_ref[0])
    hb = h.astype(BF16)
    h_ref[...] = hb
    hl = (h - hb.astype(F32)).astype(BF16)
    rh = rh_ref[...]
    lg_ref[...] = (jnp.dot(hb, rh, preferred_element_type=F32)
                   + jnp.dot(hl, rh, preferred_element_type=F32)
                   + jnp.dot(hb, rl_ref[...], preferred_element_type=F32))


def router(x2, g, sc, sh, r, rows_per_mod, tm=1024):
    m, d = x2.shape
    e = r.shape[1]
    tm = min(tm, m)
    tpb = rows_per_mod // tm
    rh = r.astype(BF16)
    rl = (r - rh.astype(F32)).astype(BF16)
    return pl.pallas_call(
        _router_kernel,
        grid=(m // tm,),
        in_specs=[pl.BlockSpec((tm, d), lambda i: (i, 0)),
                  pl.BlockSpec((1, d), lambda i: (0, 0)),
                  pl.BlockSpec((1, 1, d), lambda i: (i // tpb, 0, 0)),
                  pl.BlockSpec((1, 1, d), lambda i: (i // tpb, 0, 0)),
                  pl.BlockSpec((d, e), lambda i: (0, 0)),
                  pl.BlockSpec((d, e), lambda i: (0, 0))],
        out_specs=[pl.BlockSpec((tm, d), lambda i: (i, 0)),
                   pl.BlockSpec((tm, e), lambda i: (i, 0))],
        out_shape=[jax.ShapeDtypeStruct((m, d), BF16), jax.ShapeDtypeStruct((m, e), F32)],
        compiler_params=_cparams(("parallel",)),
        name="router",
    )(x2, g.reshape(1, d), sc, sh, rh, rl)


def _expert_kernel(te_ref, hs_ref, w1_ref, w3_ref, w2_ref, o_ref, acc_ref):
    i = pl.program_id(0)
    f = pl.program_id(1)
    nf = pl.num_programs(1)
    valid = te_ref[i] >= 0

    @pl.when(f == 0)
    def _():
        acc_ref[...] = jnp.zeros_like(acc_ref)

    @pl.when(valid)
    def _():
        h = hs_ref[...]
        a = jnp.dot(h, w1_ref[0], preferred_element_type=F32)
        b = jnp.dot(h, w3_ref[0], preferred_element_type=F32)
        act = (a * jax.nn.sigmoid(a) * b).astype(BF16)
        acc_ref[...] += jnp.dot(act, w2_ref[0], preferred_element_type=F32)

    @pl.when(f == nf - 1)
    def _():
        o_ref[...] = acc_ref[...].astype(o_ref.dtype)


def experts(tile_expert, hs, w1, w3, w2, tm=512, tf=1792):
    mp, d = hs.shape
    ff = w1.shape[2]

    def wmap(i, f, te):
        return (jnp.maximum(te[i], 0), 0, f)

    def w2map(i, f, te):
        return (jnp.maximum(te[i], 0), f, 0)

    return pl.pallas_call(
        _expert_kernel,
        grid_spec=pltpu.PrefetchScalarGridSpec(
            num_scalar_prefetch=1,
            grid=(mp // tm, ff // tf),
            in_specs=[pl.BlockSpec((tm, d), lambda i, f, te: (i, 0)),
                      pl.BlockSpec((1, d, tf), wmap),
                      pl.BlockSpec((1, d, tf), wmap),
                      pl.BlockSpec((1, tf, d), w2map)],
            out_specs=pl.BlockSpec((tm, d), lambda i, f, te: (i, 0)),
            scratch_shapes=[pltpu.VMEM((tm, d), F32)]),
        out_shape=jax.ShapeDtypeStruct((mp, d), BF16),
        compiler_params=_cparams(("parallel", "arbitrary")),
        name="experts",
    )(tile_expert, hs, w1, w3, w2)


def _combine_kernel(y1_ref, y2_ref, cw_ref, x_ref, gp_ref, gt_ref, o_ref):
    cw = cw_ref[...]
    y = cw[:, 0:1] * y1_ref[...].astype(F32) + cw[:, 1:2] * y2_ref[...].astype(F32)
    o_ref[...] = x_ref[...] + gt_ref[0] * _rms(y, gp_ref[...])


def combine(y1, y2, cw, x2, gpost, gt, rows_per_mod, tm=1024):
    m, d = x2.shape
    tm = min(tm, m)
    tpb = rows_per_mod // tm
    return pl.pallas_call(
        _combine_kernel,
        grid=(m // tm,),
        in_specs=[pl.BlockSpec((tm, d), lambda i: (i, 0)),
                  pl.BlockSpec((tm, d), lambda i: (i, 0)),
                  pl.BlockSpec((tm, TOP_K), lambda i: (i, 0)),
                  pl.BlockSpec((tm, d), lambda i: (i, 0)),
                  pl.BlockSpec((1, d), lambda i: (0, 0)),
                  pl.BlockSpec((1, 1, d), lambda i: (i // tpb, 0, 0))],
        out_specs=pl.BlockSpec((tm, d), lambda i: (i, 0)),
        out_shape=jax.ShapeDtypeStruct((m, d), F32),
        compiler_params=_cparams(("parallel",)),
        name="combine",
    )(y1, y2, cw, x2, gpost.reshape(1, d), gt)


def moe(x2, g, sc, sh, r, w1, w3, w2, gpost, gt, rows_per_mod, tm=512):
    m, d = x2.shape
    e = r.shape[1]
    h, logits = router(x2, g, sc, sh, r, rows_per_mod)
    top_v, top_i = lax.top_k(logits, TOP_K)
    top_w = jax.nn.softmax(top_v, axis=-1)
    flat_e = top_i.reshape(-1)
    onehot = (flat_e[:, None] == jnp.arange(e)[None, :]).astype(jnp.int32)
    rank = jnp.cumsum(onehot, axis=0) - onehot
    counts = jnp.sum(onehot, axis=0)
    padded = ((counts + tm - 1) // tm) * tm
    pad_end = jnp.cumsum(padded)
    pad_start = pad_end - padded
    slot = pad_start[flat_e] + jnp.sum(rank * onehot, axis=1)
    mp = m * TOP_K + e * tm
    token = jnp.arange(m * TOP_K, dtype=jnp.int32) // TOP_K
    slot_token = jnp.zeros((mp,), jnp.int32).at[slot].set(token)
    n_tiles = mp // tm
    tile_start = jnp.arange(n_tiles, dtype=jnp.int32) * tm
    tile_e = jnp.sum((tile_start[:, None] >= pad_end[None, :]).astype(jnp.int32), axis=1)
    tile_e = jnp.where(tile_start < pad_end[-1], tile_e, -1).astype(jnp.int32)
    hs = jnp.take(h, slot_token, axis=0)
    ys = experts(tile_e, hs, w1, w3, w2, tm=tm)
    slot2 = slot.reshape(m, TOP_K)
    y1 = jnp.take(ys, slot2[:, 0], axis=0)
    y2 = jnp.take(ys, slot2[:, 1], axis=0)
    return combine(y1, y2, top_w, x2, gpost, gt, rows_per_mod)


def rope_tables(L, width):
    t = np.arange(L)
    j = np.arange(width) % HEAD_DIM
    pos = np.where(j[None, :] < HEAD_DIM // 2, (t // GRID_W)[:, None], (t % GRID_W)[:, None])
    quarter = HEAD_DIM // 4
    inv = ROPE_BASE ** (-(j % quarter).astype(np.float32) / quarter)
    ang = jnp.asarray(pos.astype(np.float32)) * jnp.asarray(inv.astype(np.float32))[None, :]
    sign = np.where((j % (2 * quarter)) < quarter, -1.0, 1.0).astype(np.float32)
    return jnp.cos(ang), jnp.sin(ang) * jnp.asarray(sign)[None, :]


def _rope_kernel(aq_ref, aqr_ref, ak_ref, akr_ref, cq_ref, sq_ref, ck_ref, sk_ref, q_ref, k_ref):
    scale = HEAD_DIM ** -0.5
    q = aq_ref[...].astype(F32) * cq_ref[...] + aqr_ref[...].astype(F32) * sq_ref[...]
    q_ref[...] = (q * scale).astype(BF16)
    k = ak_ref[...].astype(F32) * ck_ref[...] + akr_ref[...].astype(F32) * sk_ref[...]
    k_ref[...] = k.astype(BF16)


def rope_qk(p, L, tm=1024):
    m = p.shape[0]
    tm = min(tm, L)
    nt = L // tm
    cq, sq = rope_tables(L, A_Q_W)
    ck, sk = rope_tables(L, A_KV_W)
    return pl.pallas_call(
        _rope_kernel,
        grid=(m // tm,),
        in_specs=[pl.BlockSpec((tm, A_Q_W), lambda i: (i, OFF_AQ // A_Q_W)),
                  pl.BlockSpec((tm, A_Q_W), lambda i: (i, OFF_AQR // A_Q_W)),
                  pl.BlockSpec((tm, A_KV_W), lambda i: (i, OFF_AK // A_KV_W)),
                  pl.BlockSpec((tm, A_KV_W), lambda i: (i, OFF_AKR // A_KV_W)),
                  pl.BlockSpec((tm, A_Q_W), lambda i: (i % nt, 0)),
                  pl.BlockSpec((tm, A_Q_W), lambda i: (i % nt, 0)),
                  pl.BlockSpec((tm, A_KV_W), lambda i: (i % nt, 0)),
                  pl.BlockSpec((tm, A_KV_W), lambda i: (i % nt, 0))],
        out_specs=[pl.BlockSpec((tm, A_Q_W), lambda i: (i, 0)),
                   pl.BlockSpec((tm, A_KV_W), lambda i: (i, 0))],
        out_shape=[jax.ShapeDtypeStruct((m, A_Q_W), BF16), jax.ShapeDtypeStruct((m, A_KV_W), BF16)],
        compiler_params=_cparams(("parallel",)),
        name="rope_qk",
    )(p, p, p, p, cq, sq, ck, sk)


def _dot_nt(a, b):
    return lax.dot_general(a, b, (((1,), (1,)), ((), ())), preferred_element_type=F32)


A_SPAN = A_BLOCK + 2 * WINDOW


def _wa_kernel(sink_ref, q_ref, k_ref, v_ref, kc_ref, vc_ref, o_ref):
    i = pl.program_id(1)
    L = k_ref.shape[1]
    start = pl.multiple_of(jnp.clip(i * A_BLOCK - WINDOW, 0, L - A_SPAN), A_BLOCK)
    qpos = i * A_BLOCK + lax.broadcasted_iota(jnp.int32, (A_BLOCK, A_SPAN), 0)
    kpos = start + lax.broadcasted_iota(jnp.int32, (A_BLOCK, A_SPAN), 1)
    valid = jnp.abs(qpos - kpos) <= WINDOW
    q = q_ref[0]
    scores = []
    for h in range(A_HEADS):
        ks = slice((h // A_GROUP) * HEAD_DIM, (h // A_GROUP + 1) * HEAD_DIM)
        qh = q[:, h * HEAD_DIM:(h + 1) * HEAD_DIM]
        s_loc = jnp.where(valid, _dot_nt(qh, k_ref[0, pl.ds(start, A_SPAN), ks]), NEG)
        scores.append(jnp.concatenate([s_loc, _dot_nt(qh, kc_ref[0, :, ks])], axis=1))
    probs, rdens = [], []
    for h in range(A_HEADS):
        sink = sink_ref[h]
        m = jnp.maximum(jnp.max(scores[h], axis=-1, keepdims=True), sink)
        p = jnp.exp(scores[h] - m)
        rdens.append(1.0 / (jnp.sum(p, axis=-1, keepdims=True) + jnp.exp(sink - m)))
        probs.append(p.astype(BF16))
    for h in range(A_HEADS):
        ks = slice((h // A_GROUP) * HEAD_DIM, (h // A_GROUP + 1) * HEAD_DIM)
        v_all = jnp.concatenate([v_ref[0, pl.ds(start, A_SPAN), ks], vc_ref[0, :, ks]], axis=0)
        o = jnp.dot(probs[h], v_all, preferred_element_type=F32)
        o_ref[0, :, h * HEAD_DIM:(h + 1) * HEAD_DIM] = (o * rdens[h]).astype(BF16)


def window_attention(qr, kr, p3, pc3, coff, sink):
    B, L, _ = qr.shape
    Lc = pc3.shape[1]
    return pl.pallas_call(
        _wa_kernel,
        grid=(B, L // A_BLOCK),
        in_specs=[pl.BlockSpec(memory_space=pltpu.SMEM),
                  pl.BlockSpec((1, A_BLOCK, A_Q_W), lambda b, i: (b, i, 0)),
                  pl.BlockSpec((1, L, A_KV_W), lambda b, i: (b, 0, 0)),
                  pl.BlockSpec((1, L, A_KV_W), lambda b, i: (b, 0, OFF_AV // A_KV_W)),
                  pl.BlockSpec((1, Lc, A_KV_W), lambda b, i: (b, 0, (OFF_AK - coff) // A_KV_W)),
                  pl.BlockSpec((1, Lc, A_KV_W), lambda b, i: (b, 0, (OFF_AV - coff) // A_KV_W))],
        out_specs=pl.BlockSpec((1, A_BLOCK, A_Q_W), lambda b, i: (b, i, 0)),
        out_shape=jax.ShapeDtypeStruct((B, L, A_Q_W), BF16),
        compiler_params=_cparams(("parallel", "arbitrary")),
        name="window_attention",
    )(sink.astype(F32), qr, kr, p3, pc3, pc3)


NA_QROWS = 4
NA_KROWS = NA_ROWS + NA_QROWS
NA_TQ = NA_QROWS * GRID_W
NA_TK = NA_KROWS * GRID_W
NA_HEAD_GROUP = 2


def na_bias_table(rpb, rows):
    col = np.arange(GRID_W)
    col_start = np.clip(col - NA_COLS // 2, 0, GRID_W - NA_COLS)
    col_ok = (col[None, :] >= col_start[:, None]) & (col[None, :] < col_start[:, None] + NA_COLS)
    dc = np.clip(col[None, :] - col[:, None] + NA_COLS - 1, 0, 2 * NA_COLS - 2)
    a = np.arange(NA_QROWS)
    j = np.arange(NA_KROWS)
    dr_all, ok_all = [], []
    for r0, u in ((0, 0), (NA_QROWS, 0), (rows - NA_QROWS, rows - NA_KROWS)):
        r = r0 + a
        rs = np.clip(r - NA_ROWS // 2, 0, rows - NA_ROWS)
        kr = u + j
        ok = (kr[None, :] >= rs[:, None]) & (kr[None, :] < rs[:, None] + NA_ROWS)
        dr = np.clip(kr[None, :] - r[:, None] + NA_ROWS - 1, 0, 2 * NA_ROWS - 2)
        dr_all.append(dr)
        ok_all.append(ok)
    n_dr, n_dc = 2 * NA_ROWS - 1, 2 * NA_COLS - 1
    H = rpb.shape[0]
    onehot = (dc.reshape(-1)[None, :] == np.arange(n_dc)[:, None]).astype(np.float32)
    tiles = jnp.dot(rpb.astype(F32).reshape(H * n_dr, n_dc), jnp.asarray(onehot),
                    precision=lax.Precision.HIGHEST).reshape(H, n_dr, GRID_W, GRID_W)
    tiles = jnp.where(jnp.asarray(col_ok)[None, None], tiles, NEG)
    tiles = jnp.concatenate([tiles, jnp.full((H, 1, GRID_W, GRID_W), NEG, F32)], axis=1)
    sel = np.where(np.stack(ok_all), np.stack(dr_all), n_dr)
    out = tiles[:, sel.reshape(-1)].reshape(H, 3, NA_QROWS, NA_KROWS, GRID_W, GRID_W)
    return out.transpose(1, 0, 2, 4, 3, 5).reshape(3, H, NA_TQ, NA_TK)


def _na_kernel(q_ref, k_ref, v_ref, kc_ref, vc_ref, bias_ref, o_ref):
    blk = pl.program_id(1)
    rows = k_ref.shape[1] // GRID_W
    u = jnp.clip(blk * NA_QROWS - NA_ROWS // 2, 0, rows - NA_KROWS)
    start = pl.multiple_of(u * GRID_W, GRID_W)
    q = q_ref[0] * (HEAD_DIM ** -0.5)
    for h0 in range(0, B_HEADS, NA_HEAD_GROUP):
        heads = range(h0, h0 + NA_HEAD_GROUP)
        scores = []
        for h in heads:
            hs = slice(h * HEAD_DIM, (h + 1) * HEAD_DIM)
            qh = q[:, hs]
            s_loc = _dot_nt(qh, k_ref[0, pl.ds(start, NA_TK), hs]) + bias_ref[0, h]
            scores.append(jnp.concatenate([s_loc, _dot_nt(qh, kc_ref[0, :, hs])], axis=1))
        probs, rdens = [], []
        for s in scores:
            p = jnp.exp(s - jnp.max(s, axis=-1, keepdims=True))
            rdens.append(1.0 / jnp.sum(p, axis=-1, keepdims=True))
            probs.append(p.astype(BF16))
        for h, p, rden in zip(heads, probs, rdens):
            hs = slice(h * HEAD_DIM, (h + 1) * HEAD_DIM)
            v_all = jnp.concatenate([v_ref[0, pl.ds(start, NA_TK), hs], vc_ref[0, :, hs]], axis=0)
            o = jnp.dot(p, v_all, preferred_element_type=F32)
            o_ref[0, :, hs] = (o * rden).astype(BF16)


def neighbourhood_attention(p3, pc3, coff, rpb):
    B, L, _ = p3.shape
    Lc = pc3.shape[1]
    nblk = L // NA_TQ
    bias = na_bias_table(rpb, L // GRID_W)

    def bias_map(b, i):
        return (jnp.where(i == 0, 0, jnp.where(i == nblk - 1, 2, 1)), 0, 0, 0)

    return pl.pallas_call(
        _na_kernel,
        grid=(B, nblk),
        in_specs=[pl.BlockSpec((1, NA_TQ, B_W), lambda b, i: (b, i, OFF_BQ // B_W)),
                  pl.BlockSpec((1, L, B_W), lambda b, i: (b, 0, OFF_BK // B_W)),
                  pl.BlockSpec((1, L, B_W), lambda b, i: (b, 0, OFF_BV // B_W)),
                  pl.BlockSpec((1, Lc, B_W), lambda b, i: (b, 0, (OFF_BK - coff) // B_W)),
                  pl.BlockSpec((1, Lc, B_W), lambda b, i: (b, 0, (OFF_BV - coff) // B_W)),
                  pl.BlockSpec((1, B_HEADS, NA_TQ, NA_TK), bias_map)],
        out_specs=pl.BlockSpec((1, NA_TQ, B_W), lambda b, i: (b, i, 0)),
        out_shape=jax.ShapeDtypeStruct((B, L, B_W), BF16),
        compiler_params=_cparams(("parallel", "arbitrary")),
        name="neighbourhood_attention",
    )(p3, p3, p3, pc3, pc3, bias)


def _ctx_attn_kernel(sink_ref, aq_ref, ak_ref, av_ref, bq_ref, bk_ref, bv_ref, oa_ref, ob_ref):
    scale = HEAD_DIM ** -0.5
    aq = aq_ref[0] * scale
    bq = bq_ref[0] * scale
    for h in range(A_HEADS):
        hs = slice(h * HEAD_DIM, (h + 1) * HEAD_DIM)
        g = h // A_GROUP
        ks = slice(g * HEAD_DIM, (g + 1) * HEAD_DIM)
        s = _dot_nt(aq[:, hs], ak_ref[0, :, ks])
        sink = sink_ref[h]
        m = jnp.maximum(jnp.max(s, axis=-1, keepdims=True), sink)
        p = jnp.exp(s - m)
        den = jnp.sum(p, axis=-1, keepdims=True) + jnp.exp(sink - m)
        o = jnp.dot(p.astype(BF16), av_ref[0, :, ks], preferred_element_type=F32)
        oa_ref[0, :, hs] = (o / den).astype(BF16)
    for h in range(B_HEADS):
        hs = slice(h * HEAD_DIM, (h + 1) * HEAD_DIM)
        s = _dot_nt(bq[:, hs], bk_ref[0, :, hs])
        m = jnp.max(s, axis=-1, keepdims=True)
        p = jnp.exp(s - m)
        den = jnp.sum(p, axis=-1, keepdims=True)
        o = jnp.dot(p.astype(BF16), bv_ref[0, :, hs], preferred_element_type=F32)
        ob_ref[0, :, hs] = (o / den).astype(BF16)


def ctx_attention(pc3, sink):
    B, Lc, _ = pc3.shape
    return pl.pallas_call(
        _ctx_attn_kernel,
        grid=(B,),
        in_specs=[pl.BlockSpec(memory_space=pltpu.SMEM),
                  pl.BlockSpec((1, Lc, A_Q_W), lambda b: (b, 0, OFF_AQ // A_Q_W)),
                  pl.BlockSpec((1, Lc, A_KV_W), lambda b: (b, 0, OFF_AK // A_KV_W)),
                  pl.BlockSpec((1, Lc, A_KV_W), lambda b: (b, 0, OFF_AV // A_KV_W)),
                  pl.BlockSpec((1, Lc, B_W), lambda b: (b, 0, OFF_BQ // B_W)),
                  pl.BlockSpec((1, Lc, B_W), lambda b: (b, 0, OFF_BK // B_W)),
                  pl.BlockSpec((1, Lc, B_W), lambda b: (b, 0, OFF_BV // B_W))],
        out_specs=[pl.BlockSpec((1, Lc, A_Q_W), lambda b: (b, 0, 0)),
                   pl.BlockSpec((1, Lc, B_W), lambda b: (b, 0, 0))],
        out_shape=[jax.ShapeDtypeStruct((B, Lc, A_Q_W), BF16), jax.ShapeDtypeStruct((B, Lc, B_W), BF16)],
        compiler_params=_cparams(("parallel",)),
        name="ctx_attention",
    )(sink.astype(F32), pc3, pc3, pc3, pc3, pc3, pc3)


HY_N1 = 128
HY_SUB = 8
HY_CB = 256
HY_S_UNROLL = 16
HY_E_UNROLL = 32


def hyena_mats(L):
    nf = 2 * L
    n2f = nf // HY_N1
    n2h = n2f // 2
    eye = np.eye(HY_SUB)

    def real_block(m):
        return np.block([[m.real, -m.imag], [m.imag, m.real]])

    f = np.exp(-2j * np.pi * np.outer(np.arange(n2f), np.arange(n2h)) / n2f)
    ms1 = real_block(np.kron(f, eye))
    fi = np.exp(2j * np.pi * np.outer(np.arange(n2h), np.arange(n2f)) / n2f) / nf
    ms2 = real_block(np.kron(fi, eye))
    k1 = np.arange(HY_N1)
    n1 = np.arange(HY_N1)
    e1, e2 = [], []
    for k2 in range(n2f):
        e = np.exp(-2j * np.pi * np.outer(n2f * k1 + k2, n1) / nf)
        e1.append(real_block(e))
        e2.append(real_block(np.conj(e).T))
    to = lambda m: jnp.asarray(np.asarray(m, np.float32), dtype=BF16)
    return to(ms1), to(np.stack(e1)), to(np.stack(e2)), to(ms2)


def hyena_spectrum(filt, hy_bias):
    L = filt.shape[0]
    nf = 2 * L
    n2f = nf // HY_N1
    out = []
    for o in range(HY_ORDER):
        hf, hb = filt[:, 0, o], filt[:, 1, o]
        g0 = (hf[0] + hb[0] + hy_bias[o].astype(F32))[None]
        g = jnp.concatenate([g0, hf[1:], jnp.zeros((1, hf.shape[1]), F32), hb[1:][::-1]], axis=0)
        gk = jnp.fft.fft(g, axis=0).reshape(HY_N1, n2f, -1).transpose(1, 0, 2)
        out.append(jnp.concatenate([jnp.real(gk), jnp.imag(gk)], axis=1).astype(F32))
    return out


def _short_conv(u, w_ref, b_ref):
    L = u.shape[0]
    row = lax.broadcasted_iota(jnp.int32, (HY_SUB, u.shape[1]), 0)
    prev = pltpu.roll(u, 1, axis=0)
    prev = jnp.concatenate([jnp.where(row == 0, 0.0, prev[:HY_SUB]), prev[HY_SUB:]], axis=0)
    nxt = pltpu.roll(u, L - 1, axis=0)
    nxt = jnp.concatenate([nxt[:L - HY_SUB], jnp.where(row == HY_SUB - 1, 0.0, nxt[L - HY_SUB:])], axis=0)
    return prev * w_ref[0:1, :] + u * w_ref[1:2, :] + nxt * w_ref[2:3, :] + b_ref[...]


def _hyena_kernel(u_ref, gate_ref, cwu_ref, cbu_ref, cwg_ref, cbg_ref,
                  g_ref, ms1_ref, e1_ref, e2_ref, ms2_ref, o_ref, in_ref, buf_ref, *, conv_u):
    n2h, _, C = u_ref.shape[1:]
    n2f = 2 * n2h
    L = n2h * HY_N1
    half = HY_N1

    for s in range(2):
        u = u_ref[s].reshape(L, C).astype(F32)
        if conv_u:
            u = _short_conv(u, cwu_ref, cbu_ref)
        in_ref[s] = u.reshape(n2h, HY_N1, C)

    def s1_body(j, carry):
        r0 = pl.multiple_of(j * HY_SUB, HY_SUB)
        sre = in_ref[0, :, pl.ds(r0, HY_SUB), :].reshape(n2h * HY_SUB, C)
        sim = in_ref[1, :, pl.ds(r0, HY_SUB), :].reshape(n2h * HY_SUB, C)
        slab = jnp.concatenate([sre, sim], axis=0).astype(BF16)
        out = jnp.dot(ms1_ref[...], slab, preferred_element_type=F32)
        buf_ref[:, pl.ds(r0, HY_SUB), :] = out[:n2f * HY_SUB].reshape(n2f, HY_SUB, C)
        buf_ref[:, pl.ds(half + r0, HY_SUB), :] = out[n2f * HY_SUB:].reshape(n2f, HY_SUB, C)
        return carry

    lax.fori_loop(0, HY_N1 // HY_SUB, s1_body, 0, unroll=HY_S_UNROLL)

    def e_body(k2, carry):
        blk = buf_ref[k2].astype(BF16)
        x = jnp.dot(e1_ref[k2], blk, preferred_element_type=F32)
        xr, xi = x[:half], x[half:]
        gr = g_ref[k2, :half, :]
        gi = g_ref[k2, half:, :]
        y = jnp.concatenate([xr * gr - xi * gi, xr * gi + xi * gr], axis=0).astype(BF16)
        buf_ref[k2] = jnp.dot(e2_ref[k2], y, preferred_element_type=F32)
        return carry

    lax.fori_loop(0, n2f, e_body, 0, unroll=min(n2f, HY_E_UNROLL))

    def s2_body(j, carry):
        r0 = pl.multiple_of(j * HY_SUB, HY_SUB)
        sre = buf_ref[:, pl.ds(r0, HY_SUB), :].reshape(n2f * HY_SUB, C)
        sim = buf_ref[:, pl.ds(half + r0, HY_SUB), :].reshape(n2f * HY_SUB, C)
        slab = jnp.concatenate([sre, sim], axis=0).astype(BF16)
        out = jnp.dot(ms2_ref[...], slab, preferred_element_type=F32)
        in_ref[0, :, pl.ds(r0, HY_SUB), :] = out[:n2h * HY_SUB].reshape(n2h, HY_SUB, C)
        in_ref[1, :, pl.ds(r0, HY_SUB), :] = out[n2h * HY_SUB:].reshape(n2h, HY_SUB, C)
        return carry

    lax.fori_loop(0, HY_N1 // HY_SUB, s2_body, 0, unroll=HY_S_UNROLL)

    for s in range(2):
        gate = _short_conv(gate_ref[s].reshape(L, C).astype(F32), cwg_ref, cbg_ref)
        o_ref[s] = (gate * in_ref[s].reshape(L, C)).reshape(n2h, HY_N1, C).astype(o_ref.dtype)


def _hyena_order(u4, u_blk, p4, gate_blk, cw, cb, g, mats, conv_u):
    B, n2h = u4.shape[0], u4.shape[1]
    n2f = 2 * n2h
    ncb = C_WIDTH // HY_CB
    ms1, e1, e2, ms2 = mats
    cin_blk = OFF_CIN // HY_CB
    one = pl.Buffered(1)

    def tok(col0):
        return pl.BlockSpec((2, n2h, HY_N1, HY_CB), lambda c, i: (i, 0, 0, col0 + c))

    kern = functools.partial(_hyena_kernel, conv_u=conv_u)
    return pl.pallas_call(
        kern,
        grid=(ncb, B // 2),
        in_specs=[tok(u_blk), tok(gate_blk),
                  pl.BlockSpec((3, HY_CB), lambda c, i: (0, c)),
                  pl.BlockSpec((1, HY_CB), lambda c, i: (0, c)),
                  pl.BlockSpec((3, HY_CB), lambda c, i: (0, (gate_blk - cin_blk) + c)),
                  pl.BlockSpec((1, HY_CB), lambda c, i: (0, (gate_blk - cin_blk) + c)),
                  pl.BlockSpec((n2f, 2 * HY_N1, HY_CB), lambda c, i: (0, 0, c), pipeline_mode=one),
                  pl.BlockSpec(ms1.shape, lambda c, i: (0, 0), pipeline_mode=one),
                  pl.BlockSpec(e1.shape, lambda c, i: (0, 0, 0), pipeline_mode=one),
                  pl.BlockSpec(e2.shape, lambda c, i: (0, 0, 0), pipeline_mode=one),
                  pl.BlockSpec(ms2.shape, lambda c, i: (0, 0), pipeline_mode=one)],
        out_specs=pl.BlockSpec((2, n2h, HY_N1, HY_CB), lambda c, i: (i, 0, 0, c)),
        out_shape=jax.ShapeDtypeStruct((B, n2h, HY_N1, C_WIDTH), BF16),
        scratch_shapes=[pltpu.VMEM((2, n2h, HY_N1, HY_CB), F32),
                        pltpu.VMEM((n2f, 2 * HY_N1, HY_CB), F32)],
        compiler_params=_cparams(("arbitrary", "arbitrary")),
        name="hyena_order",
    )(u4, p4, cw, cb.reshape(1, -1), cw, cb.reshape(1, -1), g, ms1, e1, e2, ms2)


def hyena_conv(p3, conv_w, conv_b, g0, g1):
    B, L, W = p3.shape
    n2h = L // HY_N1
    p4 = p3.reshape(B, n2h, HY_N1, W)
    mats = hyena_mats(L)
    cin_blk = OFF_CIN // HY_CB
    ncb = C_WIDTH // HY_CB
    cw, cb = conv_w.astype(F32), conv_b.astype(F32)
    z1 = _hyena_order(p4, cin_blk, p4, cin_blk + ncb, cw, cb, g0, mats, True)
    z2 = _hyena_order(z1, 0, p4, cin_blk + 2 * ncb, cw, cb, g1, mats, False)
    return z2.reshape(B, L, C_WIDTH)


def rope_axis(x, pos):
    half = x.shape[-1] // 2
    inv = ROPE_BASE ** (-jnp.arange(half, dtype=jnp.float32) / half)
    ang = pos.astype(jnp.float32)[:, None] * inv
    cos = jnp.cos(ang)[None, :, None, :].astype(x.dtype)
    sin = jnp.sin(ang)[None, :, None, :].astype(x.dtype)
    a, b = x[..., :half], x[..., half:]
    return jnp.concatenate([a * cos - b * sin, a * sin + b * cos], axis=-1)


def rope_2d(x, rows, cols):
    h = x.shape[-1] // 2
    return jnp.concatenate([rope_axis(x[..., :h], rows), rope_axis(x[..., h:], cols)], axis=-1)


def softmax_with_sink(s, sink):
    m = jnp.maximum(jnp.max(s, axis=-1, keepdims=True), sink)
    p = jnp.exp(s - m)
    return p / (jnp.sum(p, axis=-1, keepdims=True) + jnp.exp(sink - m))


def window_attn_latent(q, k, v, kc, vc, sink):
    B, L = q.shape[0], q.shape[1]
    nb = L // A_BLOCK
    span = A_BLOCK + 2 * WINDOW
    scale = HEAD_DIM ** -0.5
    pad = ((0, 0), (WINDOW, WINDOW), (0, 0), (0, 0))
    kp, vp = jnp.pad(k, pad), jnp.pad(v, pad)
    qg = q.reshape(B, L, A_KV_HEADS, A_GROUP, HEAD_DIM)
    sink_g = sink.astype(jnp.float32).reshape(1, A_KV_HEADS, A_GROUP, 1, 1)

    def block(i):
        q0 = i * A_BLOCK
        qb = lax.dynamic_slice_in_dim(qg, q0, A_BLOCK, axis=1)
        kb = lax.dynamic_slice_in_dim(kp, q0, span, axis=1)
        vb = lax.dynamic_slice_in_dim(vp, q0, span, axis=1)
        s_loc = jnp.einsum('bqgrd,bkgd->bgrqk', qb, kb).astype(jnp.float32) * scale
        s_ctx = jnp.einsum('bqgrd,bcgd->bgrqc', qb, kc).astype(jnp.float32) * scale
        qpos = q0 + jnp.arange(A_BLOCK)
        kpos = q0 - WINDOW + jnp.arange(span)
        valid = ((kpos[None, :] >= 0) & (kpos[None, :] < L)
                 & (jnp.abs(qpos[:, None] - kpos[None, :]) <= WINDOW))
        s_loc = jnp.where(valid, s_loc, NEG)
        p = softmax_with_sink(jnp.concatenate([s_loc, s_ctx], axis=-1), sink_g).astype(v.dtype)
        o = (jnp.einsum('bgrqk,bkgd->bqgrd', p[..., :span], vb)
             + jnp.einsum('bgrqc,bcgd->bqgrd', p[..., span:], vc))
        return o.reshape(B, A_BLOCK, A_Q_W)

    out = lax.map(block, jnp.arange(nb))
    return out.transpose(1, 0, 2, 3).reshape(B, L, A_Q_W)


def ctx_attn_a(qc, kc, vc, sink):
    B, Lc = qc.shape[0], qc.shape[1]
    qg = qc.reshape(B, Lc, A_KV_HEADS, A_GROUP, HEAD_DIM)
    s = jnp.einsum('bqgrd,bkgd->bgrqk', qg, kc).astype(jnp.float32) * HEAD_DIM ** -0.5
    sink_g = sink.astype(jnp.float32).reshape(1, A_KV_HEADS, A_GROUP, 1, 1)
    p = softmax_with_sink(s, sink_g).astype(vc.dtype)
    return jnp.einsum('bgrqk,bkgd->bqgrd', p, vc).reshape(B, Lc, A_Q_W)


def neighbourhood_attn_latent(q, k, v, kc, vc, rpb):
    B, L = q.shape[0], q.shape[1]
    rows = L // GRID_W
    wr = min(NA_ROWS, rows)
    scale = HEAD_DIM ** -0.5
    qg = q.reshape(B, rows, GRID_W, B_HEADS, HEAD_DIM)
    kg = k.reshape(B, rows, GRID_W, B_HEADS, HEAD_DIM)
    vg = v.reshape(B, rows, GRID_W, B_HEADS, HEAD_DIM)
    col = jnp.arange(GRID_W)
    col_start = jnp.clip(col - NA_COLS // 2, 0, GRID_W - NA_COLS)
    col_ok = (col[None, :] >= col_start[:, None]) & (col[None, :] < col_start[:, None] + NA_COLS)
    dc_idx = jnp.clip(col[None, :] - col[:, None] + NA_COLS - 1, 0, 2 * NA_COLS - 2)
    rpb_f = rpb.astype(jnp.float32)
    nloc = wr * GRID_W

    def row(r):
        rs = jnp.clip(r - wr // 2, 0, rows - wr)
        qr = lax.dynamic_index_in_dim(qg, r, axis=1, keepdims=False)
        kr = lax.dynamic_slice_in_dim(kg, rs, wr, axis=1)
        vr = lax.dynamic_slice_in_dim(vg, rs, wr, axis=1)
        dr_idx = rs + jnp.arange(wr) - r + NA_ROWS - 1
        bias = rpb_f[:, dr_idx[None, :, None], dc_idx[:, None, :]]
        s_loc = jnp.einsum('bqhd,bjkhd->bhqjk', qr, kr).astype(jnp.float32) * scale + bias
        s_loc = jnp.where(col_ok[:, None, :], s_loc, NEG).reshape(B, B_HEADS, GRID_W, nloc)
        s_ctx = jnp.einsum('bqhd,bchd->bhqc', qr, kc).astype(jnp.float32) * scale
        p = jax.nn.softmax(jnp.concatenate([s_loc, s_ctx], axis=-1), axis=-1).astype(v.dtype)
        p_loc = p[..., :nloc].reshape(B, B_HEADS, GRID_W, wr, GRID_W)
        o = (jnp.einsum('bhqjk,bjkhd->bqhd', p_loc, vr)
             + jnp.einsum('bhqc,bchd->bqhd', p[..., nloc:], vc))
        return o.reshape(B, GRID_W, B_W)

    out = lax.map(row, jnp.arange(rows))
    return out.transpose(1, 0, 2, 3).reshape(B, L, B_W)


def ctx_attn_b(qc, kc, vc):
    B, Lc = qc.shape[0], qc.shape[1]
    s = jnp.einsum('bqhd,bkhd->bhqk', qc, kc).astype(jnp.float32) * HEAD_DIM ** -0.5
    p = jax.nn.softmax(s, axis=-1).astype(vc.dtype)
    return jnp.einsum('bhqk,bkhd->bqhd', p, vc).reshape(B, Lc, B_W)


def short_conv(u, w, b):
    L = u.shape[1]
    up = jnp.pad(u, ((0, 0), (1, 1), (0, 0)))
    return up[:, :L] * w[0] + up[:, 1:L + 1] * w[1] + up[:, 2:] * w[2] + b


def hyena_filters(L, w1, b1, w2, b2, w3, freq):
    t = jnp.linspace(0.0, 1.0, L, dtype=jnp.float32)[:, None]
    omega = 2.0 * math.pi * jnp.arange(L, dtype=jnp.float32)[:, None] / L
    f = jnp.linspace(1e-4, HY_BANDS - 1, HY_BANDS, dtype=jnp.float32)[None, :]
    z = jnp.concatenate([t, jnp.cos(f * omega), -jnp.sin(f * omega)], axis=-1).astype(w1.dtype)
    h = jnp.sin(freq * (z @ w1 + b1))
    h = jnp.sin(freq * (h @ w2 + b2))
    h = (h @ w3).astype(jnp.float32).reshape(L, HY_DIRS, HY_ORDER, C_WIDTH)
    deltas = jnp.abs(jnp.linspace(math.log(HY_TARGET) / HY_SLOW, math.log(HY_TARGET) / HY_FAST,
                                  C_WIDTH, dtype=jnp.float32))
    h = h * jnp.exp(-t[:, :, None, None] * deltas)
    return h / jnp.sqrt(jnp.sum(h * h, axis=0, keepdims=True) + EPS)


def fft_conv(u, h):
    L = u.shape[1]
    n = 2 * L
    U = jnp.fft.rfft(u, n=n, axis=1)
    H = jnp.fft.rfft(h, n=n, axis=0)
    return jnp.fft.irfft(U * H[None], n=n, axis=1)[:, :L]


def bidir_long_conv(u, h, bias):
    return fft_conv(u, h[:, 0]) + fft_conv(u[:, ::-1], h[:, 1])[:, ::-1] + u * bias


def hyena(p, conv_w, conv_b, filt, hy_bias):
    v, x1, x2 = jnp.split(short_conv(p, conv_w, conv_b), 3, axis=-1)
    z = v.astype(jnp.float32)
    for o, gate in enumerate((x1, x2)):
        z = gate.astype(jnp.float32) * bidir_long_conv(z, filt[:, :, o], hy_bias[o].astype(jnp.float32))
    return z.astype(p.dtype)


def _rope_partner(width):
    j = np.arange(width)
    return np.where((j % 32) < 16, j + 16, j - 16)


def prep_w_in(w):
    ak, av, bk, bv, aq, bq, cin, gates = jnp.split(w, SPLITS, axis=-1)
    aqr = aq[:, _rope_partner(A_Q_W)]
    akr = ak[:, _rope_partner(A_KV_W)]
    pad = jnp.zeros((w.shape[0], P_COLS - OFF_AV - A_KV_W), w.dtype)
    return jnp.concatenate([gates, cin, aq, aqr, bq, bk, bv, ak, akr, av, pad], axis=-1).astype(BF16)


def kernel(x, c, ctx, c_ctx, w_mod, b_mod, g_mix_pre, g_mix_post, g_ffn_pre, g_ffn_post,
           w_in, sink_a, rpb_b, conv_c_w, conv_c_b, hy_w1, hy_b1, hy_w2, hy_b2, hy_w3,
           hy_freq, hy_bias, w_branch, w_out, ffn_w1, ffn_w3, ffn_w2,
           moe_router, moe_w1, moe_w3, moe_w2):
    B, L, D = x.shape
    Lc = ctx.shape[1]
    N, Nc = B * L, B * Lc
    x2 = x.reshape(N, D)
    ctx2 = ctx.reshape(Nc, D)
    cs = jnp.concatenate([c, c_ctx[None, :], jnp.zeros((7, D), F32)], axis=0)

    for l in range(DEPTH):
        last = l == DEPTH - 1
        mod = modulation(cs, w_mod[l], b_mod[l])
        mods = [mod[:B, k * D:(k + 1) * D].reshape(B, 1, D) for k in range(6)]
        modc = [mod[B:B + 1, k * D:(k + 1) * D].reshape(1, 1, D) for k in range(6)]
        sh_m, sc_m, gt_m, sh_f, sc_f, gt_f = mods
        shc_m, scc_m, gtc_m, shc_f, scc_f, gtc_f = modc

        w_all = prep_w_in(w_in[l])
        p = proj(x2, g_mix_pre[l], sc_m, sh_m, w_all, L)
        if last:
            pc = proj(ctx2, g_mix_pre[l], scc_m, shc_m, w_all[:, KV_OFF:], Nc)
            coff = KV_OFF
        else:
            pc = proj(ctx2, g_mix_pre[l], scc_m, shc_m, w_all, Nc)
            coff = 0

        p3 = p.reshape(B, L, -1)
        pc3 = pc.reshape(B, Lc, -1)
        qr, kr = rope_qk(p, L)
        ya = window_attention(qr.reshape(B, L, -1), kr.reshape(B, L, -1), p3, pc3, coff, sink_a[l])
        yb = neighbourhood_attention(p3, pc3, coff, rpb_b[l])
        filt = hyena_filters(L, hy_w1[l], hy_b1[l], hy_w2[l], hy_b2[l], hy_w3[l], hy_freq[l])
        g0, g1 = hyena_spectrum(filt, hy_bias[l])
        yc = hyena_conv(p3, conv_c_w[l], conv_c_b[l], g0, g1)
        wb = w_branch[l].astype(BF16)
        wo = w_out[l].astype(BF16)
        x2 = merge(ya.reshape(N, -1).astype(BF16), yb.reshape(N, -1).astype(BF16),
                   yc.reshape(N, -1).astype(BF16), p, wb, wo, x2, g_mix_post[l], gt_m, L)
        if not last:
            yac, ybc = ctx_attention(pc3, sink_a[l])
            filt_c = hyena_filters(Lc, hy_w1[l], hy_b1[l], hy_w2[l], hy_b2[l], hy_w3[l], hy_freq[l])
            gc0, gc1 = hyena_spectrum(filt_c, hy_bias[l])
            ycc = hyena_conv(pc3, conv_c_w[l], conv_c_b[l], gc0, gc1)
            ctx2 = merge(yac.reshape(Nc, -1).astype(BF16), ybc.reshape(Nc, -1).astype(BF16),
                         ycc.reshape(Nc, -1).astype(BF16), pc, wb, wo, ctx2, g_mix_post[l], gtc_m, Nc)

        j = l // 2
        if l % 2 == 0:
            w1, w3, w2 = ffn_w1[j].astype(BF16), ffn_w3[j].astype(BF16), ffn_w2[j].astype(BF16)
            x2 = ffn(x2, g_ffn_pre[l], sc_f, sh_f, w1, w3, w2, g_ffn_post[l], gt_f, L)
            if not last:
                ctx2 = ffn(ctx2, g_ffn_pre[l], scc_f, shc_f, w1, w3, w2, g_ffn_post[l], gtc_f, Nc)
        else:
            w1, w3, w2 = moe_w1[j].astype(BF16), moe_w3[j].astype(BF16), moe_w2[j].astype(BF16)
            x2 = moe(x2, g_ffn_pre[l], sc_f, sh_f, moe_router[j], w1, w3, w2, g_ffn_post[l], gt_f, L)
            if not last:
                ctx2 = moe(ctx2, g_ffn_pre[l], scc_f, shc_f, moe_router[j], w1, w3, w2,
                           g_ffn_post[l], gtc_f, Nc)
    return x2.reshape(B, L, D)
```

```python
import functools
import math

import jax
import jax.numpy as jnp
import numpy as np
from jax import lax
from jax.experimental import pallas as pl
from jax.experimental.pallas import tpu as pltpu

F32 = jnp.float32
BF16 = jnp.bfloat16

D_MODEL = 1024
DEPTH = 2
GRID_W = 64
HEAD_DIM = 64
A_HEADS = 8
A_KV_HEADS = 2
A_GROUP = A_HEADS // A_KV_HEADS
WINDOW = 128
A_BLOCK = 128
B_HEADS = 8
NA_ROWS = 8
NA_COLS = 16
C_WIDTH = 512
HY_ORDER = 2
HY_DIRS = 2
HY_BANDS = 16
HY_TARGET = 1e-2
HY_FAST = 0.3
HY_SLOW = 1.5
N_BRANCH = 3
N_EXPERTS = 8
TOP_K = 2
ROPE_BASE = 10000.0
EPS = 1e-6
NEG = -1e30

A_KV_W = A_KV_HEADS * HEAD_DIM
A_Q_W = A_HEADS * HEAD_DIM
B_W = B_HEADS * HEAD_DIM
KV_COLS = 2 * A_KV_W + 2 * B_W
SPLITS = (A_KV_W, 2 * A_KV_W, 2 * A_KV_W + B_W, KV_COLS,
          KV_COLS + A_Q_W, KV_COLS + A_Q_W + B_W,
          KV_COLS + A_Q_W + B_W + 3 * C_WIDTH)

OFF_GATES = 0
OFF_CIN = 3072
OFF_AQ = 4608
OFF_AQR = 5120
OFF_BQ = 5632
OFF_BK = 6144
OFF_BV = 6656
OFF_AK = 7168
OFF_AKR = 7296
OFF_AV = 7424
P_COLS = 7680
KV_OFF = OFF_BK
KV_W = P_COLS - KV_OFF

VMEM_LIMIT = 48 * 1024 * 1024
LOG2E = math.log2(math.e)
Q_SCALE = HEAD_DIM ** -0.5 * LOG2E


def _cparams(sem):
    return pltpu.CompilerParams(dimension_semantics=sem, vmem_limit_bytes=VMEM_LIMIT)


def _rms(y, g):
    return y * lax.rsqrt(jnp.mean(y * y, axis=-1, keepdims=True) + EPS) * g


def _norm_mod(x, g, sc, sh):
    return _rms(x, g) * (1.0 + sc) + sh


def _mod_kernel(c_ref, w_ref, b_ref, o_ref):
    c = c_ref[...]
    s = c * jax.nn.sigmoid(c)
    o_ref[...] = jnp.dot(s, w_ref[...], preferred_element_type=F32,
                         precision=lax.Precision.HIGHEST) + b_ref[...]


def modulation(cs, w, b, tn=512):
    m, d = cs.shape
    n = w.shape[1]
    return pl.pallas_call(
        _mod_kernel,
        grid=(n // tn,),
        in_specs=[pl.BlockSpec((m, d), lambda j: (0, 0)),
                  pl.BlockSpec((d, tn), lambda j: (0, j)),
                  pl.BlockSpec((1, tn), lambda j: (0, j))],
        out_specs=pl.BlockSpec((m, tn), lambda j: (0, j)),
        out_shape=jax.ShapeDtypeStruct((m, n), F32),
        compiler_params=_cparams(("parallel",)),
        name="modulation",
    )(cs, w, b.reshape(1, n))


def _proj_kernel(x_ref, g_ref, sc_ref, sh_ref, w_ref, o_ref, h_ref):
    @pl.when(pl.program_id(1) == 0)
    def _():
        h_ref[...] = _norm_mod(x_ref[...], g_ref[...], sc_ref[0], sh_ref[0]).astype(BF16)

    o_ref[...] = jnp.dot(h_ref[...], w_ref[...], preferred_element_type=F32).astype(o_ref.dtype)


def proj(x2, g, sc, sh, w, rows_per_mod, tm=1024, tn=3840):
    m, d = x2.shape
    n = w.shape[1]
    tm = min(tm, m)
    tn = min(tn, n)
    tpb = rows_per_mod // tm
    return pl.pallas_call(
        _proj_kernel,
        grid=(m // tm, n // tn),
        in_specs=[pl.BlockSpec((tm, d), lambda i, j: (i, 0)),
                  pl.BlockSpec((1, d), lambda i, j: (0, 0)),
                  pl.BlockSpec((1, 1, d), lambda i, j: (i // tpb, 0, 0)),
                  pl.BlockSpec((1, 1, d), lambda i, j: (i // tpb, 0, 0)),
                  pl.BlockSpec((d, tn), lambda i, j: (0, j))],
        out_specs=pl.BlockSpec((tm, tn), lambda i, j: (i, j)),
        out_shape=jax.ShapeDtypeStruct((m, n), BF16),
        scratch_shapes=[pltpu.VMEM((tm, d), BF16)],
        compiler_params=_cparams(("parallel", "arbitrary")),
        name="proj",
    )(x2, g.reshape(1, d), sc, sh, w)


def _merge_kernel(ya_ref, yb_ref, yc_ref, ga_ref, gb_ref, gc_ref, wb_ref, wo_ref,
                  x_ref, gp_ref, gt_ref, o_ref):
    def branch(y_ref, g_ref, k):
        t = jnp.dot(y_ref[...], wb_ref[k], preferred_element_type=F32)
        return jax.nn.sigmoid(g_ref[...].astype(F32)) * t

    m = branch(ya_ref, ga_ref, 0) + branch(yb_ref, gb_ref, 1) + branch(yc_ref, gc_ref, 2)
    y = jnp.dot(m.astype(BF16), wo_ref[...], preferred_element_type=F32)
    o_ref[...] = x_ref[...] + gt_ref[0] * _rms(y, gp_ref[...])


def merge(ya, yb, yc, p, wb, wo, x2, gpost, gt, rows_per_mod, tm=512):
    m, d = x2.shape
    bw = ya.shape[1]
    tm = min(tm, m)
    tpb = rows_per_mod // tm
    gblk = OFF_GATES // d
    return pl.pallas_call(
        _merge_kernel,
        grid=(m // tm,),
        in_specs=[pl.BlockSpec((tm, bw), lambda i: (i, 0)),
                  pl.BlockSpec((tm, bw), lambda i: (i, 0)),
                  pl.BlockSpec((tm, bw), lambda i: (i, 0)),
                  pl.BlockSpec((tm, d), lambda i: (i, gblk)),
                  pl.BlockSpec((tm, d), lambda i: (i, gblk + 1)),
                  pl.BlockSpec((tm, d), lambda i: (i, gblk + 2)),
                  pl.BlockSpec((N_BRANCH, bw, d), lambda i: (0, 0, 0)),
                  pl.BlockSpec((d, d), lambda i: (0, 0)),
                  pl.BlockSpec((tm, d), lambda i: (i, 0)),
                  pl.BlockSpec((1, d), lambda i: (0, 0)),
                  pl.BlockSpec((1, 1, d), lambda i: (i // tpb, 0, 0))],
        out_specs=pl.BlockSpec((tm, d), lambda i: (i, 0)),
        out_shape=jax.ShapeDtypeStruct((m, d), F32),
        compiler_params=_cparams(("parallel",)),
        name="merge",
    )(ya, yb, yc, p, p, p, wb, wo, x2, gpost.reshape(1, d), gt)


def _ffn_kernel(x_ref, g_ref, sc_ref, sh_ref, w1_ref, w3_ref, w2_ref, gp_ref, gt_ref,
                o_ref, h_ref, acc_ref):
    f = pl.program_id(1)

    @pl.when(f == 0)
    def _():
        h_ref[...] = _norm_mod(x_ref[...], g_ref[...], sc_ref[0], sh_ref[0]).astype(BF16)
        acc_ref[...] = jnp.zeros_like(acc_ref)

    h = h_ref[...]
    a = jnp.dot(h, w1_ref[...], preferred_element_type=F32)
    b = jnp.dot(h, w3_ref[...], preferred_element_type=F32)
    act = (a * jax.nn.sigmoid(a) * b).astype(BF16)
    acc_ref[...] += jnp.dot(act, w2_ref[...], preferred_element_type=F32)

    @pl.when(f == pl.num_programs(1) - 1)
    def _():
        o_ref[...] = x_ref[...] + gt_ref[0] * _rms(acc_ref[...], gp_ref[...])


def ffn(x2, g, sc, sh, w1, w3, w2, gpost, gt, rows_per_mod, tm=512, tf=2816):
    m, d = x2.shape
    ff = w1.shape[1]
    tm = min(tm, m)
    tpb = rows_per_mod // tm
    return pl.pallas_call(
        _ffn_kernel,
        grid=(m // tm, ff // tf),
        in_specs=[pl.BlockSpec((tm, d), lambda i, f: (i, 0)),
                  pl.BlockSpec((1, d), lambda i, f: (0, 0)),
                  pl.BlockSpec((1, 1, d), lambda i, f: (i // tpb, 0, 0)),
                  pl.BlockSpec((1, 1, d), lambda i, f: (i // tpb, 0, 0)),
                  pl.BlockSpec((d, tf), lambda i, f: (0, f)),
                  pl.BlockSpec((d, tf), lambda i, f: (0, f)),
                  pl.BlockSpec((tf, d), lambda i, f: (f, 0)),
                  pl.BlockSpec((1, d), lambda i, f: (0, 0)),
                  pl.BlockSpec((1, 1, d), lambda i, f: (i // tpb, 0, 0))],
        out_specs=pl.BlockSpec((tm, d), lambda i, f: (i, 0)),
        out_shape=jax.ShapeDtypeStruct((m, d), F32),
        scratch_shapes=[pltpu.VMEM((tm, d), BF16), pltpu.VMEM((tm, d), F32)],
        compiler_params=_cparams(("parallel", "arbitrary")),
        name="ffn",
    )(x2, g.reshape(1, d), sc, sh, w1, w3, w2, gpost.reshape(1, d), gt)


def _router_kernel(x_ref, g_ref, sc_ref, sh_ref, rh_ref, rl_ref, h_ref, lg_ref):
    h = _norm_mod(x_ref[...], g_ref[...], sc_ref[0], sh_ref[0])
    hb = h.astype(BF16)
    h_ref[...] = hb
    hl = (h - hb.astype(F32)).astype(BF16)
    rh = rh_ref[...]
    lg_ref[...] = (jnp.dot(hb, rh, preferred_element_type=F32)
                   + jnp.dot(hl, rh, preferred_element_type=F32)
                   + jnp.dot(hb, rl_ref[...], preferred_element_type=F32))


def router(x2, g, sc, sh, r, rows_per_mod, tm=1024):
    m, d = x2.shape
    e = r.shape[1]
    tm = min(tm, m)
    tpb = rows_per_mod // tm
    rh = r.astype(BF16)
    rl = (r - rh.astype(F32)).astype(BF16)
    return pl.pallas_call(
        _router_kernel,
        grid=(m // tm,),
        in_specs=[pl.BlockSpec((tm, d), lambda i: (i, 0)),
                  pl.BlockSpec((1, d), lambda i: (0, 0)),
                  pl.BlockSpec((1, 1, d), lambda i: (i // tpb, 0, 0)),
                  pl.BlockSpec((1, 1, d), lambda i: (i // tpb, 0, 0)),
                  pl.BlockSpec((d, e), lambda i: (0, 0)),
                  pl.BlockSpec((d, e), lambda i: (0, 0))],
        out_specs=[pl.BlockSpec((tm, d), lambda i: (i, 0)),
                   pl.BlockSpec((tm, e), lambda i: (i, 0))],
        out_shape=[jax.ShapeDtypeStruct((m, d), BF16), jax.ShapeDtypeStruct((m, e), F32)],
        compiler_params=_cparams(("parallel",)),
        name="router",
    )(x2, g.reshape(1, d), sc, sh, rh, rl)


def _expert_kernel(te_ref, hs_ref, w1_ref, w3_ref, w2_ref, o_ref, acc_ref):
    i = pl.program_id(0)
    f = pl.program_id(1)
    nf = pl.num_programs(1)
    valid = te_ref[i] >= 0

    @pl.when(f == 0)
    def _():
        acc_ref[...] = jnp.zeros_like(acc_ref)

    @pl.when(valid)
    def _():
        h = hs_ref[...]
        a = jnp.dot(h, w1_ref[0], preferred_element_type=F32)
        b = jnp.dot(h, w3_ref[0], preferred_element_type=F32)
        act = (a * jax.nn.sigmoid(a) * b).astype(BF16)
        acc_ref[...] += jnp.dot(act, w2_ref[0], preferred_element_type=F32)

    @pl.when(f == nf - 1)
    def _():
        o_ref[...] = acc_ref[...].astype(o_ref.dtype)


def experts(tile_expert, hs, w1, w3, w2, tm=512, tf=1792):
    mp, d = hs.shape
    ff = w1.shape[2]

    def wmap(i, f, te):
        return (jnp.maximum(te[i], 0), 0, f)

    def w2map(i, f, te):
        return (jnp.maximum(te[i], 0), f, 0)

    return pl.pallas_call(
        _expert_kernel,
        grid_spec=pltpu.PrefetchScalarGridSpec(
            num_scalar_prefetch=1,
            grid=(mp // tm, ff // tf),
            in_specs=[pl.BlockSpec((tm, d), lambda i, f, te: (i, 0)),
                      pl.BlockSpec((1, d, tf), wmap),
                      pl.BlockSpec((1, d, tf), wmap),
                      pl.BlockSpec((1, tf, d), w2map)],
            out_specs=pl.BlockSpec((tm, d), lambda i, f, te: (i, 0)),
            scratch_shapes=[pltpu.VMEM((tm, d), F32)]),
        out_shape=jax.ShapeDtypeStruct((mp, d), BF16),
        compiler_params=_cparams(("parallel", "arbitrary")),
        name="experts",
    )(tile_expert, hs, w1, w3, w2)


def _combine_kernel(y1_ref, y2_ref, cw_ref, x_ref, gp_ref, gt_ref, o_ref):
    cw = cw_ref[...]
    y = cw[:, 0:1] * y1_ref[...].astype(F32) + cw[:, 1:2] * y2_ref[...].astype(F32)
    o_ref[...] = x_ref[...] + gt_ref[0] * _rms(y, gp_ref[...])


def combine(y1, y2, cw, x2, gpost, gt, rows_per_mod, tm=1024):
    m, d = x2.shape
    tm = min(tm, m)
    tpb = rows_per_mod // tm
    return pl.pallas_call(
        _combine_kernel,
        grid=(m // tm,),
        in_specs=[pl.BlockSpec((tm, d), lambda i: (i, 0)),
                  pl.BlockSpec((tm, d), lambda i: (i, 0)),
                  pl.BlockSpec((tm, TOP_K), lambda i: (i, 0)),
                  pl.BlockSpec((tm, d), lambda i: (i, 0)),
                  pl.BlockSpec((1, d), lambda i: (0, 0)),
                  pl.BlockSpec((1, 1, d), lambda i: (i // tpb, 0, 0))],
        out_specs=pl.BlockSpec((tm, d), lambda i: (i, 0)),
        out_shape=jax.ShapeDtypeStruct((m, d), F32),
        compiler_params=_cparams(("parallel",)),
        name="combine",
    )(y1, y2, cw, x2, gpost.reshape(1, d), gt)


def moe(x2, g, sc, sh, r, w1, w3, w2, gpost, gt, rows_per_mod, tm=512):
    m, d = x2.shape
    e = r.shape[1]
    h, logits = router(x2, g, sc, sh, r, rows_per_mod)
    top_v, top_i = lax.top_k(logits, TOP_K)
    top_w = jax.nn.softmax(top_v, axis=-1)
    flat_e = top_i.reshape(-1)
    onehot = (flat_e[:, None] == jnp.arange(e)[None, :]).astype(jnp.int32)
    rank = jnp.cumsum(onehot, axis=0) - onehot
    counts = jnp.sum(onehot, axis=0)
    padded = ((counts + tm - 1) // tm) * tm
    pad_end = jnp.cumsum(padded)
    pad_start = pad_end - padded
    slot = pad_start[flat_e] + jnp.sum(rank * onehot, axis=1)
    mp = m * TOP_K + e * tm
    n_tiles = mp // tm
    tile_start = jnp.arange(n_tiles, dtype=jnp.int32) * tm
    tile_e = jnp.sum((tile_start[:, None] >= pad_end[None, :]).astype(jnp.int32), axis=1)
    te = jnp.minimum(tile_e, e - 1)
    tile_e = jnp.where(tile_start < pad_end[-1], tile_e, -1).astype(jnp.int32)
    order = jnp.argsort(flat_e, stable=True).astype(jnp.int32)
    first = jnp.cumsum(counts) - counts
    in_group = (tile_start - pad_start[te])[:, None] + jnp.arange(tm, dtype=jnp.int32)[None, :]
    live = (in_group < counts[te][:, None]) & (tile_e >= 0)[:, None]
    src = jnp.where(live, first[te][:, None] + in_group, 0)
    slot_token = order[src.reshape(-1)] // TOP_K
    hs = jnp.take(h, slot_token, axis=0)
    ys = experts(tile_e, hs, w1, w3, w2, tm=tm)
    slot2 = slot.reshape(m, TOP_K)
    y1 = jnp.take(ys, slot2[:, 0], axis=0)
    y2 = jnp.take(ys, slot2[:, 1], axis=0)
    return combine(y1, y2, top_w, x2, gpost, gt, rows_per_mod)


def rope_tables(L, width):
    t = np.arange(L)
    j = np.arange(width) % HEAD_DIM
    pos = np.where(j[None, :] < HEAD_DIM // 2, (t // GRID_W)[:, None], (t % GRID_W)[:, None])
    quarter = HEAD_DIM // 4
    inv = ROPE_BASE ** (-(j % quarter).astype(np.float32) / quarter)
    ang = jnp.asarray(pos.astype(np.float32)) * jnp.asarray(inv.astype(np.float32))[None, :]
    sign = np.where((j % (2 * quarter)) < quarter, -1.0, 1.0).astype(np.float32)
    return jnp.cos(ang), jnp.sin(ang) * jnp.asarray(sign)[None, :]


def _rope_kernel(aq_ref, aqr_ref, ak_ref, akr_ref, cq_ref, sq_ref, ck_ref, sk_ref, q_ref, k_ref):
    q = aq_ref[...].astype(F32) * cq_ref[...] + aqr_ref[...].astype(F32) * sq_ref[...]
    q_ref[...] = q.astype(BF16)
    k = ak_ref[...].astype(F32) * ck_ref[...] + akr_ref[...].astype(F32) * sk_ref[...]
    k_ref[...] = k.astype(BF16)


def rope_qk(p, L, tm=1024):
    m = p.shape[0]
    tm = min(tm, L)
    nt = L // tm
    cq, sq = rope_tables(L, A_Q_W)
    ck, sk = rope_tables(L, A_KV_W)
    return pl.pallas_call(
        _rope_kernel,
        grid=(m // tm,),
        in_specs=[pl.BlockSpec((tm, A_Q_W), lambda i: (i, OFF_AQ // A_Q_W)),
                  pl.BlockSpec((tm, A_Q_W), lambda i: (i, OFF_AQR // A_Q_W)),
                  pl.BlockSpec((tm, A_KV_W), lambda i: (i, OFF_AK // A_KV_W)),
                  pl.BlockSpec((tm, A_KV_W), lambda i: (i, OFF_AKR // A_KV_W)),
                  pl.BlockSpec((tm, A_Q_W), lambda i: (i % nt, 0)),
                  pl.BlockSpec((tm, A_Q_W), lambda i: (i % nt, 0)),
                  pl.BlockSpec((tm, A_KV_W), lambda i: (i % nt, 0)),
                  pl.BlockSpec((tm, A_KV_W), lambda i: (i % nt, 0))],
        out_specs=[pl.BlockSpec((tm, A_Q_W), lambda i: (i, 0)),
                   pl.BlockSpec((tm, A_KV_W), lambda i: (i, 0))],
        out_shape=[jax.ShapeDtypeStruct((m, A_Q_W), BF16), jax.ShapeDtypeStruct((m, A_KV_W), BF16)],
        compiler_params=_cparams(("parallel",)),
        name="rope_qk",
    )(p, p, p, p, cq, sq, ck, sk)


def _dot_nt(a, b):
    return lax.dot_general(a, b, (((1,), (1,)), ((), ())), preferred_element_type=F32)


A_SPAN = A_BLOCK + 2 * WINDOW


def _wa_kernel(sink_ref, q_ref, k_ref, v_ref, kc_ref, vc_ref, o_ref):
    i = pl.program_id(1)
    L = k_ref.shape[1]
    start = pl.multiple_of(jnp.clip(i * A_BLOCK - WINDOW, 0, L - A_SPAN), A_BLOCK)
    qpos = i * A_BLOCK + lax.broadcasted_iota(jnp.int32, (A_BLOCK, A_SPAN), 0)
    kpos = start + lax.broadcasted_iota(jnp.int32, (A_BLOCK, A_SPAN), 1)
    valid = jnp.abs(qpos - kpos) <= WINDOW
    q = q_ref[0]
    scores = []
    for h in range(A_HEADS):
        ks = slice((h // A_GROUP) * HEAD_DIM, (h // A_GROUP + 1) * HEAD_DIM)
        qh = q[:, h * HEAD_DIM:(h + 1) * HEAD_DIM]
        s_loc = jnp.where(valid, _dot_nt(qh, k_ref[0, pl.ds(start, A_SPAN), ks]), NEG)
        scores.append(jnp.concatenate([s_loc, _dot_nt(qh, kc_ref[0, :, ks])], axis=1))
    probs, rdens = [], []
    for h in range(A_HEADS):
        sink = sink_ref[h]
        m = jnp.maximum(jnp.max(scores[h], axis=-1, keepdims=True), sink)
        p = jnp.exp2(scores[h] - m)
        rdens.append(1.0 / (jnp.sum(p, axis=-1, keepdims=True) + jnp.exp2(sink - m)))
        probs.append(p.astype(BF16))
    for h in range(A_HEADS):
        ks = slice((h // A_GROUP) * HEAD_DIM, (h // A_GROUP + 1) * HEAD_DIM)
        v_all = jnp.concatenate([v_ref[0, pl.ds(start, A_SPAN), ks], vc_ref[0, :, ks]], axis=0)
        o = jnp.dot(probs[h], v_all, preferred_element_type=F32)
        o_ref[0, :, h * HEAD_DIM:(h + 1) * HEAD_DIM] = (o * rdens[h]).astype(BF16)


def window_attention(qr, kr, p3, pc3, coff, sink):
    B, L, _ = qr.shape
    Lc = pc3.shape[1]
    return pl.pallas_call(
        _wa_kernel,
        grid=(B, L // A_BLOCK),
        in_specs=[pl.BlockSpec(memory_space=pltpu.SMEM),
                  pl.BlockSpec((1, A_BLOCK, A_Q_W), lambda b, i: (b, i, 0)),
                  pl.BlockSpec((1, L, A_KV_W), lambda b, i: (b, 0, 0)),
                  pl.BlockSpec((1, L, A_KV_W), lambda b, i: (b, 0, OFF_AV // A_KV_W)),
                  pl.BlockSpec((1, Lc, A_KV_W), lambda b, i: (b, 0, (OFF_AK - coff) // A_KV_W)),
                  pl.BlockSpec((1, Lc, A_KV_W), lambda b, i: (b, 0, (OFF_AV - coff) // A_KV_W))],
        out_specs=pl.BlockSpec((1, A_BLOCK, A_Q_W), lambda b, i: (b, i, 0)),
        out_shape=jax.ShapeDtypeStruct((B, L, A_Q_W), BF16),
        compiler_params=_cparams(("parallel", "arbitrary")),
        name="window_attention",
    )(sink.astype(F32) * LOG2E, qr, kr, p3, pc3, pc3)


NA_QROWS = 4
NA_KROWS = NA_ROWS + NA_QROWS
NA_TQ = NA_QROWS * GRID_W
NA_TK = NA_KROWS * GRID_W
NA_HEAD_GROUP = 2


def na_bias_table(rpb, rows):
    col = np.arange(GRID_W)
    col_start = np.clip(col - NA_COLS // 2, 0, GRID_W - NA_COLS)
    col_ok = (col[None, :] >= col_start[:, None]) & (col[None, :] < col_start[:, None] + NA_COLS)
    dc = np.clip(col[None, :] - col[:, None] + NA_COLS - 1, 0, 2 * NA_COLS - 2)
    a = np.arange(NA_QROWS)
    j = np.arange(NA_KROWS)
    dr_all, ok_all = [], []
    for r0, u in ((0, 0), (NA_QROWS, 0), (rows - NA_QROWS, rows - NA_KROWS)):
        r = r0 + a
        rs = np.clip(r - NA_ROWS // 2, 0, rows - NA_ROWS)
        kr = u + j
        ok = (kr[None, :] >= rs[:, None]) & (kr[None, :] < rs[:, None] + NA_ROWS)
        dr = np.clip(kr[None, :] - r[:, None] + NA_ROWS - 1, 0, 2 * NA_ROWS - 2)
        dr_all.append(dr)
        ok_all.append(ok)
    n_dr, n_dc = 2 * NA_ROWS - 1, 2 * NA_COLS - 1
    H = rpb.shape[0]
    onehot = (dc.reshape(-1)[None, :] == np.arange(n_dc)[:, None]).astype(np.float32)
    tiles = jnp.dot(rpb.astype(F32).reshape(H * n_dr, n_dc), jnp.asarray(onehot),
                    precision=lax.Precision.HIGHEST).reshape(H, n_dr, GRID_W, GRID_W)
    tiles = jnp.where(jnp.asarray(col_ok)[None, None], tiles * LOG2E, NEG)
    tiles = jnp.concatenate([tiles, jnp.full((H, 1, GRID_W, GRID_W), NEG, F32)], axis=1)
    sel = np.where(np.stack(ok_all), np.stack(dr_all), n_dr)
    out = tiles[:, sel.reshape(-1)].reshape(H, 3, NA_QROWS, NA_KROWS, GRID_W, GRID_W)
    return out.transpose(1, 0, 2, 4, 3, 5).reshape(3, H, NA_TQ, NA_TK)


def _na_kernel(q_ref, k_ref, v_ref, kc_ref, vc_ref, bias_ref, o_ref):
    blk = pl.program_id(1)
    rows = k_ref.shape[1] // GRID_W
    u = jnp.clip(blk * NA_QROWS - NA_ROWS // 2, 0, rows - NA_KROWS)
    start = pl.multiple_of(u * GRID_W, GRID_W)
    q = q_ref[0]
    for h0 in range(0, B_HEADS, NA_HEAD_GROUP):
        heads = range(h0, h0 + NA_HEAD_GROUP)
        scores = []
        for h in heads:
            hs = slice(h * HEAD_DIM, (h + 1) * HEAD_DIM)
            qh = q[:, hs]
            s_loc = _dot_nt(qh, k_ref[0, pl.ds(start, NA_TK), hs]) + bias_ref[0, h]
            scores.append(jnp.concatenate([s_loc, _dot_nt(qh, kc_ref[0, :, hs])], axis=1))
        probs, rdens = [], []
        for s in scores:
            p = jnp.exp2(s - jnp.max(s, axis=-1, keepdims=True))
            rdens.append(1.0 / jnp.sum(p, axis=-1, keepdims=True))
            probs.append(p.astype(BF16))
        for h, p, rden in zip(heads, probs, rdens):
            hs = slice(h * HEAD_DIM, (h + 1) * HEAD_DIM)
            v_all = jnp.concatenate([v_ref[0, pl.ds(start, NA_TK), hs], vc_ref[0, :, hs]], axis=0)
            o = jnp.dot(p, v_all, preferred_element_type=F32)
            o_ref[0, :, hs] = (o * rden).astype(BF16)


def neighbourhood_attention(p3, pc3, coff, rpb):
    B, L, _ = p3.shape
    Lc = pc3.shape[1]
    nblk = L // NA_TQ
    bias = na_bias_table(rpb, L // GRID_W)

    def bias_map(b, i):
        return (jnp.where(i == 0, 0, jnp.where(i == nblk - 1, 2, 1)), 0, 0, 0)

    return pl.pallas_call(
        _na_kernel,
        grid=(B, nblk),
        in_specs=[pl.BlockSpec((1, NA_TQ, B_W), lambda b, i: (b, i, OFF_BQ // B_W)),
                  pl.BlockSpec((1, L, B_W), lambda b, i: (b, 0, OFF_BK // B_W)),
                  pl.BlockSpec((1, L, B_W), lambda b, i: (b, 0, OFF_BV // B_W)),
                  pl.BlockSpec((1, Lc, B_W), lambda b, i: (b, 0, (OFF_BK - coff) // B_W)),
                  pl.BlockSpec((1, Lc, B_W), lambda b, i: (b, 0, (OFF_BV - coff) // B_W)),
                  pl.BlockSpec((1, B_HEADS, NA_TQ, NA_TK), bias_map)],
        out_specs=pl.BlockSpec((1, NA_TQ, B_W), lambda b, i: (b, i, 0)),
        out_shape=jax.ShapeDtypeStruct((B, L, B_W), BF16),
        compiler_params=_cparams(("parallel", "arbitrary")),
        name="neighbourhood_attention",
    )(p3, p3, p3, pc3, pc3, bias)


def _ctx_attn_kernel(sink_ref, aq_ref, ak_ref, av_ref, bq_ref, bk_ref, bv_ref, oa_ref, ob_ref):
    aq = aq_ref[0]
    bq = bq_ref[0]
    for h in range(A_HEADS):
        hs = slice(h * HEAD_DIM, (h + 1) * HEAD_DIM)
        g = h // A_GROUP
        ks = slice(g * HEAD_DIM, (g + 1) * HEAD_DIM)
        s = _dot_nt(aq[:, hs], ak_ref[0, :, ks])
        sink = sink_ref[h]
        m = jnp.maximum(jnp.max(s, axis=-1, keepdims=True), sink)
        p = jnp.exp2(s - m)
        den = jnp.sum(p, axis=-1, keepdims=True) + jnp.exp2(sink - m)
        o = jnp.dot(p.astype(BF16), av_ref[0, :, ks], preferred_element_type=F32)
        oa_ref[0, :, hs] = (o / den).astype(BF16)
    for h in range(B_HEADS):
        hs = slice(h * HEAD_DIM, (h + 1) * HEAD_DIM)
        s = _dot_nt(bq[:, hs], bk_ref[0, :, hs])
        m = jnp.max(s, axis=-1, keepdims=True)
        p = jnp.exp2(s - m)
        den = jnp.sum(p, axis=-1, keepdims=True)
        o = jnp.dot(p.astype(BF16), bv_ref[0, :, hs], preferred_element_type=F32)
        ob_ref[0, :, hs] = (o / den).astype(BF16)


def ctx_attention(pc3, sink):
    B, Lc, _ = pc3.shape
    return pl.pallas_call(
        _ctx_attn_kernel,
        grid=(B,),
        in_specs=[pl.BlockSpec(memory_space=pltpu.SMEM),
                  pl.BlockSpec((1, Lc, A_Q_W), lambda b: (b, 0, OFF_AQ // A_Q_W)),
                  pl.BlockSpec((1, Lc, A_KV_W), lambda b: (b, 0, OFF_AK // A_KV_W)),
                  pl.BlockSpec((1, Lc, A_KV_W), lambda b: (b, 0, OFF_AV // A_KV_W)),
                  pl.BlockSpec((1, Lc, B_W), lambda b: (b, 0, OFF_BQ // B_W)),
                  pl.BlockSpec((1, Lc, B_W), lambda b: (b, 0, OFF_BK // B_W)),
                  pl.BlockSpec((1, Lc, B_W), lambda b: (b, 0, OFF_BV // B_W))],
        out_specs=[pl.BlockSpec((1, Lc, A_Q_W), lambda b: (b, 0, 0)),
                   pl.BlockSpec((1, Lc, B_W), lambda b: (b, 0, 0))],
        out_shape=[jax.ShapeDtypeStruct((B, Lc, A_Q_W), BF16), jax.ShapeDtypeStruct((B, Lc, B_W), BF16)],
        compiler_params=_cparams(("parallel",)),
        name="ctx_attention",
    )(sink.astype(F32) * LOG2E, pc3, pc3, pc3, pc3, pc3, pc3)


HY_N1 = 128
HY_SUB = 8
HY_CB = 256
HY_S_UNROLL = 16
HY_E_UNROLL = 32


def hyena_mats(L):
    nf = 2 * L
    n2f = nf // HY_N1
    n2h = n2f // 2
    eye = np.eye(HY_SUB)

    def real_block(m):
        return np.block([[m.real, -m.imag], [m.imag, m.real]])

    f = np.exp(-2j * np.pi * np.outer(np.arange(n2f), np.arange(n2h)) / n2f)
    ms1 = real_block(np.kron(f, eye))
    fi = np.exp(2j * np.pi * np.outer(np.arange(n2h), np.arange(n2f)) / n2f) / nf
    ms2 = real_block(np.kron(fi, eye))
    k1 = np.arange(HY_N1)
    n1 = np.arange(HY_N1)
    e1, e2 = [], []
    for k2 in range(n2f):
        e = np.exp(-2j * np.pi * np.outer(n2f * k1 + k2, n1) / nf)
        e1.append(real_block(e))
        e2.append(real_block(np.conj(e).T))
    to = lambda m: jnp.asarray(np.asarray(m, np.float32), dtype=BF16)
    return to(ms1), to(np.stack(e1)), to(np.stack(e2)), to(ms2)


def hyena_spectrum(filt, hy_bias):
    L = filt.shape[0]
    nf = 2 * L
    n2f = nf // HY_N1
    out = []
    for o in range(HY_ORDER):
        hf, hb = filt[:, 0, o], filt[:, 1, o]
        g0 = (hf[0] + hb[0] + hy_bias[o].astype(F32))[None]
        g = jnp.concatenate([g0, hf[1:], jnp.zeros((1, hf.shape[1]), F32), hb[1:][::-1]], axis=0)
        gk = jnp.fft.fft(g, axis=0).reshape(HY_N1, n2f, -1).transpose(1, 0, 2)
        out.append(jnp.concatenate([jnp.real(gk), jnp.imag(gk)], axis=1).astype(F32))
    return out


def _short_conv(u, w_ref, b_ref):
    L = u.shape[0]
    row = lax.broadcasted_iota(jnp.int32, (HY_SUB, u.shape[1]), 0)
    prev = pltpu.roll(u, 1, axis=0)
    prev = jnp.concatenate([jnp.where(row == 0, 0.0, prev[:HY_SUB]), prev[HY_SUB:]], axis=0)
    nxt = pltpu.roll(u, L - 1, axis=0)
    nxt = jnp.concatenate([nxt[:L - HY_SUB], jnp.where(row == HY_SUB - 1, 0.0, nxt[L - HY_SUB:])], axis=0)
    return prev * w_ref[0:1, :] + u * w_ref[1:2, :] + nxt * w_ref[2:3, :] + b_ref[...]


def _hyena_kernel(u_ref, gate_ref, cwu_ref, cbu_ref, cwg_ref, cbg_ref,
                  g_ref, ms1_ref, e1_ref, e2_ref, ms2_ref, o_ref, in_ref, buf_ref, *, conv_u):
    n2h, _, C = u_ref.shape[1:]
    n2f = 2 * n2h
    L = n2h * HY_N1
    half = HY_N1

    for s in range(2):
        u = u_ref[s].reshape(L, C).astype(F32)
        if conv_u:
            u = _short_conv(u, cwu_ref, cbu_ref)
        in_ref[s] = u.reshape(n2h, HY_N1, C)

    def s1_body(j, carry):
        r0 = pl.multiple_of(j * HY_SUB, HY_SUB)
        sre = in_ref[0, :, pl.ds(r0, HY_SUB), :].reshape(n2h * HY_SUB, C)
        sim = in_ref[1, :, pl.ds(r0, HY_SUB), :].reshape(n2h * HY_SUB, C)
        slab = jnp.concatenate([sre, sim], axis=0).astype(BF16)
        out = jnp.dot(ms1_ref[...], slab, preferred_element_type=F32)
        buf_ref[:, pl.ds(r0, HY_SUB), :] = out[:n2f * HY_SUB].reshape(n2f, HY_SUB, C)
        buf_ref[:, pl.ds(half + r0, HY_SUB), :] = out[n2f * HY_SUB:].reshape(n2f, HY_SUB, C)
        return carry

    lax.fori_loop(0, HY_N1 // HY_SUB, s1_body, 0, unroll=HY_S_UNROLL)

    def e_body(k2, carry):
        blk = buf_ref[k2].astype(BF16)
        x = jnp.dot(e1_ref[k2], blk, preferred_element_type=F32)
        xr, xi = x[:half], x[half:]
        gr = g_ref[k2, :half, :]
        gi = g_ref[k2, half:, :]
        y = jnp.concatenate([xr * gr - xi * gi, xr * gi + xi * gr], axis=0).astype(BF16)
        buf_ref[k2] = jnp.dot(e2_ref[k2], y, preferred_element_type=F32)
        return carry

    lax.fori_loop(0, n2f, e_body, 0, unroll=min(n2f, HY_E_UNROLL))

    def s2_body(j, carry):
        r0 = pl.multiple_of(j * HY_SUB, HY_SUB)
        sre = buf_ref[:, pl.ds(r0, HY_SUB), :].reshape(n2f * HY_SUB, C)
        sim = buf_ref[:, pl.ds(half + r0, HY_SUB), :].reshape(n2f * HY_SUB, C)
        slab = jnp.concatenate([sre, sim], axis=0).astype(BF16)
        out = jnp.dot(ms2_ref[...], slab, preferred_element_type=F32)
        in_ref[0, :, pl.ds(r0, HY_SUB), :] = out[:n2h * HY_SUB].reshape(n2h, HY_SUB, C)
        in_ref[1, :, pl.ds(r0, HY_SUB), :] = out[n2h * HY_SUB:].reshape(n2h, HY_SUB, C)
        return carry

    lax.fori_loop(0, HY_N1 // HY_SUB, s2_body, 0, unroll=HY_S_UNROLL)

    for s in range(2):
        gate = _short_conv(gate_ref[s].reshape(L, C).astype(F32), cwg_ref, cbg_ref)
        o_ref[s] = (gate * in_ref[s].reshape(L, C)).reshape(n2h, HY_N1, C).astype(o_ref.dtype)


def _hyena_order(u4, u_blk, p4, gate_blk, cw, cb, g, mats, conv_u):
    B, n2h = u4.shape[0], u4.shape[1]
    n2f = 2 * n2h
    ncb = C_WIDTH // HY_CB
    ms1, e1, e2, ms2 = mats
    cin_blk = OFF_CIN // HY_CB
    one = pl.Buffered(1)

    def tok(col0):
        return pl.BlockSpec((2, n2h, HY_N1, HY_CB), lambda c, i: (i, 0, 0, col0 + c))

    kern = functools.partial(_hyena_kernel, conv_u=conv_u)
    return pl.pallas_call(
        kern,
        grid=(ncb, B // 2),
        in_specs=[tok(u_blk), tok(gate_blk),
                  pl.BlockSpec((3, HY_CB), lambda c, i: (0, c)),
                  pl.BlockSpec((1, HY_CB), lambda c, i: (0, c)),
                  pl.BlockSpec((3, HY_CB), lambda c, i: (0, (gate_blk - cin_blk) + c)),
                  pl.BlockSpec((1, HY_CB), lambda c, i: (0, (gate_blk - cin_blk) + c)),
                  pl.BlockSpec((n2f, 2 * HY_N1, HY_CB), lambda c, i: (0, 0, c), pipeline_mode=one),
                  pl.BlockSpec(ms1.shape, lambda c, i: (0, 0), pipeline_mode=one),
                  pl.BlockSpec(e1.shape, lambda c, i: (0, 0, 0), pipeline_mode=one),
                  pl.BlockSpec(e2.shape, lambda c, i: (0, 0, 0), pipeline_mode=one),
                  pl.BlockSpec(ms2.shape, lambda c, i: (0, 0), pipeline_mode=one)],
        out_specs=pl.BlockSpec((2, n2h, HY_N1, HY_CB), lambda c, i: (i, 0, 0, c)),
        out_shape=jax.ShapeDtypeStruct((B, n2h, HY_N1, C_WIDTH), BF16),
        scratch_shapes=[pltpu.VMEM((2, n2h, HY_N1, HY_CB), F32),
                        pltpu.VMEM((n2f, 2 * HY_N1, HY_CB), F32)],
        compiler_params=_cparams(("arbitrary", "arbitrary")),
        name="hyena_order",
    )(u4, p4, cw, cb.reshape(1, -1), cw, cb.reshape(1, -1), g, ms1, e1, e2, ms2)


def hyena_conv(p3, conv_w, conv_b, g0, g1):
    B, L, W = p3.shape
    n2h = L // HY_N1
    p4 = p3.reshape(B, n2h, HY_N1, W)
    mats = hyena_mats(L)
    cin_blk = OFF_CIN // HY_CB
    ncb = C_WIDTH // HY_CB
    cw, cb = conv_w.astype(F32), conv_b.astype(F32)
    z1 = _hyena_order(p4, cin_blk, p4, cin_blk + ncb, cw, cb, g0, mats, True)
    z2 = _hyena_order(z1, 0, p4, cin_blk + 2 * ncb, cw, cb, g1, mats, False)
    return z2.reshape(B, L, C_WIDTH)


def rope_axis(x, pos):
    half = x.shape[-1] // 2
    inv = ROPE_BASE ** (-jnp.arange(half, dtype=jnp.float32) / half)
    ang = pos.astype(jnp.float32)[:, None] * inv
    cos = jnp.cos(ang)[None, :, None, :].astype(x.dtype)
    sin = jnp.sin(ang)[None, :, None, :].astype(x.dtype)
    a, b = x[..., :half], x[..., half:]
    return jnp.concatenate([a * cos - b * sin, a * sin + b * cos], axis=-1)


def rope_2d(x, rows, cols):
    h = x.shape[-1] // 2
    return jnp.concatenate([rope_axis(x[..., :h], rows), rope_axis(x[..., h:], cols)], axis=-1)


def softmax_with_sink(s, sink):
    m = jnp.maximum(jnp.max(s, axis=-1, keepdims=True), sink)
    p = jnp.exp(s - m)
    return p / (jnp.sum(p, axis=-1, keepdims=True) + jnp.exp(sink - m))


def window_attn_latent(q, k, v, kc, vc, sink):
    B, L = q.shape[0], q.shape[1]
    nb = L // A_BLOCK
    span = A_BLOCK + 2 * WINDOW
    scale = HEAD_DIM ** -0.5
    pad = ((0, 0), (WINDOW, WINDOW), (0, 0), (0, 0))
    kp, vp = jnp.pad(k, pad), jnp.pad(v, pad)
    qg = q.reshape(B, L, A_KV_HEADS, A_GROUP, HEAD_DIM)
    sink_g = sink.astype(jnp.float32).reshape(1, A_KV_HEADS, A_GROUP, 1, 1)

    def block(i):
        q0 = i * A_BLOCK
        qb = lax.dynamic_slice_in_dim(qg, q0, A_BLOCK, axis=1)
        kb = lax.dynamic_slice_in_dim(kp, q0, span, axis=1)
        vb = lax.dynamic_slice_in_dim(vp, q0, span, axis=1)
        s_loc = jnp.einsum('bqgrd,bkgd->bgrqk', qb, kb).astype(jnp.float32) * scale
        s_ctx = jnp.einsum('bqgrd,bcgd->bgrqc', qb, kc).astype(jnp.float32) * scale
        qpos = q0 + jnp.arange(A_BLOCK)
        kpos = q0 - WINDOW + jnp.arange(span)
        valid = ((kpos[None, :] >= 0) & (kpos[None, :] < L)
                 & (jnp.abs(qpos[:, None] - kpos[None, :]) <= WINDOW))
        s_loc = jnp.where(valid, s_loc, NEG)
        p = softmax_with_sink(jnp.concatenate([s_loc, s_ctx], axis=-1), sink_g).astype(v.dtype)
        o = (jnp.einsum('bgrqk,bkgd->bqgrd', p[..., :span], vb)
             + jnp.einsum('bgrqc,bcgd->bqgrd', p[..., span:], vc))
        return o.reshape(B, A_BLOCK, A_Q_W)

    out = lax.map(block, jnp.arange(nb))
    return out.transpose(1, 0, 2, 3).reshape(B, L, A_Q_W)


def ctx_attn_a(qc, kc, vc, sink):
    B, Lc = qc.shape[0], qc.shape[1]
    qg = qc.reshape(B, Lc, A_KV_HEADS, A_GROUP, HEAD_DIM)
    s = jnp.einsum('bqgrd,bkgd->bgrqk', qg, kc).astype(jnp.float32) * HEAD_DIM ** -0.5
    sink_g = sink.astype(jnp.float32).reshape(1, A_KV_HEADS, A_GROUP, 1, 1)
    p = softmax_with_sink(s, sink_g).astype(vc.dtype)
    return jnp.einsum('bgrqk,bkgd->bqgrd', p, vc).reshape(B, Lc, A_Q_W)


def neighbourhood_attn_latent(q, k, v, kc, vc, rpb):
    B, L = q.shape[0], q.shape[1]
    rows = L // GRID_W
    wr = min(NA_ROWS, rows)
    scale = HEAD_DIM ** -0.5
    qg = q.reshape(B, rows, GRID_W, B_HEADS, HEAD_DIM)
    kg = k.reshape(B, rows, GRID_W, B_HEADS, HEAD_DIM)
    vg = v.reshape(B, rows, GRID_W, B_HEADS, HEAD_DIM)
    col = jnp.arange(GRID_W)
    col_start = jnp.clip(col - NA_COLS // 2, 0, GRID_W - NA_COLS)
    col_ok = (col[None, :] >= col_start[:, None]) & (col[None, :] < col_start[:, None] + NA_COLS)
    dc_idx = jnp.clip(col[None, :] - col[:, None] + NA_COLS - 1, 0, 2 * NA_COLS - 2)
    rpb_f = rpb.astype(jnp.float32)
    nloc = wr * GRID_W

    def row(r):
        rs = jnp.clip(r - wr // 2, 0, rows - wr)
        qr = lax.dynamic_index_in_dim(qg, r, axis=1, keepdims=False)
        kr = lax.dynamic_slice_in_dim(kg, rs, wr, axis=1)
        vr = lax.dynamic_slice_in_dim(vg, rs, wr, axis=1)
        dr_idx = rs + jnp.arange(wr) - r + NA_ROWS - 1
        bias = rpb_f[:, dr_idx[None, :, None], dc_idx[:, None, :]]
        s_loc = jnp.einsum('bqhd,bjkhd->bhqjk', qr, kr).astype(jnp.float32) * scale + bias
        s_loc = jnp.where(col_ok[:, None, :], s_loc, NEG).reshape(B, B_HEADS, GRID_W, nloc)
        s_ctx = jnp.einsum('bqhd,bchd->bhqc', qr, kc).astype(jnp.float32) * scale
        p = jax.nn.softmax(jnp.concatenate([s_loc, s_ctx], axis=-1), axis=-1).astype(v.dtype)
        p_loc = p[..., :nloc].reshape(B, B_HEADS, GRID_W, wr, GRID_W)
        o = (jnp.einsum('bhqjk,bjkhd->bqhd', p_loc, vr)
             + jnp.einsum('bhqc,bchd->bqhd', p[..., nloc:], vc))
        return o.reshape(B, GRID_W, B_W)

    out = lax.map(row, jnp.arange(rows))
    return out.transpose(1, 0, 2, 3).reshape(B, L, B_W)


def ctx_attn_b(qc, kc, vc):
    B, Lc = qc.shape[0], qc.shape[1]
    s = jnp.einsum('bqhd,bkhd->bhqk', qc, kc).astype(jnp.float32) * HEAD_DIM ** -0.5
    p = jax.nn.softmax(s, axis=-1).astype(vc.dtype)
    return jnp.einsum('bhqk,bkhd->bqhd', p, vc).reshape(B, Lc, B_W)


def short_conv(u, w, b):
    L = u.shape[1]
    up = jnp.pad(u, ((0, 0), (1, 1), (0, 0)))
    return up[:, :L] * w[0] + up[:, 1:L + 1] * w[1] + up[:, 2:] * w[2] + b


def hyena_filters(L, w1, b1, w2, b2, w3, freq):
    t = jnp.linspace(0.0, 1.0, L, dtype=jnp.float32)[:, None]
    omega = 2.0 * math.pi * jnp.arange(L, dtype=jnp.float32)[:, None] / L
    f = jnp.linspace(1e-4, HY_BANDS - 1, HY_BANDS, dtype=jnp.float32)[None, :]
    z = jnp.concatenate([t, jnp.cos(f * omega), -jnp.sin(f * omega)], axis=-1).astype(w1.dtype)
    h = jnp.sin(freq * (z @ w1 + b1))
    h = jnp.sin(freq * (h @ w2 + b2))
    h = (h @ w3).astype(jnp.float32).reshape(L, HY_DIRS, HY_ORDER, C_WIDTH)
    deltas = jnp.abs(jnp.linspace(math.log(HY_TARGET) / HY_SLOW, math.log(HY_TARGET) / HY_FAST,
                                  C_WIDTH, dtype=jnp.float32))
    h = h * jnp.exp(-t[:, :, None, None] * deltas)
    return h / jnp.sqrt(jnp.sum(h * h, axis=0, keepdims=True) + EPS)


def fft_conv(u, h):
    L = u.shape[1]
    n = 2 * L
    U = jnp.fft.rfft(u, n=n, axis=1)
    H = jnp.fft.rfft(h, n=n, axis=0)
    return jnp.fft.irfft(U * H[None], n=n, axis=1)[:, :L]


def bidir_long_conv(u, h, bias):
    return fft_conv(u, h[:, 0]) + fft_conv(u[:, ::-1], h[:, 1])[:, ::-1] + u * bias


def hyena(p, conv_w, conv_b, filt, hy_bias):
    v, x1, x2 = jnp.split(short_conv(p, conv_w, conv_b), 3, axis=-1)
    z = v.astype(jnp.float32)
    for o, gate in enumerate((x1, x2)):
        z = gate.astype(jnp.float32) * bidir_long_conv(z, filt[:, :, o], hy_bias[o].astype(jnp.float32))
    return z.astype(p.dtype)


def _rope_partner(width):
    j = np.arange(width)
    return np.where((j % 32) < 16, j + 16, j - 16)


def prep_w_in(w):
    ak, av, bk, bv, aq, bq, cin, gates = jnp.split(w, SPLITS, axis=-1)
    aq, bq = aq * Q_SCALE, bq * Q_SCALE
    aqr = aq[:, _rope_partner(A_Q_W)]
    akr = ak[:, _rope_partner(A_KV_W)]
    pad = jnp.zeros((w.shape[0], P_COLS - OFF_AV - A_KV_W), w.dtype)
    return jnp.concatenate([gates, cin, aq, aqr, bq, bk, bv, ak, akr, av, pad], axis=-1).astype(BF16)


def kernel(x, c, ctx, c_ctx, w_mod, b_mod, g_mix_pre, g_mix_post, g_ffn_pre, g_ffn_post,
           w_in, sink_a, rpb_b, conv_c_w, conv_c_b, hy_w1, hy_b1, hy_w2, hy_b2, hy_w3,
           hy_freq, hy_bias, w_branch, w_out, ffn_w1, ffn_w3, ffn_w2,
           moe_router, moe_w1, moe_w3, moe_w2):
    B, L, D = x.shape
    Lc = ctx.shape[1]
    N, Nc = B * L, B * Lc
    x2 = x.reshape(N, D)
    ctx2 = ctx.reshape(Nc, D)
    cs = jnp.concatenate([c, c_ctx[None, :], jnp.zeros((7, D), F32)], axis=0)

    for l in range(DEPTH):
        last = l == DEPTH - 1
        mod = modulation(cs, w_mod[l], b_mod[l])
        mods = [mod[:B, k * D:(k + 1) * D].reshape(B, 1, D) for k in range(6)]
        modc = [mod[B:B + 1, k * D:(k + 1) * D].reshape(1, 1, D) for k in range(6)]
        sh_m, sc_m, gt_m, sh_f, sc_f, gt_f = mods
        shc_m, scc_m, gtc_m, shc_f, scc_f, gtc_f = modc

        w_all = prep_w_in(w_in[l])
        p = proj(x2, g_mix_pre[l], sc_m, sh_m, w_all, L)
        if last:
            pc = proj(ctx2, g_mix_pre[l], scc_m, shc_m, w_all[:, KV_OFF:], Nc)
            coff = KV_OFF
        else:
            pc = proj(ctx2, g_mix_pre[l], scc_m, shc_m, w_all, Nc)
            coff = 0

        p3 = p.reshape(B, L, -1)
        pc3 = pc.reshape(B, Lc, -1)
        qr, kr = rope_qk(p, L)
        ya = window_attention(qr.reshape(B, L, -1), kr.reshape(B, L, -1), p3, pc3, coff, sink_a[l])
        yb = neighbourhood_attention(p3, pc3, coff, rpb_b[l])
        filt = hyena_filters(L, hy_w1[l], hy_b1[l], hy_w2[l], hy_b2[l], hy_w3[l], hy_freq[l])
        g0, g1 = hyena_spectrum(filt, hy_bias[l])
        yc = hyena_conv(p3, conv_c_w[l], conv_c_b[l], g0, g1)
        wb = w_branch[l].astype(BF16)
        wo = w_out[l].astype(BF16)
        x2 = merge(ya.reshape(N, -1).astype(BF16), yb.reshape(N, -1).astype(BF16),
                   yc.reshape(N, -1).astype(BF16), p, wb, wo, x2, g_mix_post[l], gt_m, L)
        if not last:
            yac, ybc = ctx_attention(pc3, sink_a[l])
            filt_c = hyena_filters(Lc, hy_w1[l], hy_b1[l], hy_w2[l], hy_b2[l], hy_w3[l], hy_freq[l])
            gc0, gc1 = hyena_spectrum(filt_c, hy_bias[l])
            ycc = hyena_conv(pc3, conv_c_w[l], conv_c_b[l], gc0, gc1)
            ctx2 = merge(yac.reshape(Nc, -1).astype(BF16), ybc.reshape(Nc, -1).astype(BF16),
                         ycc.reshape(Nc, -1).astype(BF16), pc, wb, wo, ctx2, g_mix_post[l], gtc_m, Nc)

        j = l // 2
        if l % 2 == 0:
            w1, w3, w2 = ffn_w1[j].astype(BF16), ffn_w3[j].astype(BF16), ffn_w2[j].astype(BF16)
            x2 = ffn(x2, g_ffn_pre[l], sc_f, sh_f, w1, w3, w2, g_ffn_post[l], gt_f, L)
            if not last:
                ctx2 = ffn(ctx2, g_ffn_pre[l], scc_f, shc_f, w1, w3, w2, g_ffn_post[l], gtc_f, Nc)
        else:
            w1, w3, w2 = moe_w1[j].astype(BF16), moe_w3[j].astype(BF16), moe_w2[j].astype(BF16)
            x2 = moe(x2, g_ffn_pre[l], sc_f, sh_f, moe_router[j], w1, w3, w2, g_ffn_post[l], gt_f, L)
            if not last:
                ctx2 = moe(ctx2, g_ffn_pre[l], scc_f, shc_f, moe_router[j], w1, w3, w2,
                           g_ffn_post[l], gtc_f, Nc)
    return x2.reshape(B, L, D)
```

```python
import functools
import math

import jax
import jax.numpy as jnp
import numpy as np
from jax import lax
from jax.experimental import pallas as pl
from jax.experimental.pallas import tpu as pltpu

F32 = jnp.float32
BF16 = jnp.bfloat16

D_MODEL = 1024
DEPTH = 2
GRID_W = 64
HEAD_DIM = 64
A_HEADS = 8
A_KV_HEADS = 2
A_GROUP = A_HEADS // A_KV_HEADS
WINDOW = 128
A_BLOCK = 128
B_HEADS = 8
NA_ROWS = 8
NA_COLS = 16
C_WIDTH = 512
HY_ORDER = 2
HY_DIRS = 2
HY_BANDS = 16
HY_TARGET = 1e-2
HY_FAST = 0.3
HY_SLOW = 1.5
N_BRANCH = 3
N_EXPERTS = 8
TOP_K = 2
ROPE_BASE = 10000.0
EPS = 1e-6
NEG = -1e30

A_KV_W = A_KV_HEADS * HEAD_DIM
A_Q_W = A_HEADS * HEAD_DIM
B_W = B_HEADS * HEAD_DIM
KV_COLS = 2 * A_KV_W + 2 * B_W
SPLITS = (A_KV_W, 2 * A_KV_W, 2 * A_KV_W + B_W, KV_COLS,
          KV_COLS + A_Q_W, KV_COLS + A_Q_W + B_W,
          KV_COLS + A_Q_W + B_W + 3 * C_WIDTH)

OFF_GATES = 0
OFF_CIN = 3072
OFF_AQ = 4608
OFF_AQR = 5120
OFF_BQ = 5632
OFF_BK = 6144
OFF_BV = 6656
OFF_AK = 7168
OFF_AKR = 7296
OFF_AV = 7424
P_COLS = 7680
KV_OFF = OFF_BK
KV_W = P_COLS - KV_OFF

VMEM_LIMIT = 48 * 1024 * 1024
LOG2E = math.log2(math.e)
Q_SCALE = HEAD_DIM ** -0.5 * LOG2E


def _cparams(sem):
    return pltpu.CompilerParams(dimension_semantics=sem, vmem_limit_bytes=VMEM_LIMIT)


def _rms(y, g):
    return y * lax.rsqrt(jnp.mean(y * y, axis=-1, keepdims=True) + EPS) * g


def _norm_mod(x, g, sc, sh):
    return _rms(x, g) * (1.0 + sc) + sh


def _mod_kernel(c_ref, w_ref, b_ref, o_ref):
    c = c_ref[...]
    s = c * jax.nn.sigmoid(c)
    o_ref[...] = jnp.dot(s, w_ref[...], preferred_element_type=F32,
                         precision=lax.Precision.HIGHEST) + b_ref[...]


def modulation(cs, w, b, tn=512):
    m, d = cs.shape
    n = w.shape[1]
    return pl.pallas_call(
        _mod_kernel,
        grid=(n // tn,),
        in_specs=[pl.BlockSpec((m, d), lambda j: (0, 0)),
                  pl.BlockSpec((d, tn), lambda j: (0, j)),
                  pl.BlockSpec((1, tn), lambda j: (0, j))],
        out_specs=pl.BlockSpec((m, tn), lambda j: (0, j)),
        out_shape=jax.ShapeDtypeStruct((m, n), F32),
        compiler_params=_cparams(("parallel",)),
        name="modulation",
    )(cs, w, b.reshape(1, n))


def _proj_kernel(x_ref, g_ref, sc_ref, sh_ref, w_ref, o_ref, h_ref):
    @pl.when(pl.program_id(1) == 0)
    def _():
        h_ref[...] = _norm_mod(x_ref[...], g_ref[...], sc_ref[0], sh_ref[0]).astype(BF16)

    o_ref[...] = jnp.dot(h_ref[...], w_ref[...], preferred_element_type=F32).astype(o_ref.dtype)


def proj(x2, g, sc, sh, w, rows_per_mod, tm=1024, tn=3840):
    m, d = x2.shape
    n = w.shape[1]
    tm = min(tm, m)
    tn = min(tn, n)
    tpb = rows_per_mod // tm
    return pl.pallas_call(
        _proj_kernel,
        grid=(m // tm, n // tn),
        in_specs=[pl.BlockSpec((tm, d), lambda i, j: (i, 0)),
                  pl.BlockSpec((1, d), lambda i, j: (0, 0)),
                  pl.BlockSpec((1, 1, d), lambda i, j: (i // tpb, 0, 0)),
                  pl.BlockSpec((1, 1, d), lambda i, j: (i // tpb, 0, 0)),
                  pl.BlockSpec((d, tn), lambda i, j: (0, j))],
        out_specs=pl.BlockSpec((tm, tn), lambda i, j: (i, j)),
        out_shape=jax.ShapeDtypeStruct((m, n), BF16),
        scratch_shapes=[pltpu.VMEM((tm, d), BF16)],
        compiler_params=_cparams(("parallel", "arbitrary")),
        name="proj",
    )(x2, g.reshape(1, d), sc, sh, w)


def _merge_kernel(ya_ref, yb_ref, yc_ref, ga_ref, gb_ref, gc_ref, wb_ref, wo_ref,
                  x_ref, gp_ref, gt_ref, o_ref):
    def branch(y_ref, g_ref, k):
        t = jnp.dot(y_ref[...], wb_ref[k], preferred_element_type=F32)
        return jax.nn.sigmoid(g_ref[...].astype(F32)) * t

    m = branch(ya_ref, ga_ref, 0) + branch(yb_ref, gb_ref, 1) + branch(yc_ref, gc_ref, 2)
    y = jnp.dot(m.astype(BF16), wo_ref[...], preferred_element_type=F32)
    o_ref[...] = x_ref[...] + gt_ref[0] * _rms(y, gp_ref[...])


def merge(ya, yb, yc, p, wb, wo, x2, gpost, gt, rows_per_mod, tm=512):
    m, d = x2.shape
    bw = ya.shape[1]
    tm = min(tm, m)
    tpb = rows_per_mod // tm
    gblk = OFF_GATES // d
    return pl.pallas_call(
        _merge_kernel,
        grid=(m // tm,),
        in_specs=[pl.BlockSpec((tm, bw), lambda i: (i, 0)),
                  pl.BlockSpec((tm, bw), lambda i: (i, 0)),
                  pl.BlockSpec((tm, bw), lambda i: (i, 0)),
                  pl.BlockSpec((tm, d), lambda i: (i, gblk)),
                  pl.BlockSpec((tm, d), lambda i: (i, gblk + 1)),
                  pl.BlockSpec((tm, d), lambda i: (i, gblk + 2)),
                  pl.BlockSpec((N_BRANCH, bw, d), lambda i: (0, 0, 0)),
                  pl.BlockSpec((d, d), lambda i: (0, 0)),
                  pl.BlockSpec((tm, d), lambda i: (i, 0)),
                  pl.BlockSpec((1, d), lambda i: (0, 0)),
                  pl.BlockSpec((1, 1, d), lambda i: (i // tpb, 0, 0))],
        out_specs=pl.BlockSpec((tm, d), lambda i: (i, 0)),
        out_shape=jax.ShapeDtypeStruct((m, d), F32),
        compiler_params=_cparams(("parallel",)),
        name="merge",
    )(ya, yb, yc, p, p, p, wb, wo, x2, gpost.reshape(1, d), gt)


def _ffn_kernel(x_ref, g_ref, sc_ref, sh_ref, w1_ref, w3_ref, w2_ref, gp_ref, gt_ref,
                o_ref, h_ref, acc_ref):
    f = pl.program_id(1)

    @pl.when(f == 0)
    def _():
        h_ref[...] = _norm_mod(x_ref[...], g_ref[...], sc_ref[0], sh_ref[0]).astype(BF16)
        acc_ref[...] = jnp.zeros_like(acc_ref)

    h = h_ref[...]
    a = jnp.dot(h, w1_ref[...], preferred_element_type=F32)
    b = jnp.dot(h, w3_ref[...], preferred_element_type=F32)
    act = (a * jax.nn.sigmoid(a) * b).astype(BF16)
    acc_ref[...] += jnp.dot(act, w2_ref[...], preferred_element_type=F32)

    @pl.when(f == pl.num_programs(1) - 1)
    def _():
        o_ref[...] = x_ref[...] + gt_ref[0] * _rms(acc_ref[...], gp_ref[...])


def ffn(x2, g, sc, sh, w1, w3, w2, gpost, gt, rows_per_mod, tm=512, tf=2816):
    m, d = x2.shape
    ff = w1.shape[1]
    tm = min(tm, m)
    tpb = rows_per_mod // tm
    return pl.pallas_call(
        _ffn_kernel,
        grid=(m // tm, ff // tf),
        in_specs=[pl.BlockSpec((tm, d), lambda i, f: (i, 0)),
                  pl.BlockSpec((1, d), lambda i, f: (0, 0)),
                  pl.BlockSpec((1, 1, d), lambda i, f: (i // tpb, 0, 0)),
                  pl.BlockSpec((1, 1, d), lambda i, f: (i // tpb, 0, 0)),
                  pl.BlockSpec((d, tf), lambda i, f: (0, f)),
                  pl.BlockSpec((d, tf), lambda i, f: (0, f)),
                  pl.BlockSpec((tf, d), lambda i, f: (f, 0)),
                  pl.BlockSpec((1, d), lambda i, f: (0, 0)),
                  pl.BlockSpec((1, 1, d), lambda i, f: (i // tpb, 0, 0))],
        out_specs=pl.BlockSpec((tm, d), lambda i, f: (i, 0)),
        out_shape=jax.ShapeDtypeStruct((m, d), F32),
        scratch_shapes=[pltpu.VMEM((tm, d), BF16), pltpu.VMEM((tm, d), F32)],
        compiler_params=_cparams(("parallel", "arbitrary")),
        name="ffn",
    )(x2, g.reshape(1, d), sc, sh, w1, w3, w2, gpost.reshape(1, d), gt)


def _router_kernel(x_ref, g_ref, sc_ref, sh_ref, rh_ref, rl_ref, h_ref, lg_ref):
    h = _norm_mod(x_ref[...], g_ref[...], sc_ref[0], sh_ref[0])
    hb = h.astype(BF16)
    h_ref[...] = hb
    hl = (h - hb.astype(F32)).astype(BF16)
    rh = rh_ref[...]
    lg_ref[...] = (jnp.dot(hb, rh, preferred_element_type=F32)
                   + jnp.dot(hl, rh, preferred_element_type=F32)
                   + jnp.dot(hb, rl_ref[...], preferred_element_type=F32))


def router(x2, g, sc, sh, r, rows_per_mod, tm=1024):
    m, d = x2.shape
    e = r.shape[1]
    tm = min(tm, m)
    tpb = rows_per_mod // tm
    rh = r.astype(BF16)
    rl = (r - rh.astype(F32)).astype(BF16)
    return pl.pallas_call(
        _router_kernel,
        grid=(m // tm,),
        in_specs=[pl.BlockSpec((tm, d), lambda i: (i, 0)),
                  pl.BlockSpec((1, d), lambda i: (0, 0)),
                  pl.BlockSpec((1, 1, d), lambda i: (i // tpb, 0, 0)),
                  pl.BlockSpec((1, 1, d), lambda i: (i // tpb, 0, 0)),
                  pl.BlockSpec((d, e), lambda i: (0, 0)),
                  pl.BlockSpec((d, e), lambda i: (0, 0))],
        out_specs=[pl.BlockSpec((tm, d), lambda i: (i, 0)),
                   pl.BlockSpec((tm, e), lambda i: (i, 0))],
        out_shape=[jax.ShapeDtypeStruct((m, d), BF16), jax.ShapeDtypeStruct((m, e), F32)],
        compiler_params=_cparams(("parallel",)),
        name="router",
    )(x2, g.reshape(1, d), sc, sh, rh, rl)


def _expert_kernel(te_ref, hs_ref, w1_ref, w3_ref, w2_ref, o_ref, acc_ref):
    i = pl.program_id(0)
    f = pl.program_id(1)
    nf = pl.num_programs(1)
    valid = te_ref[i] >= 0

    @pl.when(f == 0)
    def _():
        acc_ref[...] = jnp.zeros_like(acc_ref)

    @pl.when(valid)
    def _():
        h = hs_ref[...]
        a = jnp.dot(h, w1_ref[0], preferred_element_type=F32)
        b = jnp.dot(h, w3_ref[0], preferred_element_type=F32)
        act = (a * jax.nn.sigmoid(a) * b).astype(BF16)
        acc_ref[...] += jnp.dot(act, w2_ref[0], preferred_element_type=F32)

    @pl.when(f == nf - 1)
    def _():
        o_ref[...] = acc_ref[...].astype(o_ref.dtype)


def experts(tile_expert, hs, w1, w3, w2, tm=512, tf=1792):
    mp, d = hs.shape
    ff = w1.shape[2]

    def wmap(i, f, te):
        return (jnp.maximum(te[i], 0), 0, f)

    def w2map(i, f, te):
        return (jnp.maximum(te[i], 0), f, 0)

    return pl.pallas_call(
        _expert_kernel,
        grid_spec=pltpu.PrefetchScalarGridSpec(
            num_scalar_prefetch=1,
            grid=(mp // tm, ff // tf),
            in_specs=[pl.BlockSpec((tm, d), lambda i, f, te: (i, 0)),
                      pl.BlockSpec((1, d, tf), wmap),
                      pl.BlockSpec((1, d, tf), wmap),
                      pl.BlockSpec((1, tf, d), w2map)],
            out_specs=pl.BlockSpec((tm, d), lambda i, f, te: (i, 0)),
            scratch_shapes=[pltpu.VMEM((tm, d), F32)]),
        out_shape=jax.ShapeDtypeStruct((mp, d), BF16),
        compiler_params=_cparams(("parallel", "arbitrary")),
        name="experts",
    )(tile_expert, hs, w1, w3, w2)


def _combine_kernel(y1_ref, y2_ref, cw_ref, x_ref, gp_ref, gt_ref, o_ref):
    cw = cw_ref[...]
    y = cw[:, 0:1] * y1_ref[...].astype(F32) + cw[:, 1:2] * y2_ref[...].astype(F32)
    o_ref[...] = x_ref[...] + gt_ref[0] * _rms(y, gp_ref[...])


def combine(y1, y2, cw, x2, gpost, gt, rows_per_mod, tm=1024):
    m, d = x2.shape
    tm = min(tm, m)
    tpb = rows_per_mod // tm
    return pl.pallas_call(
        _combine_kernel,
        grid=(m // tm,),
        in_specs=[pl.BlockSpec((tm, d), lambda i: (i, 0)),
                  pl.BlockSpec((tm, d), lambda i: (i, 0)),
                  pl.BlockSpec((tm, TOP_K), lambda i: (i, 0)),
                  pl.BlockSpec((tm, d), lambda i: (i, 0)),
                  pl.BlockSpec((1, d), lambda i: (0, 0)),
                  pl.BlockSpec((1, 1, d), lambda i: (i // tpb, 0, 0))],
        out_specs=pl.BlockSpec((tm, d), lambda i: (i, 0)),
        out_shape=jax.ShapeDtypeStruct((m, d), F32),
        compiler_params=_cparams(("parallel",)),
        name="combine",
    )(y1, y2, cw, x2, gpost.reshape(1, d), gt)


def moe(x2, g, sc, sh, r, w1, w3, w2, gpost, gt, rows_per_mod, tm=512):
    m, d = x2.shape
    e = r.shape[1]
    h, logits = router(x2, g, sc, sh, r, rows_per_mod)
    top_v, top_i = lax.top_k(logits, TOP_K)
    top_w = jax.nn.softmax(top_v, axis=-1)
    flat_e = top_i.reshape(-1)
    onehot = (flat_e[:, None] == jnp.arange(e)[None, :]).astype(jnp.int32)
    rank = jnp.cumsum(onehot, axis=0) - onehot
    counts = jnp.sum(onehot, axis=0)
    padded = ((counts + tm - 1) // tm) * tm
    pad_end = jnp.cumsum(padded)
    pad_start = pad_end - padded
    slot = pad_start[flat_e] + jnp.sum(rank * onehot, axis=1)
    mp = m * TOP_K + e * tm
    n_tiles = mp // tm
    tile_start = jnp.arange(n_tiles, dtype=jnp.int32) * tm
    tile_e = jnp.sum((tile_start[:, None] >= pad_end[None, :]).astype(jnp.int32), axis=1)
    te = jnp.minimum(tile_e, e - 1)
    tile_e = jnp.where(tile_start < pad_end[-1], tile_e, -1).astype(jnp.int32)
    order = jnp.argsort(flat_e, stable=True).astype(jnp.int32)
    first = jnp.cumsum(counts) - counts
    in_group = (tile_start - pad_start[te])[:, None] + jnp.arange(tm, dtype=jnp.int32)[None, :]
    live = (in_group < counts[te][:, None]) & (tile_e >= 0)[:, None]
    src = jnp.where(live, first[te][:, None] + in_group, 0)
    slot_token = order[src.reshape(-1)] // TOP_K
    hs = jnp.take(h, slot_token, axis=0)
    ys = experts(tile_e, hs, w1, w3, w2, tm=tm)
    slot2 = slot.reshape(m, TOP_K)
    y1 = jnp.take(ys, slot2[:, 0], axis=0)
    y2 = jnp.take(ys, slot2[:, 1], axis=0)
    return combine(y1, y2, top_w, x2, gpost, gt, rows_per_mod)


def rope_tables(L, width):
    t = np.arange(L)
    j = np.arange(width) % HEAD_DIM
    pos = np.where(j[None, :] < HEAD_DIM // 2, (t // GRID_W)[:, None], (t % GRID_W)[:, None])
    quarter = HEAD_DIM // 4
    inv = ROPE_BASE ** (-(j % quarter).astype(np.float32) / quarter)
    ang = jnp.asarray(pos.astype(np.float32)) * jnp.asarray(inv.astype(np.float32))[None, :]
    sign = np.where((j % (2 * quarter)) < quarter, -1.0, 1.0).astype(np.float32)
    return jnp.cos(ang), jnp.sin(ang) * jnp.asarray(sign)[None, :]


def _rope_kernel(aq_ref, aqr_ref, ak_ref, akr_ref, cq_ref, sq_ref, ck_ref, sk_ref, q_ref, k_ref):
    q = aq_ref[...].astype(F32) * cq_ref[...] + aqr_ref[...].astype(F32) * sq_ref[...]
    q_ref[...] = q.astype(BF16)
    k = ak_ref[...].astype(F32) * ck_ref[...] + akr_ref[...].astype(F32) * sk_ref[...]
    k_ref[...] = k.astype(BF16)


def rope_qk(p, L, tm=1024):
    m = p.shape[0]
    tm = min(tm, L)
    nt = L // tm
    cq, sq = rope_tables(L, A_Q_W)
    ck, sk = rope_tables(L, A_KV_W)
    return pl.pallas_call(
        _rope_kernel,
        grid=(m // tm,),
        in_specs=[pl.BlockSpec((tm, A_Q_W), lambda i: (i, OFF_AQ // A_Q_W)),
                  pl.BlockSpec((tm, A_Q_W), lambda i: (i, OFF_AQR // A_Q_W)),
                  pl.BlockSpec((tm, A_KV_W), lambda i: (i, OFF_AK // A_KV_W)),
                  pl.BlockSpec((tm, A_KV_W), lambda i: (i, OFF_AKR // A_KV_W)),
                  pl.BlockSpec((tm, A_Q_W), lambda i: (i % nt, 0)),
                  pl.BlockSpec((tm, A_Q_W), lambda i: (i % nt, 0)),
                  pl.BlockSpec((tm, A_KV_W), lambda i: (i % nt, 0)),
                  pl.BlockSpec((tm, A_KV_W), lambda i: (i % nt, 0))],
        out_specs=[pl.BlockSpec((tm, A_Q_W), lambda i: (i, 0)),
                   pl.BlockSpec((tm, A_KV_W), lambda i: (i, 0))],
        out_shape=[jax.ShapeDtypeStruct((m, A_Q_W), BF16), jax.ShapeDtypeStruct((m, A_KV_W), BF16)],
        compiler_params=_cparams(("parallel",)),
        name="rope_qk",
    )(p, p, p, p, cq, sq, ck, sk)


def _dot_nt(a, b):
    return lax.dot_general(a, b, (((1,), (1,)), ((), ())), preferred_element_type=F32)


A_SPAN = A_BLOCK + 2 * WINDOW


def _wa_kernel(sink_ref, q_ref, k_ref, v_ref, kc_ref, vc_ref, o_ref):
    i = pl.program_id(1)
    L = k_ref.shape[1]
    start = pl.multiple_of(jnp.clip(i * A_BLOCK - WINDOW, 0, L - A_SPAN), A_BLOCK)
    qpos = i * A_BLOCK + lax.broadcasted_iota(jnp.int32, (A_BLOCK, A_SPAN), 0)
    kpos = start + lax.broadcasted_iota(jnp.int32, (A_BLOCK, A_SPAN), 1)
    valid = jnp.abs(qpos - kpos) <= WINDOW
    q = q_ref[0]
    scores = []
    for h in range(A_HEADS):
        ks = slice((h // A_GROUP) * HEAD_DIM, (h // A_GROUP + 1) * HEAD_DIM)
        qh = q[:, h * HEAD_DIM:(h + 1) * HEAD_DIM]
        s_loc = jnp.where(valid, _dot_nt(qh, k_ref[0, pl.ds(start, A_SPAN), ks]), NEG)
        scores.append(jnp.concatenate([s_loc, _dot_nt(qh, kc_ref[0, :, ks])], axis=1))
    probs, sinks = [], []
    for h in range(A_HEADS):
        sink = sink_ref[h]
        m = jnp.maximum(jnp.max(scores[h], axis=-1, keepdims=True), sink)
        probs.append(jnp.exp2((scores[h] - m).astype(BF16)))
        sinks.append(jnp.exp2(sink - m))
    ones = jnp.ones((A_SPAN + kc_ref.shape[1], HEAD_DIM), BF16)
    for g in range(A_KV_HEADS):
        ks = slice(g * HEAD_DIM, (g + 1) * HEAD_DIM)
        v_ext = jnp.concatenate(
            [jnp.concatenate([v_ref[0, pl.ds(start, A_SPAN), ks], vc_ref[0, :, ks]], axis=0), ones], axis=1)
        for h in range(g * A_GROUP, (g + 1) * A_GROUP):
            oe = jnp.dot(probs[h], v_ext, preferred_element_type=F32)
            den = pltpu.roll(oe, HEAD_DIM, axis=1) + sinks[h]
            o_ref[0, :, h * HEAD_DIM:(h + 1) * HEAD_DIM] = (oe / den)[:, :HEAD_DIM].astype(BF16)


def window_attention(qr, kr, p3, pc3, coff, sink):
    B, L, _ = qr.shape
    Lc = pc3.shape[1]
    return pl.pallas_call(
        _wa_kernel,
        grid=(B, L // A_BLOCK),
        in_specs=[pl.BlockSpec(memory_space=pltpu.SMEM),
                  pl.BlockSpec((1, A_BLOCK, A_Q_W), lambda b, i: (b, i, 0)),
                  pl.BlockSpec((1, L, A_KV_W), lambda b, i: (b, 0, 0)),
                  pl.BlockSpec((1, L, A_KV_W), lambda b, i: (b, 0, OFF_AV // A_KV_W)),
                  pl.BlockSpec((1, Lc, A_KV_W), lambda b, i: (b, 0, (OFF_AK - coff) // A_KV_W)),
                  pl.BlockSpec((1, Lc, A_KV_W), lambda b, i: (b, 0, (OFF_AV - coff) // A_KV_W))],
        out_specs=pl.BlockSpec((1, A_BLOCK, A_Q_W), lambda b, i: (b, i, 0)),
        out_shape=jax.ShapeDtypeStruct((B, L, A_Q_W), BF16),
        compiler_params=_cparams(("parallel", "arbitrary")),
        name="window_attention",
    )(sink.astype(F32) * LOG2E, qr, kr, p3, pc3, pc3)


NA_QROWS = 4
NA_KROWS = NA_ROWS + NA_QROWS
NA_TQ = NA_QROWS * GRID_W
NA_TK = NA_KROWS * GRID_W
NA_HEAD_GROUP = 8


def na_bias_table(rpb, rows):
    col = np.arange(GRID_W)
    col_start = np.clip(col - NA_COLS // 2, 0, GRID_W - NA_COLS)
    col_ok = (col[None, :] >= col_start[:, None]) & (col[None, :] < col_start[:, None] + NA_COLS)
    dc = np.clip(col[None, :] - col[:, None] + NA_COLS - 1, 0, 2 * NA_COLS - 2)
    a = np.arange(NA_QROWS)
    j = np.arange(NA_KROWS)
    dr_all, ok_all = [], []
    for r0, u in ((0, 0), (NA_QROWS, 0), (rows - NA_QROWS, rows - NA_KROWS)):
        r = r0 + a
        rs = np.clip(r - NA_ROWS // 2, 0, rows - NA_ROWS)
        kr = u + j
        ok = (kr[None, :] >= rs[:, None]) & (kr[None, :] < rs[:, None] + NA_ROWS)
        dr = np.clip(kr[None, :] - r[:, None] + NA_ROWS - 1, 0, 2 * NA_ROWS - 2)
        dr_all.append(dr)
        ok_all.append(ok)
    n_dr, n_dc = 2 * NA_ROWS - 1, 2 * NA_COLS - 1
    H = rpb.shape[0]
    onehot = (dc.reshape(-1)[None, :] == np.arange(n_dc)[:, None]).astype(np.float32)
    tiles = jnp.dot(rpb.astype(F32).reshape(H * n_dr, n_dc), jnp.asarray(onehot),
                    precision=lax.Precision.HIGHEST).reshape(H, n_dr, GRID_W, GRID_W)
    tiles = jnp.where(jnp.asarray(col_ok)[None, None], tiles * LOG2E, NEG)
    tiles = jnp.concatenate([tiles, jnp.full((H, 1, GRID_W, GRID_W), NEG, F32)], axis=1)
    sel = np.where(np.stack(ok_all), np.stack(dr_all), n_dr)
    out = tiles[:, sel.reshape(-1)].reshape(H, 3, NA_QROWS, NA_KROWS, GRID_W, GRID_W)
    return out.transpose(1, 0, 2, 4, 3, 5).reshape(3, H, NA_TQ, NA_TK)


def _na_kernel(q_ref, k_ref, v_ref, kc_ref, vc_ref, bias_ref, o_ref):
    blk = pl.program_id(1)
    rows = k_ref.shape[1] // GRID_W
    u = jnp.clip(blk * NA_QROWS - NA_ROWS // 2, 0, rows - NA_KROWS)
    start = pl.multiple_of(u * GRID_W, GRID_W)
    q = q_ref[0]
    ones = jnp.ones((NA_TK + kc_ref.shape[1], HEAD_DIM), BF16)
    for h0 in range(0, B_HEADS, NA_HEAD_GROUP):
        heads = range(h0, h0 + NA_HEAD_GROUP)
        scores = []
        for h in heads:
            hs = slice(h * HEAD_DIM, (h + 1) * HEAD_DIM)
            qh = q[:, hs]
            s_loc = _dot_nt(qh, k_ref[0, pl.ds(start, NA_TK), hs]) + bias_ref[0, h]
            scores.append(jnp.concatenate([s_loc, _dot_nt(qh, kc_ref[0, :, hs])], axis=1))
        probs = [jnp.exp2((s - jnp.max(s, axis=-1, keepdims=True)).astype(BF16)) for s in scores]
        for h, p in zip(heads, probs):
            hs = slice(h * HEAD_DIM, (h + 1) * HEAD_DIM)
            v_all = jnp.concatenate([v_ref[0, pl.ds(start, NA_TK), hs], vc_ref[0, :, hs]], axis=0)
            oe = jnp.dot(p, jnp.concatenate([v_all, ones], axis=1), preferred_element_type=F32)
            den = pltpu.roll(oe, HEAD_DIM, axis=1)
            o_ref[0, :, hs] = (oe / den)[:, :HEAD_DIM].astype(BF16)


def neighbourhood_attention(p3, pc3, coff, rpb):
    B, L, _ = p3.shape
    Lc = pc3.shape[1]
    nblk = L // NA_TQ
    bias = na_bias_table(rpb, L // GRID_W)

    def bias_map(b, i):
        return (jnp.where(i == 0, 0, jnp.where(i == nblk - 1, 2, 1)), 0, 0, 0)

    return pl.pallas_call(
        _na_kernel,
        grid=(B, nblk),
        in_specs=[pl.BlockSpec((1, NA_TQ, B_W), lambda b, i: (b, i, OFF_BQ // B_W)),
                  pl.BlockSpec((1, L, B_W), lambda b, i: (b, 0, OFF_BK // B_W)),
                  pl.BlockSpec((1, L, B_W), lambda b, i: (b, 0, OFF_BV // B_W)),
                  pl.BlockSpec((1, Lc, B_W), lambda b, i: (b, 0, (OFF_BK - coff) // B_W)),
                  pl.BlockSpec((1, Lc, B_W), lambda b, i: (b, 0, (OFF_BV - coff) // B_W)),
                  pl.BlockSpec((1, B_HEADS, NA_TQ, NA_TK), bias_map)],
        out_specs=pl.BlockSpec((1, NA_TQ, B_W), lambda b, i: (b, i, 0)),
        out_shape=jax.ShapeDtypeStruct((B, L, B_W), BF16),
        compiler_params=_cparams(("parallel", "arbitrary")),
        name="neighbourhood_attention",
    )(p3, p3, p3, pc3, pc3, bias)


def _ctx_attn_kernel(sink_ref, aq_ref, ak_ref, av_ref, bq_ref, bk_ref, bv_ref, oa_ref, ob_ref):
    aq = aq_ref[0]
    bq = bq_ref[0]
    for h in range(A_HEADS):
        hs = slice(h * HEAD_DIM, (h + 1) * HEAD_DIM)
        g = h // A_GROUP
        ks = slice(g * HEAD_DIM, (g + 1) * HEAD_DIM)
        s = _dot_nt(aq[:, hs], ak_ref[0, :, ks])
        sink = sink_ref[h]
        m = jnp.maximum(jnp.max(s, axis=-1, keepdims=True), sink)
        p = jnp.exp2(s - m)
        den = jnp.sum(p, axis=-1, keepdims=True) + jnp.exp2(sink - m)
        o = jnp.dot(p.astype(BF16), av_ref[0, :, ks], preferred_element_type=F32)
        oa_ref[0, :, hs] = (o / den).astype(BF16)
    for h in range(B_HEADS):
        hs = slice(h * HEAD_DIM, (h + 1) * HEAD_DIM)
        s = _dot_nt(bq[:, hs], bk_ref[0, :, hs])
        m = jnp.max(s, axis=-1, keepdims=True)
        p = jnp.exp2(s - m)
        den = jnp.sum(p, axis=-1, keepdims=True)
        o = jnp.dot(p.astype(BF16), bv_ref[0, :, hs], preferred_element_type=F32)
        ob_ref[0, :, hs] = (o / den).astype(BF16)


def ctx_attention(pc3, sink):
    B, Lc, _ = pc3.shape
    return pl.pallas_call(
        _ctx_attn_kernel,
        grid=(B,),
        in_specs=[pl.BlockSpec(memory_space=pltpu.SMEM),
                  pl.BlockSpec((1, Lc, A_Q_W), lambda b: (b, 0, OFF_AQ // A_Q_W)),
                  pl.BlockSpec((1, Lc, A_KV_W), lambda b: (b, 0, OFF_AK // A_KV_W)),
                  pl.BlockSpec((1, Lc, A_KV_W), lambda b: (b, 0, OFF_AV // A_KV_W)),
                  pl.BlockSpec((1, Lc, B_W), lambda b: (b, 0, OFF_BQ // B_W)),
                  pl.BlockSpec((1, Lc, B_W), lambda b: (b, 0, OFF_BK // B_W)),
                  pl.BlockSpec((1, Lc, B_W), lambda b: (b, 0, OFF_BV // B_W))],
        out_specs=[pl.BlockSpec((1, Lc, A_Q_W), lambda b: (b, 0, 0)),
                   pl.BlockSpec((1, Lc, B_W), lambda b: (b, 0, 0))],
        out_shape=[jax.ShapeDtypeStruct((B, Lc, A_Q_W), BF16), jax.ShapeDtypeStruct((B, Lc, B_W), BF16)],
        compiler_params=_cparams(("parallel",)),
        name="ctx_attention",
    )(sink.astype(F32) * LOG2E, pc3, pc3, pc3, pc3, pc3, pc3)


HY_N1 = 128
HY_SUB = 8
HY_CB = 256
HY_S_UNROLL = 16
HY_E_UNROLL = 32


def hyena_mats(L):
    nf = 2 * L
    n2f = nf // HY_N1
    n2h = n2f // 2
    eye = np.eye(HY_SUB)

    def real_block(m):
        return np.block([[m.real, -m.imag], [m.imag, m.real]])

    f = np.exp(-2j * np.pi * np.outer(np.arange(n2f), np.arange(n2h)) / n2f)
    ms1 = real_block(np.kron(f, eye))
    fi = np.exp(2j * np.pi * np.outer(np.arange(n2h), np.arange(n2f)) / n2f) / nf
    ms2 = real_block(np.kron(fi, eye))
    k1 = np.arange(HY_N1)
    n1 = np.arange(HY_N1)
    e1, e2 = [], []
    for k2 in range(n2f):
        e = np.exp(-2j * np.pi * np.outer(n2f * k1 + k2, n1) / nf)
        e1.append(real_block(e))
        e2.append(real_block(np.conj(e).T))
    to = lambda m: jnp.asarray(np.asarray(m, np.float32), dtype=BF16)
    return to(ms1), to(np.stack(e1)), to(np.stack(e2)), to(ms2)


def hyena_spectrum(filt, hy_bias):
    L = filt.shape[0]
    nf = 2 * L
    n2f = nf // HY_N1
    out = []
    for o in range(HY_ORDER):
        hf, hb = filt[:, 0, o], filt[:, 1, o]
        g0 = (hf[0] + hb[0] + hy_bias[o].astype(F32))[None]
        g = jnp.concatenate([g0, hf[1:], jnp.zeros((1, hf.shape[1]), F32), hb[1:][::-1]], axis=0)
        gk = jnp.fft.fft(g, axis=0).reshape(HY_N1, n2f, -1).transpose(1, 0, 2)
        out.append(jnp.concatenate([jnp.real(gk), jnp.imag(gk)], axis=1).astype(F32))
    return out


def _short_conv(u, w_ref, b_ref):
    L = u.shape[0]
    row = lax.broadcasted_iota(jnp.int32, (HY_SUB, u.shape[1]), 0)
    prev = pltpu.roll(u, 1, axis=0)
    prev = jnp.concatenate([jnp.where(row == 0, 0.0, prev[:HY_SUB]), prev[HY_SUB:]], axis=0)
    nxt = pltpu.roll(u, L - 1, axis=0)
    nxt = jnp.concatenate([nxt[:L - HY_SUB], jnp.where(row == HY_SUB - 1, 0.0, nxt[L - HY_SUB:])], axis=0)
    return prev * w_ref[0:1, :] + u * w_ref[1:2, :] + nxt * w_ref[2:3, :] + b_ref[...]


def _hyena_kernel(u_ref, gate_ref, cwu_ref, cbu_ref, cwg_ref, cbg_ref,
                  g_ref, ms1_ref, e1_ref, e2_ref, ms2_ref, o_ref, in_ref, buf_ref, *, conv_u):
    n2h, _, C = u_ref.shape[1:]
    n2f = 2 * n2h
    L = n2h * HY_N1
    half = HY_N1

    for s in range(2):
        u = u_ref[s].reshape(L, C).astype(F32)
        if conv_u:
            u = _short_conv(u, cwu_ref, cbu_ref)
        in_ref[s] = u.reshape(n2h, HY_N1, C)

    def s1_body(j, carry):
        r0 = pl.multiple_of(j * HY_SUB, HY_SUB)
        sre = in_ref[0, :, pl.ds(r0, HY_SUB), :].reshape(n2h * HY_SUB, C)
        sim = in_ref[1, :, pl.ds(r0, HY_SUB), :].reshape(n2h * HY_SUB, C)
        slab = jnp.concatenate([sre, sim], axis=0).astype(BF16)
        out = jnp.dot(ms1_ref[...], slab, preferred_element_type=F32)
        buf_ref[:, pl.ds(r0, HY_SUB), :] = out[:n2f * HY_SUB].reshape(n2f, HY_SUB, C)
        buf_ref[:, pl.ds(half + r0, HY_SUB), :] = out[n2f * HY_SUB:].reshape(n2f, HY_SUB, C)
        return carry

    lax.fori_loop(0, HY_N1 // HY_SUB, s1_body, 0, unroll=HY_S_UNROLL)

    def e_body(k2, carry):
        blk = buf_ref[k2].astype(BF16)
        x = jnp.dot(e1_ref[k2], blk, preferred_element_type=F32)
        xr, xi = x[:half], x[half:]
        gr = g_ref[k2, :half, :]
        gi = g_ref[k2, half:, :]
        y = jnp.concatenate([xr * gr - xi * gi, xr * gi + xi * gr], axis=0).astype(BF16)
        buf_ref[k2] = jnp.dot(e2_ref[k2], y, preferred_element_type=F32)
        return carry

    lax.fori_loop(0, n2f, e_body, 0, unroll=min(n2f, HY_E_UNROLL))

    def s2_body(j, carry):
        r0 = pl.multiple_of(j * HY_SUB, HY_SUB)
        sre = buf_ref[:, pl.ds(r0, HY_SUB), :].reshape(n2f * HY_SUB, C)
        sim = buf_ref[:, pl.ds(half + r0, HY_SUB), :].reshape(n2f * HY_SUB, C)
        slab = jnp.concatenate([sre, sim], axis=0).astype(BF16)
        out = jnp.dot(ms2_ref[...], slab, preferred_element_type=F32)
        in_ref[0, :, pl.ds(r0, HY_SUB), :] = out[:n2h * HY_SUB].reshape(n2h, HY_SUB, C)
        in_ref[1, :, pl.ds(r0, HY_SUB), :] = out[n2h * HY_SUB:].reshape(n2h, HY_SUB, C)
        return carry

    lax.fori_loop(0, HY_N1 // HY_SUB, s2_body, 0, unroll=HY_S_UNROLL)

    for s in range(2):
        gate = _short_conv(gate_ref[s].reshape(L, C).astype(F32), cwg_ref, cbg_ref)
        o_ref[s] = (gate * in_ref[s].reshape(L, C)).reshape(n2h, HY_N1, C).astype(o_ref.dtype)


def _hyena_order(u4, u_blk, p4, gate_blk, cw, cb, g, mats, conv_u):
    B, n2h = u4.shape[0], u4.shape[1]
    n2f = 2 * n2h
    ncb = C_WIDTH // HY_CB
    ms1, e1, e2, ms2 = mats
    cin_blk = OFF_CIN // HY_CB
    one = pl.Buffered(1)

    def tok(col0):
        return pl.BlockSpec((2, n2h, HY_N1, HY_CB), lambda c, i: (i, 0, 0, col0 + c))

    kern = functools.partial(_hyena_kernel, conv_u=conv_u)
    return pl.pallas_call(
        kern,
        grid=(ncb, B // 2),
        in_specs=[tok(u_blk), tok(gate_blk),
                  pl.BlockSpec((3, HY_CB), lambda c, i: (0, c)),
                  pl.BlockSpec((1, HY_CB), lambda c, i: (0, c)),
                  pl.BlockSpec((3, HY_CB), lambda c, i: (0, (gate_blk - cin_blk) + c)),
                  pl.BlockSpec((1, HY_CB), lambda c, i: (0, (gate_blk - cin_blk) + c)),
                  pl.BlockSpec((n2f, 2 * HY_N1, HY_CB), lambda c, i: (0, 0, c), pipeline_mode=one),
                  pl.BlockSpec(ms1.shape, lambda c, i: (0, 0), pipeline_mode=one),
                  pl.BlockSpec(e1.shape, lambda c, i: (0, 0, 0), pipeline_mode=one),
                  pl.BlockSpec(e2.shape, lambda c, i: (0, 0, 0), pipeline_mode=one),
                  pl.BlockSpec(ms2.shape, lambda c, i: (0, 0), pipeline_mode=one)],
        out_specs=pl.BlockSpec((2, n2h, HY_N1, HY_CB), lambda c, i: (i, 0, 0, c)),
        out_shape=jax.ShapeDtypeStruct((B, n2h, HY_N1, C_WIDTH), BF16),
        scratch_shapes=[pltpu.VMEM((2, n2h, HY_N1, HY_CB), F32),
                        pltpu.VMEM((n2f, 2 * HY_N1, HY_CB), F32)],
        compiler_params=_cparams(("arbitrary", "arbitrary")),
        name="hyena_order",
    )(u4, p4, cw, cb.reshape(1, -1), cw, cb.reshape(1, -1), g, ms1, e1, e2, ms2)


def hyena_conv(p3, conv_w, conv_b, g0, g1):
    B, L, W = p3.shape
    n2h = L // HY_N1
    p4 = p3.reshape(B, n2h, HY_N1, W)
    mats = hyena_mats(L)
    cin_blk = OFF_CIN // HY_CB
    ncb = C_WIDTH // HY_CB
    cw, cb = conv_w.astype(F32), conv_b.astype(F32)
    z1 = _hyena_order(p4, cin_blk, p4, cin_blk + ncb, cw, cb, g0, mats, True)
    z2 = _hyena_order(z1, 0, p4, cin_blk + 2 * ncb, cw, cb, g1, mats, False)
    return z2.reshape(B, L, C_WIDTH)


def rope_axis(x, pos):
    half = x.shape[-1] // 2
    inv = ROPE_BASE ** (-jnp.arange(half, dtype=jnp.float32) / half)
    ang = pos.astype(jnp.float32)[:, None] * inv
    cos = jnp.cos(ang)[None, :, None, :].astype(x.dtype)
    sin = jnp.sin(ang)[None, :, None, :].astype(x.dtype)
    a, b = x[..., :half], x[..., half:]
    return jnp.concatenate([a * cos - b * sin, a * sin + b * cos], axis=-1)


def rope_2d(x, rows, cols):
    h = x.shape[-1] // 2
    return jnp.concatenate([rope_axis(x[..., :h], rows), rope_axis(x[..., h:], cols)], axis=-1)


def softmax_with_sink(s, sink):
    m = jnp.maximum(jnp.max(s, axis=-1, keepdims=True), sink)
    p = jnp.exp(s - m)
    return p / (jnp.sum(p, axis=-1, keepdims=True) + jnp.exp(sink - m))


def window_attn_latent(q, k, v, kc, vc, sink):
    B, L = q.shape[0], q.shape[1]
    nb = L // A_BLOCK
    span = A_BLOCK + 2 * WINDOW
    scale = HEAD_DIM ** -0.5
    pad = ((0, 0), (WINDOW, WINDOW), (0, 0), (0, 0))
    kp, vp = jnp.pad(k, pad), jnp.pad(v, pad)
    qg = q.reshape(B, L, A_KV_HEADS, A_GROUP, HEAD_DIM)
    sink_g = sink.astype(jnp.float32).reshape(1, A_KV_HEADS, A_GROUP, 1, 1)

    def block(i):
        q0 = i * A_BLOCK
        qb = lax.dynamic_slice_in_dim(qg, q0, A_BLOCK, axis=1)
        kb = lax.dynamic_slice_in_dim(kp, q0, span, axis=1)
        vb = lax.dynamic_slice_in_dim(vp, q0, span, axis=1)
        s_loc = jnp.einsum('bqgrd,bkgd->bgrqk', qb, kb).astype(jnp.float32) * scale
        s_ctx = jnp.einsum('bqgrd,bcgd->bgrqc', qb, kc).astype(jnp.float32) * scale
        qpos = q0 + jnp.arange(A_BLOCK)
        kpos = q0 - WINDOW + jnp.arange(span)
        valid = ((kpos[None, :] >= 0) & (kpos[None, :] < L)
                 & (jnp.abs(qpos[:, None] - kpos[None, :]) <= WINDOW))
        s_loc = jnp.where(valid, s_loc, NEG)
        p = softmax_with_sink(jnp.concatenate([s_loc, s_ctx], axis=-1), sink_g).astype(v.dtype)
        o = (jnp.einsum('bgrqk,bkgd->bqgrd', p[..., :span], vb)
             + jnp.einsum('bgrqc,bcgd->bqgrd', p[..., span:], vc))
        return o.reshape(B, A_BLOCK, A_Q_W)

    out = lax.map(block, jnp.arange(nb))
    return out.transpose(1, 0, 2, 3).reshape(B, L, A_Q_W)


def ctx_attn_a(qc, kc, vc, sink):
    B, Lc = qc.shape[0], qc.shape[1]
    qg = qc.reshape(B, Lc, A_KV_HEADS, A_GROUP, HEAD_DIM)
    s = jnp.einsum('bqgrd,bkgd->bgrqk', qg, kc).astype(jnp.float32) * HEAD_DIM ** -0.5
    sink_g = sink.astype(jnp.float32).reshape(1, A_KV_HEADS, A_GROUP, 1, 1)
    p = softmax_with_sink(s, sink_g).astype(vc.dtype)
    return jnp.einsum('bgrqk,bkgd->bqgrd', p, vc).reshape(B, Lc, A_Q_W)


def neighbourhood_attn_latent(q, k, v, kc, vc, rpb):
    B, L = q.shape[0], q.shape[1]
    rows = L // GRID_W
    wr = min(NA_ROWS, rows)
    scale = HEAD_DIM ** -0.5
    qg = q.reshape(B, rows, GRID_W, B_HEADS, HEAD_DIM)
    kg = k.reshape(B, rows, GRID_W, B_HEADS, HEAD_DIM)
    vg = v.reshape(B, rows, GRID_W, B_HEADS, HEAD_DIM)
    col = jnp.arange(GRID_W)
    col_start = jnp.clip(col - NA_COLS // 2, 0, GRID_W - NA_COLS)
    col_ok = (col[None, :] >= col_start[:, None]) & (col[None, :] < col_start[:, None] + NA_COLS)
    dc_idx = jnp.clip(col[None, :] - col[:, None] + NA_COLS - 1, 0, 2 * NA_COLS - 2)
    rpb_f = rpb.astype(jnp.float32)
    nloc = wr * GRID_W

    def row(r):
        rs = jnp.clip(r - wr // 2, 0, rows - wr)
        qr = lax.dynamic_index_in_dim(qg, r, axis=1, keepdims=False)
        kr = lax.dynamic_slice_in_dim(kg, rs, wr, axis=1)
        vr = lax.dynamic_slice_in_dim(vg, rs, wr, axis=1)
        dr_idx = rs + jnp.arange(wr) - r + NA_ROWS - 1
        bias = rpb_f[:, dr_idx[None, :, None], dc_idx[:, None, :]]
        s_loc = jnp.einsum('bqhd,bjkhd->bhqjk', qr, kr).astype(jnp.float32) * scale + bias
        s_loc = jnp.where(col_ok[:, None, :], s_loc, NEG).reshape(B, B_HEADS, GRID_W, nloc)
        s_ctx = jnp.einsum('bqhd,bchd->bhqc', qr, kc).astype(jnp.float32) * scale
        p = jax.nn.softmax(jnp.concatenate([s_loc, s_ctx], axis=-1), axis=-1).astype(v.dtype)
        p_loc = p[..., :nloc].reshape(B, B_HEADS, GRID_W, wr, GRID_W)
        o = (jnp.einsum('bhqjk,bjkhd->bqhd', p_loc, vr)
             + jnp.einsum('bhqc,bchd->bqhd', p[..., nloc:], vc))
        return o.reshape(B, GRID_W, B_W)

    out = lax.map(row, jnp.arange(rows))
    return out.transpose(1, 0, 2, 3).reshape(B, L, B_W)


def ctx_attn_b(qc, kc, vc):
    B, Lc = qc.shape[0], qc.shape[1]
    s = jnp.einsum('bqhd,bkhd->bhqk', qc, kc).astype(jnp.float32) * HEAD_DIM ** -0.5
    p = jax.nn.softmax(s, axis=-1).astype(vc.dtype)
    return jnp.einsum('bhqk,bkhd->bqhd', p, vc).reshape(B, Lc, B_W)


def short_conv(u, w, b):
    L = u.shape[1]
    up = jnp.pad(u, ((0, 0), (1, 1), (0, 0)))
    return up[:, :L] * w[0] + up[:, 1:L + 1] * w[1] + up[:, 2:] * w[2] + b


def hyena_filters(L, w1, b1, w2, b2, w3, freq):
    t = jnp.linspace(0.0, 1.0, L, dtype=jnp.float32)[:, None]
    omega = 2.0 * math.pi * jnp.arange(L, dtype=jnp.float32)[:, None] / L
    f = jnp.linspace(1e-4, HY_BANDS - 1, HY_BANDS, dtype=jnp.float32)[None, :]
    z = jnp.concatenate([t, jnp.cos(f * omega), -jnp.sin(f * omega)], axis=-1).astype(w1.dtype)
    h = jnp.sin(freq * (z @ w1 + b1))
    h = jnp.sin(freq * (h @ w2 + b2))
    h = (h @ w3).astype(jnp.float32).reshape(L, HY_DIRS, HY_ORDER, C_WIDTH)
    deltas = jnp.abs(jnp.linspace(math.log(HY_TARGET) / HY_SLOW, math.log(HY_TARGET) / HY_FAST,
                                  C_WIDTH, dtype=jnp.float32))
    h = h * jnp.exp(-t[:, :, None, None] * deltas)
    return h / jnp.sqrt(jnp.sum(h * h, axis=0, keepdims=True) + EPS)


def fft_conv(u, h):
    L = u.shape[1]
    n = 2 * L
    U = jnp.fft.rfft(u, n=n, axis=1)
    H = jnp.fft.rfft(h, n=n, axis=0)
    return jnp.fft.irfft(U * H[None], n=n, axis=1)[:, :L]


def bidir_long_conv(u, h, bias):
    return fft_conv(u, h[:, 0]) + fft_conv(u[:, ::-1], h[:, 1])[:, ::-1] + u * bias


def hyena(p, conv_w, conv_b, filt, hy_bias):
    v, x1, x2 = jnp.split(short_conv(p, conv_w, conv_b), 3, axis=-1)
    z = v.astype(jnp.float32)
    for o, gate in enumerate((x1, x2)):
        z = gate.astype(jnp.float32) * bidir_long_conv(z, filt[:, :, o], hy_bias[o].astype(jnp.float32))
    return z.astype(p.dtype)


def _rope_partner(width):
    j = np.arange(width)
    return np.where((j % 32) < 16, j + 16, j - 16)


def prep_w_in(w):
    ak, av, bk, bv, aq, bq, cin, gates = jnp.split(w, SPLITS, axis=-1)
    aq, bq = aq * Q_SCALE, bq * Q_SCALE
    aqr = aq[:, _rope_partner(A_Q_W)]
    akr = ak[:, _rope_partner(A_KV_W)]
    pad = jnp.zeros((w.shape[0], P_COLS - OFF_AV - A_KV_W), w.dtype)
    return jnp.concatenate([gates, cin, aq, aqr, bq, bk, bv, ak, akr, av, pad], axis=-1).astype(BF16)


def kernel(x, c, ctx, c_ctx, w_mod, b_mod, g_mix_pre, g_mix_post, g_ffn_pre, g_ffn_post,
           w_in, sink_a, rpb_b, conv_c_w, conv_c_b, hy_w1, hy_b1, hy_w2, hy_b2, hy_w3,
           hy_freq, hy_bias, w_branch, w_out, ffn_w1, ffn_w3, ffn_w2,
           moe_router, moe_w1, moe_w3, moe_w2):
    B, L, D = x.shape
    Lc = ctx.shape[1]
    N, Nc = B * L, B * Lc
    x2 = x.reshape(N, D)
    ctx2 = ctx.reshape(Nc, D)
    cs = jnp.concatenate([c, c_ctx[None, :], jnp.zeros((7, D), F32)], axis=0)

    for l in range(DEPTH):
        last = l == DEPTH - 1
        mod = modulation(cs, w_mod[l], b_mod[l])
        mods = [mod[:B, k * D:(k + 1) * D].reshape(B, 1, D) for k in range(6)]
        modc = [mod[B:B + 1, k * D:(k + 1) * D].reshape(1, 1, D) for k in range(6)]
        sh_m, sc_m, gt_m, sh_f, sc_f, gt_f = mods
        shc_m, scc_m, gtc_m, shc_f, scc_f, gtc_f = modc

        w_all = prep_w_in(w_in[l])
        p = proj(x2, g_mix_pre[l], sc_m, sh_m, w_all, L)
        if last:
            pc = proj(ctx2, g_mix_pre[l], scc_m, shc_m, w_all[:, KV_OFF:], Nc)
            coff = KV_OFF
        else:
            pc = proj(ctx2, g_mix_pre[l], scc_m, shc_m, w_all, Nc)
            coff = 0

        p3 = p.reshape(B, L, -1)
        pc3 = pc.reshape(B, Lc, -1)
        qr, kr = rope_qk(p, L)
        ya = window_attention(qr.reshape(B, L, -1), kr.reshape(B, L, -1), p3, pc3, coff, sink_a[l])
        yb = neighbourhood_attention(p3, pc3, coff, rpb_b[l])
        filt = hyena_filters(L, hy_w1[l], hy_b1[l], hy_w2[l], hy_b2[l], hy_w3[l], hy_freq[l])
        g0, g1 = hyena_spectrum(filt, hy_bias[l])
        yc = hyena_conv(p3, conv_c_w[l], conv_c_b[l], g0, g1)
        wb = w_branch[l].astype(BF16)
        wo = w_out[l].astype(BF16)
        x2 = merge(ya.reshape(N, -1).astype(BF16), yb.reshape(N, -1).astype(BF16),
                   yc.reshape(N, -1).astype(BF16), p, wb, wo, x2, g_mix_post[l], gt_m, L)
        if not last:
            yac, ybc = ctx_attention(pc3, sink_a[l])
            filt_c = hyena_filters(Lc, hy_w1[l], hy_b1[l], hy_w2[l], hy_b2[l], hy_w3[l], hy_freq[l])
            gc0, gc1 = hyena_spectrum(filt_c, hy_bias[l])
            ycc = hyena_conv(pc3, conv_c_w[l], conv_c_b[l], gc0, gc1)
            ctx2 = merge(yac.reshape(Nc, -1).astype(BF16), ybc.reshape(Nc, -1).astype(BF16),
                         ycc.reshape(Nc, -1).astype(BF16), pc, wb, wo, ctx2, g_mix_post[l], gtc_m, Nc)

        j = l // 2
        if l % 2 == 0:
            w1, w3, w2 = ffn_w1[j].astype(BF16), ffn_w3[j].astype(BF16), ffn_w2[j].astype(BF16)
            x2 = ffn(x2, g_ffn_pre[l], sc_f, sh_f, w1, w3, w2, g_ffn_post[l], gt_f, L)
            if not last:
                ctx2 = ffn(ctx2, g_ffn_pre[l], scc_f, shc_f, w1, w3, w2, g_ffn_post[l], gtc_f, Nc)
        else:
            w1, w3, w2 = moe_w1[j].astype(BF16), moe_w3[j].astype(BF16), moe_w2[j].astype(BF16)
            x2 = moe(x2, g_ffn_pre[l], sc_f, sh_f, moe_router[j], w1, w3, w2, g_ffn_post[l], gt_f, L)
            if not last:
                ctx2 = moe(ctx2, g_ffn_pre[l], scc_f, shc_f, moe_router[j], w1, w3, w2,
                           g_ffn_post[l], gtc_f, Nc)
    return x2.reshape(B, L, D)
```

```python
import functools
import math

import jax
import jax.numpy as jnp
import numpy as np
from jax import lax
from jax.experimental import pallas as pl
from jax.experimental.pallas import tpu as pltpu

F32 = jnp.float32
BF16 = jnp.bfloat16

D_MODEL = 1024
DEPTH = 2
GRID_W = 64
HEAD_DIM = 64
A_HEADS = 8
A_KV_HEADS = 2
A_GROUP = A_HEADS // A_KV_HEADS
WINDOW = 128
A_BLOCK = 128
B_HEADS = 8
NA_ROWS = 8
NA_COLS = 16
C_WIDTH = 512
HY_ORDER = 2
HY_DIRS = 2
HY_BANDS = 16
HY_TARGET = 1e-2
HY_FAST = 0.3
HY_SLOW = 1.5
N_BRANCH = 3
N_EXPERTS = 8
TOP_K = 2
ROPE_BASE = 10000.0
EPS = 1e-6
NEG = -1e30

A_KV_W = A_KV_HEADS * HEAD_DIM
A_Q_W = A_HEADS * HEAD_DIM
B_W = B_HEADS * HEAD_DIM
KV_COLS = 2 * A_KV_W + 2 * B_W
SPLITS = (A_KV_W, 2 * A_KV_W, 2 * A_KV_W + B_W, KV_COLS,
          KV_COLS + A_Q_W, KV_COLS + A_Q_W + B_W,
          KV_COLS + A_Q_W + B_W + 3 * C_WIDTH)

OFF_GATES = 0
OFF_CIN = 3072
OFF_AQ = 4608
OFF_AQR = 5120
OFF_BQ = 5632
OFF_BK = 6144
OFF_BV = 6656
OFF_AK = 7168
OFF_AKR = 7296
OFF_AV = 7424
P_COLS = 7680
KV_OFF = OFF_BK
KV_W = P_COLS - KV_OFF

VMEM_LIMIT = 48 * 1024 * 1024
LOG2E = math.log2(math.e)
Q_SCALE = HEAD_DIM ** -0.5 * LOG2E


def _cparams(sem):
    return pltpu.CompilerParams(dimension_semantics=sem, vmem_limit_bytes=VMEM_LIMIT)


def _rms(y, g):
    return y * lax.rsqrt(jnp.mean(y * y, axis=-1, keepdims=True) + EPS) * g


def _norm_mod(x, g, sc, sh):
    return _rms(x, g) * (1.0 + sc) + sh


def _mod_kernel(c_ref, w_ref, b_ref, o_ref):
    c = c_ref[...]
    s = c * jax.nn.sigmoid(c)
    o_ref[...] = jnp.dot(s, w_ref[...], preferred_element_type=F32,
                         precision=lax.Precision.HIGHEST) + b_ref[...]


def modulation(cs, w, b, tn=512):
    m, d = cs.shape
    n = w.shape[1]
    return pl.pallas_call(
        _mod_kernel,
        grid=(n // tn,),
        in_specs=[pl.BlockSpec((m, d), lambda j: (0, 0)),
                  pl.BlockSpec((d, tn), lambda j: (0, j)),
                  pl.BlockSpec((1, tn), lambda j: (0, j))],
        out_specs=pl.BlockSpec((m, tn), lambda j: (0, j)),
        out_shape=jax.ShapeDtypeStruct((m, n), F32),
        compiler_params=_cparams(("parallel",)),
        name="modulation",
    )(cs, w, b.reshape(1, n))


def _proj_kernel(x_ref, g_ref, sc_ref, sh_ref, w_ref, o_ref, h_ref):
    @pl.when(pl.program_id(1) == 0)
    def _():
        h_ref[...] = _norm_mod(x_ref[...], g_ref[...], sc_ref[0], sh_ref[0]).astype(BF16)

    o_ref[...] = jnp.dot(h_ref[...], w_ref[...], preferred_element_type=F32).astype(o_ref.dtype)


def proj(x2, g, sc, sh, w, rows_per_mod, tm=1024, tn=3840):
    m, d = x2.shape
    n = w.shape[1]
    tm = min(tm, m)
    tn = min(tn, n)
    tpb = rows_per_mod // tm
    return pl.pallas_call(
        _proj_kernel,
        grid=(m // tm, n // tn),
        in_specs=[pl.BlockSpec((tm, d), lambda i, j: (i, 0)),
                  pl.BlockSpec((1, d), lambda i, j: (0, 0)),
                  pl.BlockSpec((1, 1, d), lambda i, j: (i // tpb, 0, 0)),
                  pl.BlockSpec((1, 1, d), lambda i, j: (i // tpb, 0, 0)),
                  pl.BlockSpec((d, tn), lambda i, j: (0, j))],
        out_specs=pl.BlockSpec((tm, tn), lambda i, j: (i, j)),
        out_shape=jax.ShapeDtypeStruct((m, n), BF16),
        scratch_shapes=[pltpu.VMEM((tm, d), BF16)],
        compiler_params=_cparams(("parallel", "arbitrary")),
        name="proj",
    )(x2, g.reshape(1, d), sc, sh, w)


def _merge_kernel(ya_ref, yb_ref, yc_ref, ga_ref, gb_ref, gc_ref, wb_ref, wo_ref,
                  x_ref, gp_ref, gt_ref, o_ref):
    def branch(y_ref, g_ref, k):
        t = jnp.dot(y_ref[...], wb_ref[k], preferred_element_type=F32)
        return jax.nn.sigmoid(g_ref[...].astype(F32)) * t

    m = branch(ya_ref, ga_ref, 0) + branch(yb_ref, gb_ref, 1) + branch(yc_ref, gc_ref, 2)
    y = jnp.dot(m.astype(BF16), wo_ref[...], preferred_element_type=F32)
    o_ref[...] = x_ref[...] + gt_ref[0] * _rms(y, gp_ref[...])


def merge(ya, yb, yc, p, wb, wo, x2, gpost, gt, rows_per_mod, tm=512):
    m, d = x2.shape
    bw = ya.shape[1]
    tm = min(tm, m)
    tpb = rows_per_mod // tm
    gblk = OFF_GATES // d
    return pl.pallas_call(
        _merge_kernel,
        grid=(m // tm,),
        in_specs=[pl.BlockSpec((tm, bw), lambda i: (i, 0)),
                  pl.BlockSpec((tm, bw), lambda i: (i, 0)),
                  pl.BlockSpec((tm, bw), lambda i: (i, 0)),
                  pl.BlockSpec((tm, d), lambda i: (i, gblk)),
                  pl.BlockSpec((tm, d), lambda i: (i, gblk + 1)),
                  pl.BlockSpec((tm, d), lambda i: (i, gblk + 2)),
                  pl.BlockSpec((N_BRANCH, bw, d), lambda i: (0, 0, 0)),
                  pl.BlockSpec((d, d), lambda i: (0, 0)),
                  pl.BlockSpec((tm, d), lambda i: (i, 0)),
                  pl.BlockSpec((1, d), lambda i: (0, 0)),
                  pl.BlockSpec((1, 1, d), lambda i: (i // tpb, 0, 0))],
        out_specs=pl.BlockSpec((tm, d), lambda i: (i, 0)),
        out_shape=jax.ShapeDtypeStruct((m, d), F32),
        compiler_params=_cparams(("parallel",)),
        name="merge",
    )(ya, yb, yc, p, p, p, wb, wo, x2, gpost.reshape(1, d), gt)


def _ffn_kernel(x_ref, g_ref, sc_ref, sh_ref, w1_ref, w3_ref, w2_ref, gp_ref, gt_ref,
                o_ref, h_ref, acc_ref):
    f = pl.program_id(1)

    @pl.when(f == 0)
    def _():
        h_ref[...] = _norm_mod(x_ref[...], g_ref[...], sc_ref[0], sh_ref[0]).astype(BF16)
        acc_ref[...] = jnp.zeros_like(acc_ref)

    h = h_ref[...]
    a = jnp.dot(h, w1_ref[...], preferred_element_type=F32)
    b = jnp.dot(h, w3_ref[...], preferred_element_type=F32)
    act = (a * jax.nn.sigmoid(a) * b).astype(BF16)
    acc_ref[...] += jnp.dot(act, w2_ref[...], preferred_element_type=F32)

    @pl.when(f == pl.num_programs(1) - 1)
    def _():
        o_ref[...] = x_ref[...] + gt_ref[0] * _rms(acc_ref[...], gp_ref[...])


def ffn(x2, g, sc, sh, w1, w3, w2, gpost, gt, rows_per_mod, tm=512, tf=2816):
    m, d = x2.shape
    ff = w1.shape[1]
    tm = min(tm, m)
    tpb = rows_per_mod // tm
    return pl.pallas_call(
        _ffn_kernel,
        grid=(m // tm, ff // tf),
        in_specs=[pl.BlockSpec((tm, d), lambda i, f: (i, 0)),
                  pl.BlockSpec((1, d), lambda i, f: (0, 0)),
                  pl.BlockSpec((1, 1, d), lambda i, f: (i // tpb, 0, 0)),
                  pl.BlockSpec((1, 1, d), lambda i, f: (i // tpb, 0, 0)),
                  pl.BlockSpec((d, tf), lambda i, f: (0, f)),
                  pl.BlockSpec((d, tf), lambda i, f: (0, f)),
                  pl.BlockSpec((tf, d), lambda i, f: (f, 0)),
                  pl.BlockSpec((1, d), lambda i, f: (0, 0)),
                  pl.BlockSpec((1, 1, d), lambda i, f: (i // tpb, 0, 0))],
        out_specs=pl.BlockSpec((tm, d), lambda i, f: (i, 0)),
        out_shape=jax.ShapeDtypeStruct((m, d), F32),
        scratch_shapes=[pltpu.VMEM((tm, d), BF16), pltpu.VMEM((tm, d), F32)],
        compiler_params=_cparams(("parallel", "arbitrary")),
        name="ffn",
    )(x2, g.reshape(1, d), sc, sh, w1, w3, w2, gpost.reshape(1, d), gt)


def _router_kernel(x_ref, g_ref, sc_ref, sh_ref, rh_ref, rl_ref, h_ref, lg_ref):
    h = _norm_mod(x_ref[...], g_ref[...], sc_ref[0], sh_ref[0])
    hb = h.astype(BF16)
    h_ref[...] = hb
    hl = (h - hb.astype(F32)).astype(BF16)
    rh = rh_ref[...]
    lg_ref[...] = (jnp.dot(hb, rh, preferred_element_type=F32)
                   + jnp.dot(hl, rh, preferred_element_type=F32)
                   + jnp.dot(hb, rl_ref[...], preferred_element_type=F32))


def router(x2, g, sc, sh, r, rows_per_mod, tm=1024):
    m, d = x2.shape
    e = r.shape[1]
    tm = min(tm, m)
    tpb = rows_per_mod // tm
    rh = r.astype(BF16)
    rl = (r - rh.astype(F32)).astype(BF16)
    return pl.pallas_call(
        _router_kernel,
        grid=(m // tm,),
        in_specs=[pl.BlockSpec((tm, d), lambda i: (i, 0)),
                  pl.BlockSpec((1, d), lambda i: (0, 0)),
                  pl.BlockSpec((1, 1, d), lambda i: (i // tpb, 0, 0)),
                  pl.BlockSpec((1, 1, d), lambda i: (i // tpb, 0, 0)),
                  pl.BlockSpec((d, e), lambda i: (0, 0)),
                  pl.BlockSpec((d, e), lambda i: (0, 0))],
        out_specs=[pl.BlockSpec((tm, d), lambda i: (i, 0)),
                   pl.BlockSpec((tm, e), lambda i: (i, 0))],
        out_shape=[jax.ShapeDtypeStruct((m, d), BF16), jax.ShapeDtypeStruct((m, e), F32)],
        compiler_params=_cparams(("parallel",)),
        name="router",
    )(x2, g.reshape(1, d), sc, sh, rh, rl)


def _expert_kernel(te_ref, hs_ref, w1_ref, w3_ref, w2_ref, o_ref, acc_ref):
    i = pl.program_id(0)
    f = pl.program_id(1)
    nf = pl.num_programs(1)
    valid = te_ref[i] >= 0

    @pl.when(f == 0)
    def _():
        acc_ref[...] = jnp.zeros_like(acc_ref)

    @pl.when(valid)
    def _():
        h = hs_ref[...]
        a = jnp.dot(h, w1_ref[0], preferred_element_type=F32)
        b = jnp.dot(h, w3_ref[0], preferred_element_type=F32)
        act = (a * jax.nn.sigmoid(a) * b).astype(BF16)
        acc_ref[...] += jnp.dot(act, w2_ref[0], preferred_element_type=F32)

    @pl.when(f == nf - 1)
    def _():
        o_ref[...] = acc_ref[...].astype(o_ref.dtype)


def experts(tile_expert, hs, w1, w3, w2, tm=512, tf=1792):
    mp, d = hs.shape
    ff = w1.shape[2]

    def wmap(i, f, te):
        return (jnp.maximum(te[i], 0), 0, f)

    def w2map(i, f, te):
        return (jnp.maximum(te[i], 0), f, 0)

    return pl.pallas_call(
        _expert_kernel,
        grid_spec=pltpu.PrefetchScalarGridSpec(
            num_scalar_prefetch=1,
            grid=(mp // tm, ff // tf),
            in_specs=[pl.BlockSpec((tm, d), lambda i, f, te: (i, 0)),
                      pl.BlockSpec((1, d, tf), wmap),
                      pl.BlockSpec((1, d, tf), wmap),
                      pl.BlockSpec((1, tf, d), w2map)],
            out_specs=pl.BlockSpec((tm, d), lambda i, f, te: (i, 0)),
            scratch_shapes=[pltpu.VMEM((tm, d), F32)]),
        out_shape=jax.ShapeDtypeStruct((mp, d), BF16),
        compiler_params=_cparams(("parallel", "arbitrary")),
        name="experts",
    )(tile_expert, hs, w1, w3, w2)


def _combine_kernel(y1_ref, y2_ref, cw_ref, x_ref, gp_ref, gt_ref, o_ref):
    cw = cw_ref[...]
    y = cw[:, 0:1] * y1_ref[...].astype(F32) + cw[:, 1:2] * y2_ref[...].astype(F32)
    o_ref[...] = x_ref[...] + gt_ref[0] * _rms(y, gp_ref[...])


def combine(y1, y2, cw, x2, gpost, gt, rows_per_mod, tm=1024):
    m, d = x2.shape
    tm = min(tm, m)
    tpb = rows_per_mod // tm
    return pl.pallas_call(
        _combine_kernel,
        grid=(m // tm,),
        in_specs=[pl.BlockSpec((tm, d), lambda i: (i, 0)),
                  pl.BlockSpec((tm, d), lambda i: (i, 0)),
                  pl.BlockSpec((tm, TOP_K), lambda i: (i, 0)),
                  pl.BlockSpec((tm, d), lambda i: (i, 0)),
                  pl.BlockSpec((1, d), lambda i: (0, 0)),
                  pl.BlockSpec((1, 1, d), lambda i: (i // tpb, 0, 0))],
        out_specs=pl.BlockSpec((tm, d), lambda i: (i, 0)),
        out_shape=jax.ShapeDtypeStruct((m, d), F32),
        compiler_params=_cparams(("parallel",)),
        name="combine",
    )(y1, y2, cw, x2, gpost.reshape(1, d), gt)


def moe(x2, g, sc, sh, r, w1, w3, w2, gpost, gt, rows_per_mod, tm=512):
    m, d = x2.shape
    e = r.shape[1]
    h, logits = router(x2, g, sc, sh, r, rows_per_mod)
    top_v, top_i = lax.top_k(logits, TOP_K)
    top_w = jax.nn.softmax(top_v, axis=-1)
    flat_e = top_i.reshape(-1)
    onehot = (flat_e[:, None] == jnp.arange(e)[None, :]).astype(jnp.int32)
    rank = jnp.cumsum(onehot, axis=0) - onehot
    counts = jnp.sum(onehot, axis=0)
    padded = ((counts + tm - 1) // tm) * tm
    pad_end = jnp.cumsum(padded)
    pad_start = pad_end - padded
    slot = pad_start[flat_e] + jnp.sum(rank * onehot, axis=1)
    mp = m * TOP_K + e * tm
    n_tiles = mp // tm
    tile_start = jnp.arange(n_tiles, dtype=jnp.int32) * tm
    tile_e = jnp.sum((tile_start[:, None] >= pad_end[None, :]).astype(jnp.int32), axis=1)
    te = jnp.minimum(tile_e, e - 1)
    tile_e = jnp.where(tile_start < pad_end[-1], tile_e, -1).astype(jnp.int32)
    order = jnp.argsort(flat_e, stable=True).astype(jnp.int32)
    first = jnp.cumsum(counts) - counts
    in_group = (tile_start - pad_start[te])[:, None] + jnp.arange(tm, dtype=jnp.int32)[None, :]
    live = (in_group < counts[te][:, None]) & (tile_e >= 0)[:, None]
    src = jnp.where(live, first[te][:, None] + in_group, 0)
    slot_token = order[src.reshape(-1)] // TOP_K
    hs = jnp.take(h, slot_token, axis=0)
    ys = experts(tile_e, hs, w1, w3, w2, tm=tm)
    slot2 = slot.reshape(m, TOP_K)
    y1 = jnp.take(ys, slot2[:, 0], axis=0)
    y2 = jnp.take(ys, slot2[:, 1], axis=0)
    return combine(y1, y2, top_w, x2, gpost, gt, rows_per_mod)


def rope_tables(L, width):
    t = np.arange(L)
    j = np.arange(width) % HEAD_DIM
    pos = np.where(j[None, :] < HEAD_DIM // 2, (t // GRID_W)[:, None], (t % GRID_W)[:, None])
    quarter = HEAD_DIM // 4
    inv = ROPE_BASE ** (-(j % quarter).astype(np.float32) / quarter)
    ang = jnp.asarray(pos.astype(np.float32)) * jnp.asarray(inv.astype(np.float32))[None, :]
    sign = np.where((j % (2 * quarter)) < quarter, -1.0, 1.0).astype(np.float32)
    return jnp.cos(ang), jnp.sin(ang) * jnp.asarray(sign)[None, :]


def _rope_kernel(aq_ref, aqr_ref, ak_ref, akr_ref, cq_ref, sq_ref, ck_ref, sk_ref, q_ref, k_ref):
    q = aq_ref[...].astype(F32) * cq_ref[...] + aqr_ref[...].astype(F32) * sq_ref[...]
    q_ref[...] = q.astype(BF16)
    k = ak_ref[...].astype(F32) * ck_ref[...] + akr_ref[...].astype(F32) * sk_ref[...]
    k_ref[...] = k.astype(BF16)


def rope_qk(p, L, tm=1024):
    m = p.shape[0]
    tm = min(tm, L)
    nt = L // tm
    cq, sq = rope_tables(L, A_Q_W)
    ck, sk = rope_tables(L, A_KV_W)
    return pl.pallas_call(
        _rope_kernel,
        grid=(m // tm,),
        in_specs=[pl.BlockSpec((tm, A_Q_W), lambda i: (i, OFF_AQ // A_Q_W)),
                  pl.BlockSpec((tm, A_Q_W), lambda i: (i, OFF_AQR // A_Q_W)),
                  pl.BlockSpec((tm, A_KV_W), lambda i: (i, OFF_AK // A_KV_W)),
                  pl.BlockSpec((tm, A_KV_W), lambda i: (i, OFF_AKR // A_KV_W)),
                  pl.BlockSpec((tm, A_Q_W), lambda i: (i % nt, 0)),
                  pl.BlockSpec((tm, A_Q_W), lambda i: (i % nt, 0)),
                  pl.BlockSpec((tm, A_KV_W), lambda i: (i % nt, 0)),
                  pl.BlockSpec((tm, A_KV_W), lambda i: (i % nt, 0))],
        out_specs=[pl.BlockSpec((tm, A_Q_W), lambda i: (i, 0)),
                   pl.BlockSpec((tm, A_KV_W), lambda i: (i, 0))],
        out_shape=[jax.ShapeDtypeStruct((m, A_Q_W), BF16), jax.ShapeDtypeStruct((m, A_KV_W), BF16)],
        compiler_params=_cparams(("parallel",)),
        name="rope_qk",
    )(p, p, p, p, cq, sq, ck, sk)


def _dot_nt(a, b):
    return lax.dot_general(a, b, (((1,), (1,)), ((), ())), preferred_element_type=F32)


A_SPAN = A_BLOCK + 2 * WINDOW


def _wa_kernel(sink_ref, q_ref, k_ref, v_ref, kc_ref, vc_ref, o_ref):
    i = pl.program_id(1)
    L = k_ref.shape[1]
    start = pl.multiple_of(jnp.clip(i * A_BLOCK - WINDOW, 0, L - A_SPAN), A_BLOCK)
    qpos = i * A_BLOCK + lax.broadcasted_iota(jnp.int32, (A_BLOCK, A_SPAN), 0)
    kpos = start + lax.broadcasted_iota(jnp.int32, (A_BLOCK, A_SPAN), 1)
    valid = jnp.abs(qpos - kpos) <= WINDOW
    q = q_ref[0]
    scores = []
    for h in range(A_HEADS):
        ks = slice((h // A_GROUP) * HEAD_DIM, (h // A_GROUP + 1) * HEAD_DIM)
        qh = q[:, h * HEAD_DIM:(h + 1) * HEAD_DIM]
        s_loc = jnp.where(valid, _dot_nt(qh, k_ref[0, pl.ds(start, A_SPAN), ks]), NEG)
        scores.append(jnp.concatenate([s_loc, _dot_nt(qh, kc_ref[0, :, ks])], axis=1))
    probs, sinks = [], []
    for h in range(A_HEADS):
        sink = sink_ref[h]
        m = jnp.maximum(jnp.max(scores[h], axis=-1, keepdims=True), sink)
        probs.append(jnp.exp2((scores[h] - m).astype(BF16)))
        sinks.append(jnp.exp2(sink - m))
    ones = jnp.ones((A_SPAN + kc_ref.shape[1], HEAD_DIM), BF16)
    for g in range(A_KV_HEADS):
        ks = slice(g * HEAD_DIM, (g + 1) * HEAD_DIM)
        v_ext = jnp.concatenate(
            [jnp.concatenate([v_ref[0, pl.ds(start, A_SPAN), ks], vc_ref[0, :, ks]], axis=0), ones], axis=1)
        for h in range(g * A_GROUP, (g + 1) * A_GROUP):
            oe = jnp.dot(probs[h], v_ext, preferred_element_type=F32)
            den = pltpu.roll(oe, HEAD_DIM, axis=1) + sinks[h]
            o_ref[0, :, h * HEAD_DIM:(h + 1) * HEAD_DIM] = (oe / den)[:, :HEAD_DIM].astype(BF16)


def window_attention(qr, kr, p3, pc3, coff, sink):
    B, L, _ = qr.shape
    Lc = pc3.shape[1]
    return pl.pallas_call(
        _wa_kernel,
        grid=(B, L // A_BLOCK),
        in_specs=[pl.BlockSpec(memory_space=pltpu.SMEM),
                  pl.BlockSpec((1, A_BLOCK, A_Q_W), lambda b, i: (b, i, 0)),
                  pl.BlockSpec((1, L, A_KV_W), lambda b, i: (b, 0, 0)),
                  pl.BlockSpec((1, L, A_KV_W), lambda b, i: (b, 0, OFF_AV // A_KV_W)),
                  pl.BlockSpec((1, Lc, A_KV_W), lambda b, i: (b, 0, (OFF_AK - coff) // A_KV_W)),
                  pl.BlockSpec((1, Lc, A_KV_W), lambda b, i: (b, 0, (OFF_AV - coff) // A_KV_W))],
        out_specs=pl.BlockSpec((1, A_BLOCK, A_Q_W), lambda b, i: (b, i, 0)),
        out_shape=jax.ShapeDtypeStruct((B, L, A_Q_W), BF16),
        compiler_params=_cparams(("parallel", "arbitrary")),
        name="window_attention",
    )(sink.astype(F32) * LOG2E, qr, kr, p3, pc3, pc3)


NA_QR, NA_QC = 8, 16
NA_KR, NA_KC = NA_QR + NA_ROWS, NA_QC + NA_COLS
NA_TQ, NA_TK = NA_QR * NA_QC, NA_KR * NA_KC
NA_CPAD = NA_COLS // 2
NA_TAIL = NA_KC - NA_CPAD


def na_bias_table(rpb, rows):
    n_dr, n_dc = 2 * NA_ROWS - 1, 2 * NA_COLS - 1
    H = rpb.shape[0]
    a, j = np.arange(NA_QR), np.arange(NA_KR)
    sel = []
    for r0, u in ((0, 0), (NA_QR, NA_QR - NA_ROWS // 2), (rows - NA_QR, rows - NA_KR)):
        r = r0 + a
        rs = np.clip(r - NA_ROWS // 2, 0, rows - NA_ROWS)
        kr = u + j
        ok = (kr[None, :] >= rs[:, None]) & (kr[None, :] < rs[:, None] + NA_ROWS)
        dr = np.clip(kr[None, :] - r[:, None] + NA_ROWS - 1, 0, n_dr - 1)
        sel.append(np.where(ok, dr, n_dr))
    sel = np.stack(sel)
    b, e = np.arange(NA_QC), np.arange(NA_KC)
    ncb = GRID_W // NA_QC
    dcs, oks = [], []
    for m in (0, 1, ncb - 1):
        qc = NA_QC * m + b
        kc = NA_QC * m - NA_CPAD + e
        cs = np.clip(qc - NA_COLS // 2, 0, GRID_W - NA_COLS)
        oks.append((kc[None, :] >= cs[:, None]) & (kc[None, :] < cs[:, None] + NA_COLS)
                   & (kc[None, :] >= 0) & (kc[None, :] < GRID_W))
        dcs.append(np.clip(kc[None, :] - qc[:, None] + NA_COLS - 1, 0, n_dc - 1))
    dc = np.stack(dcs).reshape(-1)
    onehot = (dc[None, :] == np.arange(n_dc)[:, None]).astype(np.float32)
    tiles = jnp.dot(rpb.astype(F32).reshape(H * n_dr, n_dc), jnp.asarray(onehot),
                    precision=lax.Precision.HIGHEST).reshape(H, n_dr, 3, NA_QC, NA_KC)
    tiles = jnp.where(jnp.asarray(np.stack(oks))[None, None], tiles * LOG2E, NEG)
    tiles = jnp.concatenate([tiles, jnp.full((H, 1, 3, NA_QC, NA_KC), NEG, F32)], axis=1)
    out = tiles[:, sel.reshape(-1)].reshape(H, 3, NA_QR, NA_KR, 3, NA_QC, NA_KC)
    return out.transpose(1, 4, 0, 2, 5, 3, 6).reshape(3, 3, H, NA_TQ, NA_TK)


def _na_kernel(q_ref, k_ref, v_ref, kc_ref, vc_ref, bias_ref, o_ref):
    rb = pl.program_id(1)
    m = pl.program_id(2)
    rows = (k_ref.shape[1] - NA_CPAD - NA_TAIL) // GRID_W
    u = jnp.clip(rb * NA_QR - NA_ROWS // 2, 0, rows - NA_KR)
    q = q_ref[0].reshape(NA_TQ, B_W)
    ksegs, vsegs = [], []
    for j in range(NA_KR):
        st = pl.multiple_of((u + j) * GRID_W + NA_QC * m, NA_QC)
        ksegs.append(k_ref[0, pl.ds(st, NA_KC), :])
        vsegs.append(v_ref[0, pl.ds(st, NA_KC), :])
    kt = jnp.concatenate(ksegs + [kc_ref[0]], axis=0)
    vt = jnp.concatenate(vsegs + [vc_ref[0]], axis=0)
    pair = 2 * HEAD_DIM
    lane = lax.broadcasted_iota(jnp.int32, (NA_TQ, pair), 1)
    ones = jnp.ones((kt.shape[0], pair), BF16)
    zero = jnp.zeros((), BF16)
    scores = []
    for h in range(B_HEADS):
        ps = slice((h // 2) * pair, (h // 2 + 1) * pair)
        qh = jnp.where((lane < HEAD_DIM) if h % 2 == 0 else (lane >= HEAD_DIM), q[:, ps], zero)
        s = _dot_nt(qh, kt[:, ps])
        scores.append(jnp.concatenate([s[:, :NA_TK] + bias_ref[0, 0, h], s[:, NA_TK:]], axis=1))
    probs = [jnp.exp2((s - jnp.max(s, axis=-1, keepdims=True)).astype(BF16)) for s in scores]
    outs = []
    for i in range(B_HEADS // 2):
        ps = slice(i * pair, (i + 1) * pair)
        v_ext = jnp.concatenate([vt[:, ps], ones], axis=1)
        oe = [jnp.dot(probs[2 * i + t], v_ext, preferred_element_type=F32) for t in range(2)]
        o = [x[:, :pair] / x[:, pair:] for x in oe]
        outs.append(jnp.where(lane < HEAD_DIM, o[0], o[1]).astype(BF16))
    o_ref[0] = jnp.concatenate(outs, axis=1).reshape(NA_QR, NA_QC, B_W)


def neighbourhood_attention(p3, pc3, coff, rpb):
    B, L, wp = p3.shape
    Lc = pc3.shape[1]
    rows = L // GRID_W
    nrb, ncb = rows // NA_QR, GRID_W // NA_QC
    bias = na_bias_table(rpb, rows)
    shift = ((0, 0), (NA_CPAD, NA_TAIL), (0, 0))
    kp = jnp.pad(p3[:, :, OFF_BK:OFF_BK + B_W], shift)
    vp = jnp.pad(p3[:, :, OFF_BV:OFF_BV + B_W], shift)
    lp = L + NA_CPAD + NA_TAIL

    def edge_case(i, n):
        return jnp.where(i == 0, 0, jnp.where(i == n - 1, 2, 1))

    out = pl.pallas_call(
        _na_kernel,
        grid=(B, nrb, ncb),
        in_specs=[pl.BlockSpec((1, NA_QR, NA_QC, B_W), lambda b, r, m: (b, r, m, OFF_BQ // B_W)),
                  pl.BlockSpec((1, lp, B_W), lambda b, r, m: (b, 0, 0)),
                  pl.BlockSpec((1, lp, B_W), lambda b, r, m: (b, 0, 0)),
                  pl.BlockSpec((1, Lc, B_W), lambda b, r, m: (b, 0, (OFF_BK - coff) // B_W)),
                  pl.BlockSpec((1, Lc, B_W), lambda b, r, m: (b, 0, (OFF_BV - coff) // B_W)),
                  pl.BlockSpec((1, 1, B_HEADS, NA_TQ, NA_TK),
                               lambda b, r, m: (edge_case(r, nrb), edge_case(m, ncb), 0, 0, 0))],
        out_specs=pl.BlockSpec((1, NA_QR, NA_QC, B_W), lambda b, r, m: (b, r, m, 0)),
        out_shape=jax.ShapeDtypeStruct((B, rows, GRID_W, B_W), BF16),
        compiler_params=_cparams(("parallel", "arbitrary", "arbitrary")),
        name="neighbourhood_attention",
    )(p3.reshape(B, rows, GRID_W, wp), kp, vp, pc3, pc3, bias)
    return out.reshape(B, L, B_W)


def _ctx_attn_kernel(sink_ref, aq_ref, ak_ref, av_ref, bq_ref, bk_ref, bv_ref, oa_ref, ob_ref):
    aq = aq_ref[0]
    bq = bq_ref[0]
    for h in range(A_HEADS):
        hs = slice(h * HEAD_DIM, (h + 1) * HEAD_DIM)
        g = h // A_GROUP
        ks = slice(g * HEAD_DIM, (g + 1) * HEAD_DIM)
        s = _dot_nt(aq[:, hs], ak_ref[0, :, ks])
        sink = sink_ref[h]
        m = jnp.maximum(jnp.max(s, axis=-1, keepdims=True), sink)
        p = jnp.exp2(s - m)
        den = jnp.sum(p, axis=-1, keepdims=True) + jnp.exp2(sink - m)
        o = jnp.dot(p.astype(BF16), av_ref[0, :, ks], preferred_element_type=F32)
        oa_ref[0, :, hs] = (o / den).astype(BF16)
    for h in range(B_HEADS):
        hs = slice(h * HEAD_DIM, (h + 1) * HEAD_DIM)
        s = _dot_nt(bq[:, hs], bk_ref[0, :, hs])
        m = jnp.max(s, axis=-1, keepdims=True)
        p = jnp.exp2(s - m)
        den = jnp.sum(p, axis=-1, keepdims=True)
        o = jnp.dot(p.astype(BF16), bv_ref[0, :, hs], preferred_element_type=F32)
        ob_ref[0, :, hs] = (o / den).astype(BF16)


def ctx_attention(pc3, sink):
    B, Lc, _ = pc3.shape
    return pl.pallas_call(
        _ctx_attn_kernel,
        grid=(B,),
        in_specs=[pl.BlockSpec(memory_space=pltpu.SMEM),
                  pl.BlockSpec((1, Lc, A_Q_W), lambda b: (b, 0, OFF_AQ // A_Q_W)),
                  pl.BlockSpec((1, Lc, A_KV_W), lambda b: (b, 0, OFF_AK // A_KV_W)),
                  pl.BlockSpec((1, Lc, A_KV_W), lambda b: (b, 0, OFF_AV // A_KV_W)),
                  pl.BlockSpec((1, Lc, B_W), lambda b: (b, 0, OFF_BQ // B_W)),
                  pl.BlockSpec((1, Lc, B_W), lambda b: (b, 0, OFF_BK // B_W)),
                  pl.BlockSpec((1, Lc, B_W), lambda b: (b, 0, OFF_BV // B_W))],
        out_specs=[pl.BlockSpec((1, Lc, A_Q_W), lambda b: (b, 0, 0)),
                   pl.BlockSpec((1, Lc, B_W), lambda b: (b, 0, 0))],
        out_shape=[jax.ShapeDtypeStruct((B, Lc, A_Q_W), BF16), jax.ShapeDtypeStruct((B, Lc, B_W), BF16)],
        compiler_params=_cparams(("parallel",)),
        name="ctx_attention",
    )(sink.astype(F32) * LOG2E, pc3, pc3, pc3, pc3, pc3, pc3)


HY_N1 = 128
HY_SUB = 8
HY_CB = 256
HY_S_UNROLL = 16
HY_E_UNROLL = 32


def hyena_mats(L):
    nf = 2 * L
    n2f = nf // HY_N1
    n2h = n2f // 2
    eye = np.eye(HY_SUB)

    def real_block(m):
        return np.block([[m.real, -m.imag], [m.imag, m.real]])

    f = np.exp(-2j * np.pi * np.outer(np.arange(n2f), np.arange(n2h)) / n2f)
    ms1 = real_block(np.kron(f, eye))
    fi = np.exp(2j * np.pi * np.outer(np.arange(n2h), np.arange(n2f)) / n2f) / nf
    ms2 = real_block(np.kron(fi, eye))
    k1 = np.arange(HY_N1)
    n1 = np.arange(HY_N1)
    e1, e2 = [], []
    for k2 in range(n2f):
        e = np.exp(-2j * np.pi * np.outer(n2f * k1 + k2, n1) / nf)
        e1.append(real_block(e))
        e2.append(real_block(np.conj(e).T))
    to = lambda m: jnp.asarray(np.asarray(m, np.float32), dtype=BF16)
    return to(ms1), to(np.stack(e1)), to(np.stack(e2)), to(ms2)


def hyena_spectrum(filt, hy_bias):
    L = filt.shape[0]
    nf = 2 * L
    n2f = nf // HY_N1
    out = []
    for o in range(HY_ORDER):
        hf, hb = filt[:, 0, o], filt[:, 1, o]
        g0 = (hf[0] + hb[0] + hy_bias[o].astype(F32))[None]
        g = jnp.concatenate([g0, hf[1:], jnp.zeros((1, hf.shape[1]), F32), hb[1:][::-1]], axis=0)
        gk = jnp.fft.fft(g, axis=0).reshape(HY_N1, n2f, -1).transpose(1, 0, 2)
        out.append(jnp.concatenate([jnp.real(gk), jnp.imag(gk)], axis=1).astype(F32))
    return out


def _short_conv(u, w_ref, b_ref):
    L = u.shape[0]
    row = lax.broadcasted_iota(jnp.int32, (HY_SUB, u.shape[1]), 0)
    prev = pltpu.roll(u, 1, axis=0)
    prev = jnp.concatenate([jnp.where(row == 0, 0.0, prev[:HY_SUB]), prev[HY_SUB:]], axis=0)
    nxt = pltpu.roll(u, L - 1, axis=0)
    nxt = jnp.concatenate([nxt[:L - HY_SUB], jnp.where(row == HY_SUB - 1, 0.0, nxt[L - HY_SUB:])], axis=0)
    return prev * w_ref[0:1, :] + u * w_ref[1:2, :] + nxt * w_ref[2:3, :] + b_ref[...]


def _hyena_kernel(u_ref, gate_ref, cwu_ref, cbu_ref, cwg_ref, cbg_ref,
                  g_ref, ms1_ref, e1_ref, e2_ref, ms2_ref, o_ref, in_ref, buf_ref, *, conv_u):
    n2h, _, C = u_ref.shape[1:]
    n2f = 2 * n2h
    L = n2h * HY_N1
    half = HY_N1

    for s in range(2):
        u = u_ref[s].reshape(L, C).astype(F32)
        if conv_u:
            u = _short_conv(u, cwu_ref, cbu_ref)
        in_ref[s] = u.reshape(n2h, HY_N1, C)

    def s1_body(j, carry):
        r0 = pl.multiple_of(j * HY_SUB, HY_SUB)
        sre = in_ref[0, :, pl.ds(r0, HY_SUB), :].reshape(n2h * HY_SUB, C)
        sim = in_ref[1, :, pl.ds(r0, HY_SUB), :].reshape(n2h * HY_SUB, C)
        slab = jnp.concatenate([sre, sim], axis=0).astype(BF16)
        out = jnp.dot(ms1_ref[...], slab, preferred_element_type=F32)
        buf_ref[:, pl.ds(r0, HY_SUB), :] = out[:n2f * HY_SUB].reshape(n2f, HY_SUB, C)
        buf_ref[:, pl.ds(half + r0, HY_SUB), :] = out[n2f * HY_SUB:].reshape(n2f, HY_SUB, C)
        return carry

    lax.fori_loop(0, HY_N1 // HY_SUB, s1_body, 0, unroll=HY_S_UNROLL)

    def e_body(k2, carry):
        blk = buf_ref[k2].astype(BF16)
        x = jnp.dot(e1_ref[k2], blk, preferred_element_type=F32)
        xr, xi = x[:half], x[half:]
        gr = g_ref[k2, :half, :]
        gi = g_ref[k2, half:, :]
        y = jnp.concatenate([xr * gr - xi * gi, xr * gi + xi * gr], axis=0).astype(BF16)
        buf_ref[k2] = jnp.dot(e2_ref[k2], y, preferred_element_type=F32)
        return carry

    lax.fori_loop(0, n2f, e_body, 0, unroll=min(n2f, HY_E_UNROLL))

    def s2_body(j, carry):
        r0 = pl.multiple_of(j * HY_SUB, HY_SUB)
        sre = buf_ref[:, pl.ds(r0, HY_SUB), :].reshape(n2f * HY_SUB, C)
        sim = buf_ref[:, pl.ds(half + r0, HY_SUB), :].reshape(n2f * HY_SUB, C)
        slab = jnp.concatenate([sre, sim], axis=0).astype(BF16)
        out = jnp.dot(ms2_ref[...], slab, preferred_element_type=F32)
        in_ref[0, :, pl.ds(r0, HY_SUB), :] = out[:n2h * HY_SUB].reshape(n2h, HY_SUB, C)
        in_ref[1, :, pl.ds(r0, HY_SUB), :] = out[n2h * HY_SUB:].reshape(n2h, HY_SUB, C)
        return carry

    lax.fori_loop(0, HY_N1 // HY_SUB, s2_body, 0, unroll=HY_S_UNROLL)

    for s in range(2):
        gate = _short_conv(gate_ref[s].reshape(L, C).astype(F32), cwg_ref, cbg_ref)
        o_ref[s] = (gate * in_ref[s].reshape(L, C)).reshape(n2h, HY_N1, C).astype(o_ref.dtype)


def _hyena_order(u4, u_blk, p4, gate_blk, cw, cb, g, mats, conv_u):
    B, n2h = u4.shape[0], u4.shape[1]
    n2f = 2 * n2h
    ncb = C_WIDTH // HY_CB
    ms1, e1, e2, ms2 = mats
    cin_blk = OFF_CIN // HY_CB
    one = pl.Buffered(1)

    def tok(col0):
        return pl.BlockSpec((2, n2h, HY_N1, HY_CB), lambda c, i: (i, 0, 0, col0 + c))

    kern = functools.partial(_hyena_kernel, conv_u=conv_u)
    return pl.pallas_call(
        kern,
        grid=(ncb, B // 2),
        in_specs=[tok(u_blk), tok(gate_blk),
                  pl.BlockSpec((3, HY_CB), lambda c, i: (0, c)),
                  pl.BlockSpec((1, HY_CB), lambda c, i: (0, c)),
                  pl.BlockSpec((3, HY_CB), lambda c, i: (0, (gate_blk - cin_blk) + c)),
                  pl.BlockSpec((1, HY_CB), lambda c, i: (0, (gate_blk - cin_blk) + c)),
                  pl.BlockSpec((n2f, 2 * HY_N1, HY_CB), lambda c, i: (0, 0, c), pipeline_mode=one),
                  pl.BlockSpec(ms1.shape, lambda c, i: (0, 0), pipeline_mode=one),
                  pl.BlockSpec(e1.shape, lambda c, i: (0, 0, 0), pipeline_mode=one),
                  pl.BlockSpec(e2.shape, lambda c, i: (0, 0, 0), pipeline_mode=one),
                  pl.BlockSpec(ms2.shape, lambda c, i: (0, 0), pipeline_mode=one)],
        out_specs=pl.BlockSpec((2, n2h, HY_N1, HY_CB), lambda c, i: (i, 0, 0, c)),
        out_shape=jax.ShapeDtypeStruct((B, n2h, HY_N1, C_WIDTH), BF16),
        scratch_shapes=[pltpu.VMEM((2, n2h, HY_N1, HY_CB), F32),
                        pltpu.VMEM((n2f, 2 * HY_N1, HY_CB), F32)],
        compiler_params=_cparams(("arbitrary", "arbitrary")),
        name="hyena_order",
    )(u4, p4, cw, cb.reshape(1, -1), cw, cb.reshape(1, -1), g, ms1, e1, e2, ms2)


def hyena_conv(p3, conv_w, conv_b, g0, g1):
    B, L, W = p3.shape
    n2h = L // HY_N1
    p4 = p3.reshape(B, n2h, HY_N1, W)
    mats = hyena_mats(L)
    cin_blk = OFF_CIN // HY_CB
    ncb = C_WIDTH // HY_CB
    cw, cb = conv_w.astype(F32), conv_b.astype(F32)
    z1 = _hyena_order(p4, cin_blk, p4, cin_blk + ncb, cw, cb, g0, mats, True)
    z2 = _hyena_order(z1, 0, p4, cin_blk + 2 * ncb, cw, cb, g1, mats, False)
    return z2.reshape(B, L, C_WIDTH)


def hyena_filters(L, w1, b1, w2, b2, w3, freq):
    t = jnp.linspace(0.0, 1.0, L, dtype=jnp.float32)[:, None]
    omega = 2.0 * math.pi * jnp.arange(L, dtype=jnp.float32)[:, None] / L
    f = jnp.linspace(1e-4, HY_BANDS - 1, HY_BANDS, dtype=jnp.float32)[None, :]
    z = jnp.concatenate([t, jnp.cos(f * omega), -jnp.sin(f * omega)], axis=-1).astype(w1.dtype)
    h = jnp.sin(freq * (z @ w1 + b1))
    h = jnp.sin(freq * (h @ w2 + b2))
    h = (h @ w3).astype(jnp.float32).reshape(L, HY_DIRS, HY_ORDER, C_WIDTH)
    deltas = jnp.abs(jnp.linspace(math.log(HY_TARGET) / HY_SLOW, math.log(HY_TARGET) / HY_FAST,
                                  C_WIDTH, dtype=jnp.float32))
    h = h * jnp.exp(-t[:, :, None, None] * deltas)
    return h / jnp.sqrt(jnp.sum(h * h, axis=0, keepdims=True) + EPS)


def _rope_partner(width):
    j = np.arange(width)
    return np.where((j % 32) < 16, j + 16, j - 16)


def prep_w_in(w):
    ak, av, bk, bv, aq, bq, cin, gates = jnp.split(w, SPLITS, axis=-1)
    aq, bq = aq * Q_SCALE, bq * Q_SCALE
    aqr = aq[:, _rope_partner(A_Q_W)]
    akr = ak[:, _rope_partner(A_KV_W)]
    pad = jnp.zeros((w.shape[0], P_COLS - OFF_AV - A_KV_W), w.dtype)
    return jnp.concatenate([gates, cin, aq, aqr, bq, bk, bv, ak, akr, av, pad], axis=-1).astype(BF16)


def kernel(x, c, ctx, c_ctx, w_mod, b_mod, g_mix_pre, g_mix_post, g_ffn_pre, g_ffn_post,
           w_in, sink_a, rpb_b, conv_c_w, conv_c_b, hy_w1, hy_b1, hy_w2, hy_b2, hy_w3,
           hy_freq, hy_bias, w_branch, w_out, ffn_w1, ffn_w3, ffn_w2,
           moe_router, moe_w1, moe_w3, moe_w2):
    B, L, D = x.shape
    Lc = ctx.shape[1]
    N, Nc = B * L, B * Lc
    x2 = x.reshape(N, D)
    ctx2 = ctx.reshape(Nc, D)
    cs = jnp.concatenate([c, c_ctx[None, :], jnp.zeros((7, D), F32)], axis=0)

    for l in range(DEPTH):
        last = l == DEPTH - 1
        mod = modulation(cs, w_mod[l], b_mod[l])
        mods = [mod[:B, k * D:(k + 1) * D].reshape(B, 1, D) for k in range(6)]
        modc = [mod[B:B + 1, k * D:(k + 1) * D].reshape(1, 1, D) for k in range(6)]
        sh_m, sc_m, gt_m, sh_f, sc_f, gt_f = mods
        shc_m, scc_m, gtc_m, shc_f, scc_f, gtc_f = modc

        w_all = prep_w_in(w_in[l])
        p = proj(x2, g_mix_pre[l], sc_m, sh_m, w_all, L)
        if last:
            pc = proj(ctx2, g_mix_pre[l], scc_m, shc_m, w_all[:, KV_OFF:], Nc)
            coff = KV_OFF
        else:
            pc = proj(ctx2, g_mix_pre[l], scc_m, shc_m, w_all, Nc)
            coff = 0

        p3 = p.reshape(B, L, -1)
        pc3 = pc.reshape(B, Lc, -1)
        qr, kr = rope_qk(p, L)
        ya = window_attention(qr.reshape(B, L, -1), kr.reshape(B, L, -1), p3, pc3, coff, sink_a[l])
        yb = neighbourhood_attention(p3, pc3, coff, rpb_b[l])
        filt = hyena_filters(L, hy_w1[l], hy_b1[l], hy_w2[l], hy_b2[l], hy_w3[l], hy_freq[l])
        g0, g1 = hyena_spectrum(filt, hy_bias[l])
        yc = hyena_conv(p3, conv_c_w[l], conv_c_b[l], g0, g1)
        wb = w_branch[l].astype(BF16)
        wo = w_out[l].astype(BF16)
        x2 = merge(ya.reshape(N, -1).astype(BF16), yb.reshape(N, -1).astype(BF16),
                   yc.reshape(N, -1).astype(BF16), p, wb, wo, x2, g_mix_post[l], gt_m, L)
        if not last:
            yac, ybc = ctx_attention(pc3, sink_a[l])
            filt_c = hyena_filters(Lc, hy_w1[l], hy_b1[l], hy_w2[l], hy_b2[l], hy_w3[l], hy_freq[l])
            gc0, gc1 = hyena_spectrum(filt_c, hy_bias[l])
            ycc = hyena_conv(pc3, conv_c_w[l], conv_c_b[l], gc0, gc1)
            ctx2 = merge(yac.reshape(Nc, -1).astype(BF16), ybc.reshape(Nc, -1).astype(BF16),
                         ycc.reshape(Nc, -1).astype(BF16), pc, wb, wo, ctx2, g_mix_post[l], gtc_m, Nc)

        j = l // 2
        if l % 2 == 0:
            w1, w3, w2 = ffn_w1[j].astype(BF16), ffn_w3[j].astype(BF16), ffn_w2[j].astype(BF16)
            x2 = ffn(x2, g_ffn_pre[l], sc_f, sh_f, w1, w3, w2, g_ffn_post[l], gt_f, L)
            if not last:
                ctx2 = ffn(ctx2, g_ffn_pre[l], scc_f, shc_f, w1, w3, w2, g_ffn_post[l], gtc_f, Nc)
        else:
            w1, w3, w2 = moe_w1[j].astype(BF16), moe_w3[j].astype(BF16), moe_w2[j].astype(BF16)
            x2 = moe(x2, g_ffn_pre[l], sc_f, sh_f, moe_router[j], w1, w3, w2, g_ffn_post[l], gt_f, L)
            if not last:
                ctx2 = moe(ctx2, g_ffn_pre[l], scc_f, shc_f, moe_router[j], w1, w3, w2,
                           g_ffn_post[l], gtc_f, Nc)
    return x2.reshape(B, L, D)
```

```python
import functools
import math

import jax
import jax.numpy as jnp
import numpy as np
from jax import lax
from jax.experimental import pallas as pl
from jax.experimental.pallas import tpu as pltpu

F32 = jnp.float32
BF16 = jnp.bfloat16

D_MODEL = 1024
DEPTH = 2
GRID_W = 64
HEAD_DIM = 64
A_HEADS = 8
A_KV_HEADS = 2
A_GROUP = A_HEADS // A_KV_HEADS
WINDOW = 128
A_BLOCK = 128
B_HEADS = 8
NA_ROWS = 8
NA_COLS = 16
C_WIDTH = 512
HY_ORDER = 2
HY_DIRS = 2
HY_BANDS = 16
HY_TARGET = 1e-2
HY_FAST = 0.3
HY_SLOW = 1.5
N_BRANCH = 3
N_EXPERTS = 8
TOP_K = 2
ROPE_BASE = 10000.0
EPS = 1e-6
NEG = -1e30

A_KV_W = A_KV_HEADS * HEAD_DIM
A_Q_W = A_HEADS * HEAD_DIM
B_W = B_HEADS * HEAD_DIM
KV_COLS = 2 * A_KV_W + 2 * B_W
SPLITS = (A_KV_W, 2 * A_KV_W, 2 * A_KV_W + B_W, KV_COLS,
          KV_COLS + A_Q_W, KV_COLS + A_Q_W + B_W,
          KV_COLS + A_Q_W + B_W + 3 * C_WIDTH)

OFF_GATES = 0
OFF_CIN = 3072
OFF_AQ = 4608
OFF_AQR = 5120
OFF_BQ = 5632
OFF_BK = 6144
OFF_BV = 6656
OFF_AK = 7168
OFF_AKR = 7296
OFF_AV = 7424
P_COLS = 7680
KV_OFF = OFF_BK
KV_W = P_COLS - KV_OFF

VMEM_LIMIT = 48 * 1024 * 1024
LOG2E = math.log2(math.e)
Q_SCALE = HEAD_DIM ** -0.5 * LOG2E


def _cparams(sem):
    return pltpu.CompilerParams(dimension_semantics=sem, vmem_limit_bytes=VMEM_LIMIT)


def _rms(y, g):
    return y * lax.rsqrt(jnp.mean(y * y, axis=-1, keepdims=True) + EPS) * g


def _norm_mod(x, g, sc, sh):
    return _rms(x, g) * (1.0 + sc) + sh


def _mod_kernel(c_ref, w_ref, b_ref, o_ref):
    c = c_ref[...]
    s = c * jax.nn.sigmoid(c)
    o_ref[...] = jnp.dot(s, w_ref[...], preferred_element_type=F32,
                         precision=lax.Precision.HIGHEST) + b_ref[...]


def modulation(cs, w, b, tn=512):
    m, d = cs.shape
    n = w.shape[1]
    return pl.pallas_call(
        _mod_kernel,
        grid=(n // tn,),
        in_specs=[pl.BlockSpec((m, d), lambda j: (0, 0)),
                  pl.BlockSpec((d, tn), lambda j: (0, j)),
                  pl.BlockSpec((1, tn), lambda j: (0, j))],
        out_specs=pl.BlockSpec((m, tn), lambda j: (0, j)),
        out_shape=jax.ShapeDtypeStruct((m, n), F32),
        compiler_params=_cparams(("parallel",)),
        name="modulation",
    )(cs, w, b.reshape(1, n))


def _proj_kernel(x_ref, g_ref, sc_ref, sh_ref, w_ref, o_ref, h_ref):
    @pl.when(pl.program_id(1) == 0)
    def _():
        h_ref[...] = _norm_mod(x_ref[...], g_ref[...], sc_ref[0], sh_ref[0]).astype(BF16)

    o_ref[...] = jnp.dot(h_ref[...], w_ref[...], preferred_element_type=F32).astype(o_ref.dtype)


def proj(x2, g, sc, sh, w, rows_per_mod, tm=1024, tn=3840):
    m, d = x2.shape
    n = w.shape[1]
    tm = min(tm, m)
    tn = min(tn, n)
    tpb = rows_per_mod // tm
    return pl.pallas_call(
        _proj_kernel,
        grid=(m // tm, n // tn),
        in_specs=[pl.BlockSpec((tm, d), lambda i, j: (i, 0)),
                  pl.BlockSpec((1, d), lambda i, j: (0, 0)),
                  pl.BlockSpec((1, 1, d), lambda i, j: (i // tpb, 0, 0)),
                  pl.BlockSpec((1, 1, d), lambda i, j: (i // tpb, 0, 0)),
                  pl.BlockSpec((d, tn), lambda i, j: (0, j))],
        out_specs=pl.BlockSpec((tm, tn), lambda i, j: (i, j)),
        out_shape=jax.ShapeDtypeStruct((m, n), BF16),
        scratch_shapes=[pltpu.VMEM((tm, d), BF16)],
        compiler_params=_cparams(("parallel", "arbitrary")),
        name="proj",
    )(x2, g.reshape(1, d), sc, sh, w)


def _merge_kernel(ya_ref, yb_ref, yc_ref, ga_ref, gb_ref, gc_ref, wb_ref, wo_ref,
                  x_ref, gp_ref, gt_ref, o_ref):
    def branch(y_ref, g_ref, k):
        t = jnp.dot(y_ref[...], wb_ref[k], preferred_element_type=F32)
        return jax.nn.sigmoid(g_ref[...].astype(F32)) * t

    m = branch(ya_ref, ga_ref, 0) + branch(yb_ref, gb_ref, 1) + branch(yc_ref, gc_ref, 2)
    y = jnp.dot(m.astype(BF16), wo_ref[...], preferred_element_type=F32)
    o_ref[...] = x_ref[...] + gt_ref[0] * _rms(y, gp_ref[...])


def merge(ya, yb, yc, p, wb, wo, x2, gpost, gt, rows_per_mod, tm=512):
    m, d = x2.shape
    bw = ya.shape[1]
    tm = min(tm, m)
    tpb = rows_per_mod // tm
    gblk = OFF_GATES // d
    return pl.pallas_call(
        _merge_kernel,
        grid=(m // tm,),
        in_specs=[pl.BlockSpec((tm, bw), lambda i: (i, 0)),
                  pl.BlockSpec((tm, bw), lambda i: (i, 0)),
                  pl.BlockSpec((tm, bw), lambda i: (i, 0)),
                  pl.BlockSpec((tm, d), lambda i: (i, gblk)),
                  pl.BlockSpec((tm, d), lambda i: (i, gblk + 1)),
                  pl.BlockSpec((tm, d), lambda i: (i, gblk + 2)),
                  pl.BlockSpec((N_BRANCH, bw, d), lambda i: (0, 0, 0)),
                  pl.BlockSpec((d, d), lambda i: (0, 0)),
                  pl.BlockSpec((tm, d), lambda i: (i, 0)),
                  pl.BlockSpec((1, d), lambda i: (0, 0)),
                  pl.BlockSpec((1, 1, d), lambda i: (i // tpb, 0, 0))],
        out_specs=pl.BlockSpec((tm, d), lambda i: (i, 0)),
        out_shape=jax.ShapeDtypeStruct((m, d), F32),
        compiler_params=_cparams(("parallel",)),
        name="merge",
    )(ya, yb, yc, p, p, p, wb, wo, x2, gpost.reshape(1, d), gt)


def _ffn_kernel(x_ref, g_ref, sc_ref, sh_ref, w1_ref, w3_ref, w2_ref, gp_ref, gt_ref,
                o_ref, h_ref, acc_ref):
    f = pl.program_id(1)

    @pl.when(f == 0)
    def _():
        h_ref[...] = _norm_mod(x_ref[...], g_ref[...], sc_ref[0], sh_ref[0]).astype(BF16)
        acc_ref[...] = jnp.zeros_like(acc_ref)

    h = h_ref[...]
    a = jnp.dot(h, w1_ref[...], preferred_element_type=F32)
    b = jnp.dot(h, w3_ref[...], preferred_element_type=F32)
    act = (a * jax.nn.sigmoid(a) * b).astype(BF16)
    acc_ref[...] += jnp.dot(act, w2_ref[...], preferred_element_type=F32)

    @pl.when(f == pl.num_programs(1) - 1)
    def _():
        o_ref[...] = x_ref[...] + gt_ref[0] * _rms(acc_ref[...], gp_ref[...])


def ffn(x2, g, sc, sh, w1, w3, w2, gpost, gt, rows_per_mod, tm=512, tf=2816):
    m, d = x2.shape
    ff = w1.shape[1]
    tm = min(tm, m)
    tpb = rows_per_mod // tm
    return pl.pallas_call(
        _ffn_kernel,
        grid=(m // tm, ff // tf),
        in_specs=[pl.BlockSpec((tm, d), lambda i, f: (i, 0)),
                  pl.BlockSpec((1, d), lambda i, f: (0, 0)),
                  pl.BlockSpec((1, 1, d), lambda i, f: (i // tpb, 0, 0)),
                  pl.BlockSpec((1, 1, d), lambda i, f: (i // tpb, 0, 0)),
                  pl.BlockSpec((d, tf), lambda i, f: (0, f)),
                  pl.BlockSpec((d, tf), lambda i, f: (0, f)),
                  pl.BlockSpec((tf, d), lambda i, f: (f, 0)),
                  pl.BlockSpec((1, d), lambda i, f: (0, 0)),
                  pl.BlockSpec((1, 1, d), lambda i, f: (i // tpb, 0, 0))],
        out_specs=pl.BlockSpec((tm, d), lambda i, f: (i, 0)),
        out_shape=jax.ShapeDtypeStruct((m, d), F32),
        scratch_shapes=[pltpu.VMEM((tm, d), BF16), pltpu.VMEM((tm, d), F32)],
        compiler_params=_cparams(("parallel", "arbitrary")),
        name="ffn",
    )(x2, g.reshape(1, d), sc, sh, w1, w3, w2, gpost.reshape(1, d), gt)


def _router_kernel(x_ref, g_ref, sc_ref, sh_ref, rh_ref, rl_ref, h_ref, lg_ref):
    h = _norm_mod(x_ref[...], g_ref[...], sc_ref[0], sh_ref[0])
    hb = h.astype(BF16)
    h_ref[...] = hb
    hl = (h - hb.astype(F32)).astype(BF16)
    rh = rh_ref[...]
    lg_ref[...] = (jnp.dot(hb, rh, preferred_element_type=F32)
                   + jnp.dot(hl, rh, preferred_element_type=F32)
                   + jnp.dot(hb, rl_ref[...], preferred_element_type=F32))


def router(x2, g, sc, sh, r, rows_per_mod, tm=1024):
    m, d = x2.shape
    e = r.shape[1]
    tm = min(tm, m)
    tpb = rows_per_mod // tm
    rh = r.astype(BF16)
    rl = (r - rh.astype(F32)).astype(BF16)
    return pl.pallas_call(
        _router_kernel,
        grid=(m // tm,),
        in_specs=[pl.BlockSpec((tm, d), lambda i: (i, 0)),
                  pl.BlockSpec((1, d), lambda i: (0, 0)),
                  pl.BlockSpec((1, 1, d), lambda i: (i // tpb, 0, 0)),
                  pl.BlockSpec((1, 1, d), lambda i: (i // tpb, 0, 0)),
                  pl.BlockSpec((d, e), lambda i: (0, 0)),
                  pl.BlockSpec((d, e), lambda i: (0, 0))],
        out_specs=[pl.BlockSpec((tm, d), lambda i: (i, 0)),
                   pl.BlockSpec((tm, e), lambda i: (i, 0))],
        out_shape=[jax.ShapeDtypeStruct((m, d), BF16), jax.ShapeDtypeStruct((m, e), F32)],
        compiler_params=_cparams(("parallel",)),
        name="router",
    )(x2, g.reshape(1, d), sc, sh, rh, rl)


def _expert_kernel(te_ref, hs_ref, w1_ref, w3_ref, w2_ref, o_ref, acc_ref):
    i = pl.program_id(0)
    f = pl.program_id(1)
    nf = pl.num_programs(1)
    valid = te_ref[i] >= 0

    @pl.when(f == 0)
    def _():
        acc_ref[...] = jnp.zeros_like(acc_ref)

    @pl.when(valid)
    def _():
        h = hs_ref[...]
        a = jnp.dot(h, w1_ref[0], preferred_element_type=F32)
        b = jnp.dot(h, w3_ref[0], preferred_element_type=F32)
        act = (a * jax.nn.sigmoid(a) * b).astype(BF16)
        acc_ref[...] += jnp.dot(act, w2_ref[0], preferred_element_type=F32)

    @pl.when(f == nf - 1)
    def _():
        o_ref[...] = acc_ref[...].astype(o_ref.dtype)


def experts(tile_expert, hs, w1, w3, w2, tm=512, tf=1792):
    mp, d = hs.shape
    ff = w1.shape[2]

    def wmap(i, f, te):
        return (jnp.maximum(te[i], 0), 0, f)

    def w2map(i, f, te):
        return (jnp.maximum(te[i], 0), f, 0)

    return pl.pallas_call(
        _expert_kernel,
        grid_spec=pltpu.PrefetchScalarGridSpec(
            num_scalar_prefetch=1,
            grid=(mp // tm, ff // tf),
            in_specs=[pl.BlockSpec((tm, d), lambda i, f, te: (i, 0)),
                      pl.BlockSpec((1, d, tf), wmap),
                      pl.BlockSpec((1, d, tf), wmap),
                      pl.BlockSpec((1, tf, d), w2map)],
            out_specs=pl.BlockSpec((tm, d), lambda i, f, te: (i, 0)),
            scratch_shapes=[pltpu.VMEM((tm, d), F32)]),
        out_shape=jax.ShapeDtypeStruct((mp, d), BF16),
        compiler_params=_cparams(("parallel", "arbitrary")),
        name="experts",
    )(tile_expert, hs, w1, w3, w2)


def _combine_kernel(y1_ref, y2_ref, cw_ref, x_ref, gp_ref, gt_ref, o_ref):
    cw = cw_ref[...]
    y = cw[:, 0:1] * y1_ref[...].astype(F32) + cw[:, 1:2] * y2_ref[...].astype(F32)
    o_ref[...] = x_ref[...] + gt_ref[0] * _rms(y, gp_ref[...])


def combine(y1, y2, cw, x2, gpost, gt, rows_per_mod, tm=1024):
    m, d = x2.shape
    tm = min(tm, m)
    tpb = rows_per_mod // tm
    return pl.pallas_call(
        _combine_kernel,
        grid=(m // tm,),
        in_specs=[pl.BlockSpec((tm, d), lambda i: (i, 0)),
                  pl.BlockSpec((tm, d), lambda i: (i, 0)),
                  pl.BlockSpec((tm, TOP_K), lambda i: (i, 0)),
                  pl.BlockSpec((tm, d), lambda i: (i, 0)),
                  pl.BlockSpec((1, d), lambda i: (0, 0)),
                  pl.BlockSpec((1, 1, d), lambda i: (i // tpb, 0, 0))],
        out_specs=pl.BlockSpec((tm, d), lambda i: (i, 0)),
        out_shape=jax.ShapeDtypeStruct((m, d), F32),
        compiler_params=_cparams(("parallel",)),
        name="combine",
    )(y1, y2, cw, x2, gpost.reshape(1, d), gt)


def moe(x2, g, sc, sh, r, w1, w3, w2, gpost, gt, rows_per_mod, tm=512):
    m, d = x2.shape
    e = r.shape[1]
    h, logits = router(x2, g, sc, sh, r, rows_per_mod)
    top_v, top_i = lax.top_k(logits, TOP_K)
    top_w = jax.nn.softmax(top_v, axis=-1)
    flat_e = top_i.reshape(-1)
    onehot = (flat_e[:, None] == jnp.arange(e)[None, :]).astype(jnp.int32)
    rank = jnp.cumsum(onehot, axis=0) - onehot
    counts = jnp.sum(onehot, axis=0)
    padded = ((counts + tm - 1) // tm) * tm
    pad_end = jnp.cumsum(padded)
    pad_start = pad_end - padded
    slot = pad_start[flat_e] + jnp.sum(rank * onehot, axis=1)
    mp = m * TOP_K + e * tm
    n_tiles = mp // tm
    tile_start = jnp.arange(n_tiles, dtype=jnp.int32) * tm
    tile_e = jnp.sum((tile_start[:, None] >= pad_end[None, :]).astype(jnp.int32), axis=1)
    te = jnp.minimum(tile_e, e - 1)
    tile_e = jnp.where(tile_start < pad_end[-1], tile_e, -1).astype(jnp.int32)
    order = jnp.argsort(flat_e, stable=True).astype(jnp.int32)
    first = jnp.cumsum(counts) - counts
    in_group = (tile_start - pad_start[te])[:, None] + jnp.arange(tm, dtype=jnp.int32)[None, :]
    live = (in_group < counts[te][:, None]) & (tile_e >= 0)[:, None]
    src = jnp.where(live, first[te][:, None] + in_group, 0)
    slot_token = order[src.reshape(-1)] // TOP_K
    hs = jnp.take(h, slot_token, axis=0)
    ys = experts(tile_e, hs, w1, w3, w2, tm=tm)
    slot2 = slot.reshape(m, TOP_K)
    y1 = jnp.take(ys, slot2[:, 0], axis=0)
    y2 = jnp.take(ys, slot2[:, 1], axis=0)
    return combine(y1, y2, top_w, x2, gpost, gt, rows_per_mod)


def rope_tables(L, width):
    t = np.arange(L)
    j = np.arange(width) % HEAD_DIM
    pos = np.where(j[None, :] < HEAD_DIM // 2, (t // GRID_W)[:, None], (t % GRID_W)[:, None])
    quarter = HEAD_DIM // 4
    inv = ROPE_BASE ** (-(j % quarter).astype(np.float32) / quarter)
    ang = jnp.asarray(pos.astype(np.float32)) * jnp.asarray(inv.astype(np.float32))[None, :]
    sign = np.where((j % (2 * quarter)) < quarter, -1.0, 1.0).astype(np.float32)
    return jnp.cos(ang), jnp.sin(ang) * jnp.asarray(sign)[None, :]


def _rope_kernel(aq_ref, aqr_ref, ak_ref, akr_ref, cq_ref, sq_ref, ck_ref, sk_ref, q_ref, k_ref):
    q = aq_ref[...].astype(F32) * cq_ref[...] + aqr_ref[...].astype(F32) * sq_ref[...]
    q_ref[...] = q.astype(BF16)
    k = ak_ref[...].astype(F32) * ck_ref[...] + akr_ref[...].astype(F32) * sk_ref[...]
    k_ref[...] = k.astype(BF16)


def rope_qk(p, L, tm=1024):
    m = p.shape[0]
    tm = min(tm, L)
    nt = L // tm
    cq, sq = rope_tables(L, A_Q_W)
    ck, sk = rope_tables(L, A_KV_W)
    return pl.pallas_call(
        _rope_kernel,
        grid=(m // tm,),
        in_specs=[pl.BlockSpec((tm, A_Q_W), lambda i: (i, OFF_AQ // A_Q_W)),
                  pl.BlockSpec((tm, A_Q_W), lambda i: (i, OFF_AQR // A_Q_W)),
                  pl.BlockSpec((tm, A_KV_W), lambda i: (i, OFF_AK // A_KV_W)),
                  pl.BlockSpec((tm, A_KV_W), lambda i: (i, OFF_AKR // A_KV_W)),
                  pl.BlockSpec((tm, A_Q_W), lambda i: (i % nt, 0)),
                  pl.BlockSpec((tm, A_Q_W), lambda i: (i % nt, 0)),
                  pl.BlockSpec((tm, A_KV_W), lambda i: (i % nt, 0)),
                  pl.BlockSpec((tm, A_KV_W), lambda i: (i % nt, 0))],
        out_specs=[pl.BlockSpec((tm, A_Q_W), lambda i: (i, 0)),
                   pl.BlockSpec((tm, A_KV_W), lambda i: (i, 0))],
        out_shape=[jax.ShapeDtypeStruct((m, A_Q_W), BF16), jax.ShapeDtypeStruct((m, A_KV_W), BF16)],
        compiler_params=_cparams(("parallel",)),
        name="rope_qk",
    )(p, p, p, p, cq, sq, ck, sk)


def _dot_nt(a, b):
    return lax.dot_general(a, b, (((1,), (1,)), ((), ())), preferred_element_type=F32)


A_SPAN = A_BLOCK + 2 * WINDOW


def _wa_kernel(sink_ref, q_ref, k_ref, v_ref, kc_ref, vc_ref, o_ref):
    i = pl.program_id(1)
    L = k_ref.shape[1]
    start = pl.multiple_of(jnp.clip(i * A_BLOCK - WINDOW, 0, L - A_SPAN), A_BLOCK)
    qpos = i * A_BLOCK + lax.broadcasted_iota(jnp.int32, (A_BLOCK, A_SPAN), 0)
    kpos = start + lax.broadcasted_iota(jnp.int32, (A_BLOCK, A_SPAN), 1)
    valid = jnp.abs(qpos - kpos) <= WINDOW
    q = q_ref[0]
    scores = []
    for h in range(A_HEADS):
        ks = slice((h // A_GROUP) * HEAD_DIM, (h // A_GROUP + 1) * HEAD_DIM)
        qh = q[:, h * HEAD_DIM:(h + 1) * HEAD_DIM]
        s_loc = jnp.where(valid, _dot_nt(qh, k_ref[0, pl.ds(start, A_SPAN), ks]), NEG)
        scores.append(jnp.concatenate([s_loc, _dot_nt(qh, kc_ref[0, :, ks])], axis=1))
    probs, sinks = [], []
    for h in range(A_HEADS):
        sink = sink_ref[h]
        m = jnp.maximum(jnp.max(scores[h], axis=-1, keepdims=True), sink)
        probs.append(jnp.exp2((scores[h] - m).astype(BF16)))
        sinks.append(jnp.exp2(sink - m))
    ones = jnp.ones((A_SPAN + kc_ref.shape[1], HEAD_DIM), BF16)
    for g in range(A_KV_HEADS):
        ks = slice(g * HEAD_DIM, (g + 1) * HEAD_DIM)
        v_ext = jnp.concatenate(
            [jnp.concatenate([v_ref[0, pl.ds(start, A_SPAN), ks], vc_ref[0, :, ks]], axis=0), ones], axis=1)
        for h in range(g * A_GROUP, (g + 1) * A_GROUP):
            oe = jnp.dot(probs[h], v_ext, preferred_element_type=F32)
            den = pltpu.roll(oe, HEAD_DIM, axis=1) + sinks[h]
            o_ref[0, :, h * HEAD_DIM:(h + 1) * HEAD_DIM] = (oe / den)[:, :HEAD_DIM].astype(BF16)


def window_attention(qr, kr, p3, pc3, coff, sink):
    B, L, _ = qr.shape
    Lc = pc3.shape[1]
    return pl.pallas_call(
        _wa_kernel,
        grid=(B, L // A_BLOCK),
        in_specs=[pl.BlockSpec(memory_space=pltpu.SMEM),
                  pl.BlockSpec((1, A_BLOCK, A_Q_W), lambda b, i: (b, i, 0)),
                  pl.BlockSpec((1, L, A_KV_W), lambda b, i: (b, 0, 0)),
                  pl.BlockSpec((1, L, A_KV_W), lambda b, i: (b, 0, OFF_AV // A_KV_W)),
                  pl.BlockSpec((1, Lc, A_KV_W), lambda b, i: (b, 0, (OFF_AK - coff) // A_KV_W)),
                  pl.BlockSpec((1, Lc, A_KV_W), lambda b, i: (b, 0, (OFF_AV - coff) // A_KV_W))],
        out_specs=pl.BlockSpec((1, A_BLOCK, A_Q_W), lambda b, i: (b, i, 0)),
        out_shape=jax.ShapeDtypeStruct((B, L, A_Q_W), BF16),
        compiler_params=_cparams(("parallel", "arbitrary")),
        name="window_attention",
    )(sink.astype(F32) * LOG2E, qr, kr, p3, pc3, pc3)


NA_QROWS = 4
NA_KROWS = NA_ROWS + NA_QROWS
NA_TQ = NA_QROWS * GRID_W
NA_TK = NA_KROWS * GRID_W


def na_bias_table(rpb, rows):
    col = np.arange(GRID_W)
    col_start = np.clip(col - NA_COLS // 2, 0, GRID_W - NA_COLS)
    col_ok = (col[None, :] >= col_start[:, None]) & (col[None, :] < col_start[:, None] + NA_COLS)
    dc = np.clip(col[None, :] - col[:, None] + NA_COLS - 1, 0, 2 * NA_COLS - 2)
    a = np.arange(NA_QROWS)
    j = np.arange(NA_KROWS)
    dr_all, ok_all = [], []
    for r0, u in ((0, 0), (NA_QROWS, 0), (rows - NA_QROWS, rows - NA_KROWS)):
        r = r0 + a
        rs = np.clip(r - NA_ROWS // 2, 0, rows - NA_ROWS)
        kr = u + j
        ok = (kr[None, :] >= rs[:, None]) & (kr[None, :] < rs[:, None] + NA_ROWS)
        dr = np.clip(kr[None, :] - r[:, None] + NA_ROWS - 1, 0, 2 * NA_ROWS - 2)
        dr_all.append(dr)
        ok_all.append(ok)
    n_dr, n_dc = 2 * NA_ROWS - 1, 2 * NA_COLS - 1
    H = rpb.shape[0]
    onehot = (dc.reshape(-1)[None, :] == np.arange(n_dc)[:, None]).astype(np.float32)
    tiles = jnp.dot(rpb.astype(F32).reshape(H * n_dr, n_dc), jnp.asarray(onehot),
                    precision=lax.Precision.HIGHEST).reshape(H, n_dr, GRID_W, GRID_W)
    tiles = jnp.where(jnp.asarray(col_ok)[None, None], tiles * LOG2E, NEG)
    tiles = jnp.concatenate([tiles, jnp.full((H, 1, GRID_W, GRID_W), NEG, F32)], axis=1)
    sel = np.where(np.stack(ok_all), np.stack(dr_all), n_dr)
    out = tiles[:, sel.reshape(-1)].reshape(H, 3, NA_QROWS, NA_KROWS, GRID_W, GRID_W)
    return out.transpose(1, 0, 2, 4, 3, 5).reshape(3, H, NA_TQ, NA_TK)


def _na_kernel(q_ref, k_ref, v_ref, kc_ref, vc_ref, bias_ref, o_ref):
    blk = pl.program_id(1)
    rows = k_ref.shape[1] // GRID_W
    u = jnp.clip(blk * NA_QROWS - NA_ROWS // 2, 0, rows - NA_KROWS)
    start = pl.multiple_of(u * GRID_W, GRID_W)
    q = q_ref[0]
    kt = jnp.concatenate([k_ref[0, pl.ds(start, NA_TK), :], kc_ref[0]], axis=0)
    vt = jnp.concatenate([v_ref[0, pl.ds(start, NA_TK), :], vc_ref[0]], axis=0)
    pair = 2 * HEAD_DIM
    lane = lax.broadcasted_iota(jnp.int32, (NA_TQ, pair), 1)
    ones = jnp.ones((kt.shape[0], pair), BF16)
    zero = jnp.zeros((), BF16)
    scores = []
    for h in range(B_HEADS):
        ps = slice((h // 2) * pair, (h // 2 + 1) * pair)
        qh = jnp.where((lane < HEAD_DIM) if h % 2 == 0 else (lane >= HEAD_DIM), q[:, ps], zero)
        s = _dot_nt(qh, kt[:, ps])
        scores.append(jnp.concatenate([s[:, :NA_TK] + bias_ref[0, h], s[:, NA_TK:]], axis=1))
    probs = [jnp.exp2((s - jnp.max(s, axis=-1, keepdims=True)).astype(BF16)) for s in scores]
    for i in range(B_HEADS // 2):
        ps = slice(i * pair, (i + 1) * pair)
        v_ext = jnp.concatenate([vt[:, ps], ones], axis=1)
        oe = [jnp.dot(probs[2 * i + t], v_ext, preferred_element_type=F32) for t in range(2)]
        o = [x[:, :pair] / x[:, pair:] for x in oe]
        o_ref[0, :, ps] = jnp.where(lane < HEAD_DIM, o[0], o[1]).astype(BF16)


def neighbourhood_attention(p3, pc3, coff, rpb):
    B, L, _ = p3.shape
    Lc = pc3.shape[1]
    nblk = L // NA_TQ
    bias = na_bias_table(rpb, L // GRID_W)

    def bias_map(b, i):
        return (jnp.where(i == 0, 0, jnp.where(i == nblk - 1, 2, 1)), 0, 0, 0)

    return pl.pallas_call(
        _na_kernel,
        grid=(B, nblk),
        in_specs=[pl.BlockSpec((1, NA_TQ, B_W), lambda b, i: (b, i, OFF_BQ // B_W)),
                  pl.BlockSpec((1, L, B_W), lambda b, i: (b, 0, OFF_BK // B_W)),
                  pl.BlockSpec((1, L, B_W), lambda b, i: (b, 0, OFF_BV // B_W)),
                  pl.BlockSpec((1, Lc, B_W), lambda b, i: (b, 0, (OFF_BK - coff) // B_W)),
                  pl.BlockSpec((1, Lc, B_W), lambda b, i: (b, 0, (OFF_BV - coff) // B_W)),
                  pl.BlockSpec((1, B_HEADS, NA_TQ, NA_TK), bias_map)],
        out_specs=pl.BlockSpec((1, NA_TQ, B_W), lambda b, i: (b, i, 0)),
        out_shape=jax.ShapeDtypeStruct((B, L, B_W), BF16),
        compiler_params=_cparams(("parallel", "arbitrary")),
        name="neighbourhood_attention",
    )(p3, p3, p3, pc3, pc3, bias)


def _ctx_attn_kernel(sink_ref, aq_ref, ak_ref, av_ref, bq_ref, bk_ref, bv_ref, oa_ref, ob_ref):
    aq = aq_ref[0]
    bq = bq_ref[0]
    for h in range(A_HEADS):
        hs = slice(h * HEAD_DIM, (h + 1) * HEAD_DIM)
        g = h // A_GROUP
        ks = slice(g * HEAD_DIM, (g + 1) * HEAD_DIM)
        s = _dot_nt(aq[:, hs], ak_ref[0, :, ks])
        sink = sink_ref[h]
        m = jnp.maximum(jnp.max(s, axis=-1, keepdims=True), sink)
        p = jnp.exp2(s - m)
        den = jnp.sum(p, axis=-1, keepdims=True) + jnp.exp2(sink - m)
        o = jnp.dot(p.astype(BF16), av_ref[0, :, ks], preferred_element_type=F32)
        oa_ref[0, :, hs] = (o / den).astype(BF16)
    for h in range(B_HEADS):
        hs = slice(h * HEAD_DIM, (h + 1) * HEAD_DIM)
        s = _dot_nt(bq[:, hs], bk_ref[0, :, hs])
        m = jnp.max(s, axis=-1, keepdims=True)
        p = jnp.exp2(s - m)
        den = jnp.sum(p, axis=-1, keepdims=True)
        o = jnp.dot(p.astype(BF16), bv_ref[0, :, hs], preferred_element_type=F32)
        ob_ref[0, :, hs] = (o / den).astype(BF16)


def ctx_attention(pc3, sink):
    B, Lc, _ = pc3.shape
    return pl.pallas_call(
        _ctx_attn_kernel,
        grid=(B,),
        in_specs=[pl.BlockSpec(memory_space=pltpu.SMEM),
                  pl.BlockSpec((1, Lc, A_Q_W), lambda b: (b, 0, OFF_AQ // A_Q_W)),
                  pl.BlockSpec((1, Lc, A_KV_W), lambda b: (b, 0, OFF_AK // A_KV_W)),
                  pl.BlockSpec((1, Lc, A_KV_W), lambda b: (b, 0, OFF_AV // A_KV_W)),
                  pl.BlockSpec((1, Lc, B_W), lambda b: (b, 0, OFF_BQ // B_W)),
                  pl.BlockSpec((1, Lc, B_W), lambda b: (b, 0, OFF_BK // B_W)),
                  pl.BlockSpec((1, Lc, B_W), lambda b: (b, 0, OFF_BV // B_W))],
        out_specs=[pl.BlockSpec((1, Lc, A_Q_W), lambda b: (b, 0, 0)),
                   pl.BlockSpec((1, Lc, B_W), lambda b: (b, 0, 0))],
        out_shape=[jax.ShapeDtypeStruct((B, Lc, A_Q_W), BF16), jax.ShapeDtypeStruct((B, Lc, B_W), BF16)],
        compiler_params=_cparams(("parallel",)),
        name="ctx_attention",
    )(sink.astype(F32) * LOG2E, pc3, pc3, pc3, pc3, pc3, pc3)


HY_N1 = 128
HY_SUB = 8
HY_CB = 256
HY_S_UNROLL = 16
HY_E_UNROLL = 32


def hyena_mats(L):
    nf = 2 * L
    n2f = nf // HY_N1
    n2h = n2f // 2
    eye = np.eye(HY_SUB)

    def real_block(m):
        return np.block([[m.real, -m.imag], [m.imag, m.real]])

    f = np.exp(-2j * np.pi * np.outer(np.arange(n2f), np.arange(n2h)) / n2f)
    ms1 = real_block(np.kron(f, eye))
    fi = np.exp(2j * np.pi * np.outer(np.arange(n2h), np.arange(n2f)) / n2f) / nf
    ms2 = real_block(np.kron(fi, eye))
    k1 = np.arange(HY_N1)
    n1 = np.arange(HY_N1)
    e1, e2 = [], []
    for k2 in range(n2f):
        e = np.exp(-2j * np.pi * np.outer(n2f * k1 + k2, n1) / nf)
        e1.append(real_block(e))
        e2.append(real_block(np.conj(e).T))
    to = lambda m: jnp.asarray(np.asarray(m, np.float32), dtype=BF16)
    return to(ms1), to(np.stack(e1)), to(np.stack(e2)), to(ms2)


def _short_conv(u, w_ref, b_ref):
    L = u.shape[0]
    row = lax.broadcasted_iota(jnp.int32, (HY_SUB, u.shape[1]), 0)
    prev = pltpu.roll(u, 1, axis=0)
    prev = jnp.concatenate([jnp.where(row == 0, 0.0, prev[:HY_SUB]), prev[HY_SUB:]], axis=0)
    nxt = pltpu.roll(u, L - 1, axis=0)
    nxt = jnp.concatenate([nxt[:L - HY_SUB], jnp.where(row == HY_SUB - 1, 0.0, nxt[L - HY_SUB:])], axis=0)
    return prev * w_ref[0:1, :] + u * w_ref[1:2, :] + nxt * w_ref[2:3, :] + b_ref[...]


def _hyena_kernel(u_ref, gate_ref, cwu_ref, cbu_ref, cwg_ref, cbg_ref,
                  g_ref, ms1_ref, e1_ref, e2_ref, ms2_ref, o_ref, in_ref, buf_ref, *, conv_u):
    n2h, _, C = u_ref.shape[1:]
    n2f = 2 * n2h
    L = n2h * HY_N1
    half = HY_N1

    for s in range(2):
        u = u_ref[s].reshape(L, C).astype(F32)
        if conv_u:
            u = _short_conv(u, cwu_ref, cbu_ref)
        in_ref[s] = u.reshape(n2h, HY_N1, C)

    def s1_body(j, carry):
        r0 = pl.multiple_of(j * HY_SUB, HY_SUB)
        sre = in_ref[0, :, pl.ds(r0, HY_SUB), :].reshape(n2h * HY_SUB, C)
        sim = in_ref[1, :, pl.ds(r0, HY_SUB), :].reshape(n2h * HY_SUB, C)
        slab = jnp.concatenate([sre, sim], axis=0).astype(BF16)
        out = jnp.dot(ms1_ref[...], slab, preferred_element_type=F32)
        buf_ref[:, pl.ds(r0, HY_SUB), :] = out[:n2f * HY_SUB].reshape(n2f, HY_SUB, C)
        buf_ref[:, pl.ds(half + r0, HY_SUB), :] = out[n2f * HY_SUB:].reshape(n2f, HY_SUB, C)
        return carry

    lax.fori_loop(0, HY_N1 // HY_SUB, s1_body, 0, unroll=HY_S_UNROLL)

    def e_body(k2, carry):
        blk = buf_ref[k2].astype(BF16)
        x = jnp.dot(e1_ref[k2], blk, preferred_element_type=F32)
        xr, xi = x[:half], x[half:]
        gr = g_ref[k2, :half, :]
        gi = g_ref[k2, half:, :]
        y = jnp.concatenate([xr * gr - xi * gi, xr * gi + xi * gr], axis=0).astype(BF16)
        buf_ref[k2] = jnp.dot(e2_ref[k2], y, preferred_element_type=F32)
        return carry

    lax.fori_loop(0, n2f, e_body, 0, unroll=min(n2f, HY_E_UNROLL))

    def s2_body(j, carry):
        r0 = pl.multiple_of(j * HY_SUB, HY_SUB)
        sre = buf_ref[:, pl.ds(r0, HY_SUB), :].reshape(n2f * HY_SUB, C)
        sim = buf_ref[:, pl.ds(half + r0, HY_SUB), :].reshape(n2f * HY_SUB, C)
        slab = jnp.concatenate([sre, sim], axis=0).astype(BF16)
        out = jnp.dot(ms2_ref[...], slab, preferred_element_type=F32)
        in_ref[0, :, pl.ds(r0, HY_SUB), :] = out[:n2h * HY_SUB].reshape(n2h, HY_SUB, C)
        in_ref[1, :, pl.ds(r0, HY_SUB), :] = out[n2h * HY_SUB:].reshape(n2h, HY_SUB, C)
        return carry

    lax.fori_loop(0, HY_N1 // HY_SUB, s2_body, 0, unroll=HY_S_UNROLL)

    for s in range(2):
        gate = _short_conv(gate_ref[s].reshape(L, C).astype(F32), cwg_ref, cbg_ref)
        o_ref[s] = (gate * in_ref[s].reshape(L, C)).reshape(n2h, HY_N1, C).astype(o_ref.dtype)


def _hyena_order(u4, u_blk, p4, gate_blk, cw, cb, g, mats, conv_u):
    B, n2h = u4.shape[0], u4.shape[1]
    n2f = 2 * n2h
    ncb = C_WIDTH // HY_CB
    ms1, e1, e2, ms2 = mats
    cin_blk = OFF_CIN // HY_CB
    one = pl.Buffered(1)

    def tok(col0):
        return pl.BlockSpec((2, n2h, HY_N1, HY_CB), lambda c, i: (i, 0, 0, col0 + c))

    kern = functools.partial(_hyena_kernel, conv_u=conv_u)
    return pl.pallas_call(
        kern,
        grid=(ncb, B // 2),
        in_specs=[tok(u_blk), tok(gate_blk),
                  pl.BlockSpec((3, HY_CB), lambda c, i: (0, c)),
                  pl.BlockSpec((1, HY_CB), lambda c, i: (0, c)),
                  pl.BlockSpec((3, HY_CB), lambda c, i: (0, (gate_blk - cin_blk) + c)),
                  pl.BlockSpec((1, HY_CB), lambda c, i: (0, (gate_blk - cin_blk) + c)),
                  pl.BlockSpec((n2f, 2 * HY_N1, HY_CB), lambda c, i: (0, 0, c), pipeline_mode=one),
                  pl.BlockSpec(ms1.shape, lambda c, i: (0, 0), pipeline_mode=one),
                  pl.BlockSpec(e1.shape, lambda c, i: (0, 0, 0), pipeline_mode=one),
                  pl.BlockSpec(e2.shape, lambda c, i: (0, 0, 0), pipeline_mode=one),
                  pl.BlockSpec(ms2.shape, lambda c, i: (0, 0), pipeline_mode=one)],
        out_specs=pl.BlockSpec((2, n2h, HY_N1, HY_CB), lambda c, i: (i, 0, 0, c)),
        out_shape=jax.ShapeDtypeStruct((B, n2h, HY_N1, C_WIDTH), BF16),
        scratch_shapes=[pltpu.VMEM((2, n2h, HY_N1, HY_CB), F32),
                        pltpu.VMEM((n2f, 2 * HY_N1, HY_CB), F32)],
        compiler_params=_cparams(("arbitrary", "arbitrary")),
        name="hyena_order",
    )(u4, p4, cw, cb.reshape(1, -1), cw, cb.reshape(1, -1), g, ms1, e1, e2, ms2)


def hyena_conv(p3, conv_w, conv_b, g0, g1):
    B, L, W = p3.shape
    n2h = L // HY_N1
    p4 = p3.reshape(B, n2h, HY_N1, W)
    mats = hyena_mats(L)
    cin_blk = OFF_CIN // HY_CB
    ncb = C_WIDTH // HY_CB
    cw, cb = conv_w.astype(F32), conv_b.astype(F32)
    z1 = _hyena_order(p4, cin_blk, p4, cin_blk + ncb, cw, cb, g0, mats, True)
    z2 = _hyena_order(z1, 0, p4, cin_blk + 2 * ncb, cw, cb, g1, mats, False)
    return z2.reshape(B, L, C_WIDTH)


HY_HIDDEN = 64
HY_EMB_PAD = 128
HIGHEST = lax.Precision.HIGHEST


def hyena_features(L):
    t = np.linspace(0.0, 1.0, L, dtype=np.float32)[:, None]
    omega = (2.0 * math.pi * np.arange(L, dtype=np.float32)[:, None] / L).astype(np.float32)
    f = np.linspace(1e-4, HY_BANDS - 1, HY_BANDS, dtype=np.float32)[None, :]
    z = np.concatenate([t, np.cos(f * omega), -np.sin(f * omega)], axis=-1).astype(np.float32)
    zp = np.zeros((L, HY_EMB_PAD), np.float32)
    zp[:, :z.shape[1]] = z
    deltas = np.abs(np.linspace(math.log(HY_TARGET) / HY_SLOW, math.log(HY_TARGET) / HY_FAST,
                                C_WIDTH, dtype=np.float32))
    return jnp.asarray(zp), jnp.asarray(t), jnp.asarray(deltas)


def _filter_kernel(z_ref, t_ref, w1_ref, b1_ref, w2_ref, b2_ref, w3_ref, fr_ref, dl_ref, o_ref):
    fr = fr_ref[...]
    h = jnp.sin(fr * (jnp.dot(z_ref[...], w1_ref[...], preferred_element_type=F32, precision=HIGHEST)
                      + b1_ref[...]))
    h = jnp.sin(fr * (jnp.dot(h, w2_ref[...], preferred_element_type=F32, precision=HIGHEST) + b2_ref[...]))
    h = jnp.dot(h, w3_ref[...], preferred_element_type=F32, precision=HIGHEST)
    h = h * jnp.exp(-t_ref[...] * dl_ref[...])
    o_ref[...] = h * lax.rsqrt(jnp.sum(h * h, axis=0, keepdims=True) + EPS)


def hyena_filters(L, w1, b1, w2, b2, w3, freq):
    z, t, deltas = hyena_features(L)
    n = w3.shape[1]
    tn = C_WIDTH
    w1p = jnp.zeros((HY_EMB_PAD, HY_HIDDEN), F32).at[:w1.shape[0]].set(w1.astype(F32))

    def row(v):
        return v.astype(F32).reshape(1, -1)

    return pl.pallas_call(
        _filter_kernel,
        grid=(n // tn,),
        in_specs=[pl.BlockSpec((L, HY_EMB_PAD), lambda j: (0, 0)),
                  pl.BlockSpec((L, 1), lambda j: (0, 0)),
                  pl.BlockSpec((HY_EMB_PAD, HY_HIDDEN), lambda j: (0, 0)),
                  pl.BlockSpec((1, HY_HIDDEN), lambda j: (0, 0)),
                  pl.BlockSpec((HY_HIDDEN, HY_HIDDEN), lambda j: (0, 0)),
                  pl.BlockSpec((1, HY_HIDDEN), lambda j: (0, 0)),
                  pl.BlockSpec((HY_HIDDEN, tn), lambda j: (0, j)),
                  pl.BlockSpec((1, HY_HIDDEN), lambda j: (0, 0)),
                  pl.BlockSpec((1, tn), lambda j: (0, 0))],
        out_specs=pl.BlockSpec((L, tn), lambda j: (0, j)),
        out_shape=jax.ShapeDtypeStruct((L, n), F32),
        compiler_params=_cparams(("parallel",)),
        name="hyena_filters",
    )(z, t, w1p, row(b1), w2.astype(F32), row(b2), w3.astype(F32), row(freq), deltas.reshape(1, -1))


def spectrum_mats(L):
    nf = 2 * L
    n2f = nf // HY_N1
    n2h = n2f // 2
    f = np.exp(-2j * np.pi * np.outer(np.arange(n2f), np.arange(n2h)) / n2f)
    fk = np.kron(f, np.eye(HY_SUB))
    ms1 = np.concatenate([fk.real, fk.imag], axis=0)
    k1 = np.arange(HY_N1)
    e1 = []
    for k2 in range(n2f):
        e = np.exp(-2j * np.pi * np.outer(n2f * k1 + k2, k1) / nf)
        e1.append(np.block([[e.real, -e.imag], [e.imag, e.real]]))
    return jnp.asarray(ms1, F32), jnp.asarray(np.stack(e1), F32)


def _spectrum_kernel(h_ref, ms1_ref, e1_ref, o_ref, buf_ref):
    n2h, _, C = h_ref.shape
    n2f = 2 * n2h

    def s1_body(j, carry):
        r0 = pl.multiple_of(j * HY_SUB, HY_SUB)
        slab = h_ref[:, pl.ds(r0, HY_SUB), :].reshape(n2h * HY_SUB, C)
        out = jnp.dot(ms1_ref[...], slab, preferred_element_type=F32, precision=HIGHEST)
        buf_ref[:, pl.ds(r0, HY_SUB), :] = out[:n2f * HY_SUB].reshape(n2f, HY_SUB, C)
        buf_ref[:, pl.ds(HY_N1 + r0, HY_SUB), :] = out[n2f * HY_SUB:].reshape(n2f, HY_SUB, C)
        return carry

    lax.fori_loop(0, HY_N1 // HY_SUB, s1_body, 0)

    def e_body(k2, carry):
        o_ref[0, k2] = jnp.dot(e1_ref[k2], buf_ref[k2], preferred_element_type=F32, precision=HIGHEST)
        return carry

    lax.fori_loop(0, n2f, e_body, 0)


def hyena_spectrum(filt, hy_bias):
    L = filt.shape[0]
    n2h = L // HY_N1
    n2f = 2 * n2h
    nseq = filt.shape[1] // C_WIDTH
    ncb = C_WIDTH // HY_CB
    ms1, e1 = spectrum_mats(L)
    spec = pl.pallas_call(
        _spectrum_kernel,
        grid=(nseq * ncb,),
        in_specs=[pl.BlockSpec((n2h, HY_N1, HY_CB), lambda i: (0, 0, i)),
                  pl.BlockSpec(ms1.shape, lambda i: (0, 0)),
                  pl.BlockSpec(e1.shape, lambda i: (0, 0, 0))],
        out_specs=pl.BlockSpec((1, n2f, 2 * HY_N1, HY_CB), lambda i: (i // ncb, 0, 0, i % ncb)),
        out_shape=jax.ShapeDtypeStruct((nseq, n2f, 2 * HY_N1, C_WIDTH), F32),
        scratch_shapes=[pltpu.VMEM((n2f, 2 * HY_N1, HY_CB), F32)],
        compiler_params=_cparams(("arbitrary",)),
        name="hyena_spectrum",
    )(filt.reshape(n2h, HY_N1, nseq * C_WIDTH), ms1, e1)
    spec = spec.reshape(HY_DIRS, HY_ORDER, n2f, 2, HY_N1, C_WIDTH)
    out = []
    for o in range(HY_ORDER):
        re = spec[0, o, :, 0] + spec[1, o, :, 0] + hy_bias[o].astype(F32)[None, None, :]
        im = spec[0, o, :, 1] - spec[1, o, :, 1]
        out.append(jnp.concatenate([re, im], axis=1))
    return out


def _rope_partner(width):
    j = np.arange(width)
    return np.where((j % 32) < 16, j + 16, j - 16)


def prep_w_in(w):
    ak, av, bk, bv, aq, bq, cin, gates = jnp.split(w, SPLITS, axis=-1)
    aq, bq = aq * Q_SCALE, bq * Q_SCALE
    aqr = aq[:, _rope_partner(A_Q_W)]
    akr = ak[:, _rope_partner(A_KV_W)]
    pad = jnp.zeros((w.shape[0], P_COLS - OFF_AV - A_KV_W), w.dtype)
    return jnp.concatenate([gates, cin, aq, aqr, bq, bk, bv, ak, akr, av, pad], axis=-1).astype(BF16)


def kernel(x, c, ctx, c_ctx, w_mod, b_mod, g_mix_pre, g_mix_post, g_ffn_pre, g_ffn_post,
           w_in, sink_a, rpb_b, conv_c_w, conv_c_b, hy_w1, hy_b1, hy_w2, hy_b2, hy_w3,
           hy_freq, hy_bias, w_branch, w_out, ffn_w1, ffn_w3, ffn_w2,
           moe_router, moe_w1, moe_w3, moe_w2):
    B, L, D = x.shape
    Lc = ctx.shape[1]
    N, Nc = B * L, B * Lc
    x2 = x.reshape(N, D)
    ctx2 = ctx.reshape(Nc, D)
    cs = jnp.concatenate([c, c_ctx[None, :], jnp.zeros((7, D), F32)], axis=0)

    for l in range(DEPTH):
        last = l == DEPTH - 1
        mod = modulation(cs, w_mod[l], b_mod[l])
        mods = [mod[:B, k * D:(k + 1) * D].reshape(B, 1, D) for k in range(6)]
        modc = [mod[B:B + 1, k * D:(k + 1) * D].reshape(1, 1, D) for k in range(6)]
        sh_m, sc_m, gt_m, sh_f, sc_f, gt_f = mods
        shc_m, scc_m, gtc_m, shc_f, scc_f, gtc_f = modc

        w_all = prep_w_in(w_in[l])
        p = proj(x2, g_mix_pre[l], sc_m, sh_m, w_all, L)
        if last:
            pc = proj(ctx2, g_mix_pre[l], scc_m, shc_m, w_all[:, KV_OFF:], Nc)
            coff = KV_OFF
        else:
            pc = proj(ctx2, g_mix_pre[l], scc_m, shc_m, w_all, Nc)
            coff = 0

        p3 = p.reshape(B, L, -1)
        pc3 = pc.reshape(B, Lc, -1)
        qr, kr = rope_qk(p, L)
        ya = window_attention(qr.reshape(B, L, -1), kr.reshape(B, L, -1), p3, pc3, coff, sink_a[l])
        yb = neighbourhood_attention(p3, pc3, coff, rpb_b[l])
        filt = hyena_filters(L, hy_w1[l], hy_b1[l], hy_w2[l], hy_b2[l], hy_w3[l], hy_freq[l])
        g0, g1 = hyena_spectrum(filt, hy_bias[l])
        yc = hyena_conv(p3, conv_c_w[l], conv_c_b[l], g0, g1)
        wb = w_branch[l].astype(BF16)
        wo = w_out[l].astype(BF16)
        x2 = merge(ya.reshape(N, -1).astype(BF16), yb.reshape(N, -1).astype(BF16),
                   yc.reshape(N, -1).astype(BF16), p, wb, wo, x2, g_mix_post[l], gt_m, L)
        if not last:
            yac, ybc = ctx_attention(pc3, sink_a[l])
            filt_c = hyena_filters(Lc, hy_w1[l], hy_b1[l], hy_w2[l], hy_b2[l], hy_w3[l], hy_freq[l])
            gc0, gc1 = hyena_spectrum(filt_c, hy_bias[l])
            ycc = hyena_conv(pc3, conv_c_w[l], conv_c_b[l], gc0, gc1)
            ctx2 = merge(yac.reshape(Nc, -1).astype(BF16), ybc.reshape(Nc, -1).astype(BF16),
                         ycc.reshape(Nc, -1).astype(BF16), pc, wb, wo, ctx2, g_mix_post[l], gtc_m, Nc)

        j = l // 2
        if l % 2 == 0:
            w1, w3, w2 = ffn_w1[j].astype(BF16), ffn_w3[j].astype(BF16), ffn_w2[j].astype(BF16)
            x2 = ffn(x2, g_ffn_pre[l], sc_f, sh_f, w1, w3, w2, g_ffn_post[l], gt_f, L)
            if not last:
                ctx2 = ffn(ctx2, g_ffn_pre[l], scc_f, shc_f, w1, w3, w2, g_ffn_post[l], gtc_f, Nc)
        else:
            w1, w3, w2 = moe_w1[j].astype(BF16), moe_w3[j].astype(BF16), moe_w2[j].astype(BF16)
            x2 = moe(x2, g_ffn_pre[l], sc_f, sh_f, moe_router[j], w1, w3, w2, g_ffn_post[l], gt_f, L)
            if not last:
                ctx2 = moe(ctx2, g_ffn_pre[l], scc_f, shc_f, moe_router[j], w1, w3, w2,
                           g_ffn_post[l], gtc_f, Nc)
    return x2.reshape(B, L, D)
```

```python
import functools
import math

import jax
import jax.numpy as jnp
import numpy as np
from jax import lax
from jax.experimental import pallas as pl
from jax.experimental.pallas import tpu as pltpu

F32 = jnp.float32
BF16 = jnp.bfloat16

D_MODEL = 1024
DEPTH = 2
GRID_W = 64
HEAD_DIM = 64
A_HEADS = 8
A_KV_HEADS = 2
A_GROUP = A_HEADS // A_KV_HEADS
WINDOW = 128
A_BLOCK = 128
B_HEADS = 8
NA_ROWS = 8
NA_COLS = 16
C_WIDTH = 512
HY_ORDER = 2
HY_DIRS = 2
HY_BANDS = 16
HY_TARGET = 1e-2
HY_FAST = 0.3
HY_SLOW = 1.5
N_BRANCH = 3
N_EXPERTS = 8
TOP_K = 2
ROPE_BASE = 10000.0
EPS = 1e-6
NEG = -1e30

A_KV_W = A_KV_HEADS * HEAD_DIM
A_Q_W = A_HEADS * HEAD_DIM
B_W = B_HEADS * HEAD_DIM
KV_COLS = 2 * A_KV_W + 2 * B_W
SPLITS = (A_KV_W, 2 * A_KV_W, 2 * A_KV_W + B_W, KV_COLS,
          KV_COLS + A_Q_W, KV_COLS + A_Q_W + B_W,
          KV_COLS + A_Q_W + B_W + 3 * C_WIDTH)

OFF_GATES = 0
OFF_CIN = 3072
OFF_AQ = 4608
OFF_AQR = 5120
OFF_BQ = 5632
OFF_BK = 6144
OFF_BV = 6656
OFF_AK = 7168
OFF_AKR = 7296
OFF_AV = 7424
P_COLS = 7680
KV_OFF = OFF_BK
KV_W = P_COLS - KV_OFF

VMEM_LIMIT = 48 * 1024 * 1024
LOG2E = math.log2(math.e)
Q_SCALE = HEAD_DIM ** -0.5 * LOG2E


def _cparams(sem):
    return pltpu.CompilerParams(dimension_semantics=sem, vmem_limit_bytes=VMEM_LIMIT)


def _rms(y, g):
    return y * lax.rsqrt(jnp.mean(y * y, axis=-1, keepdims=True) + EPS) * g


def _norm_mod(x, g, sc, sh):
    return _rms(x, g) * (1.0 + sc) + sh


def _mod_kernel(c_ref, w_ref, b_ref, o_ref):
    c = c_ref[...]
    s = c * jax.nn.sigmoid(c)
    o_ref[...] = jnp.dot(s, w_ref[...], preferred_element_type=F32,
                         precision=lax.Precision.HIGHEST) + b_ref[...]


def modulation(cs, w, b, tn=512):
    m, d = cs.shape
    n = w.shape[1]
    return pl.pallas_call(
        _mod_kernel,
        grid=(n // tn,),
        in_specs=[pl.BlockSpec((m, d), lambda j: (0, 0)),
                  pl.BlockSpec((d, tn), lambda j: (0, j)),
                  pl.BlockSpec((1, tn), lambda j: (0, j))],
        out_specs=pl.BlockSpec((m, tn), lambda j: (0, j)),
        out_shape=jax.ShapeDtypeStruct((m, n), F32),
        compiler_params=_cparams(("parallel",)),
        name="modulation",
    )(cs, w, b.reshape(1, n))


def _proj_kernel(x_ref, g_ref, sc_ref, sh_ref, w_ref, o_ref, h_ref):
    @pl.when(pl.program_id(1) == 0)
    def _():
        h_ref[...] = _norm_mod(x_ref[...], g_ref[...], sc_ref[0], sh_ref[0]).astype(BF16)

    o_ref[...] = jnp.dot(h_ref[...], w_ref[...], preferred_element_type=F32).astype(o_ref.dtype)


def proj(x2, g, sc, sh, w, rows_per_mod, tm=1024, tn=3840):
    m, d = x2.shape
    n = w.shape[1]
    tm = min(tm, m)
    tn = min(tn, n)
    tpb = rows_per_mod // tm
    return pl.pallas_call(
        _proj_kernel,
        grid=(m // tm, n // tn),
        in_specs=[pl.BlockSpec((tm, d), lambda i, j: (i, 0)),
                  pl.BlockSpec((1, d), lambda i, j: (0, 0)),
                  pl.BlockSpec((1, 1, d), lambda i, j: (i // tpb, 0, 0)),
                  pl.BlockSpec((1, 1, d), lambda i, j: (i // tpb, 0, 0)),
                  pl.BlockSpec((d, tn), lambda i, j: (0, j))],
        out_specs=pl.BlockSpec((tm, tn), lambda i, j: (i, j)),
        out_shape=jax.ShapeDtypeStruct((m, n), BF16),
        scratch_shapes=[pltpu.VMEM((tm, d), BF16)],
        compiler_params=_cparams(("parallel", "arbitrary")),
        name="proj",
    )(x2, g.reshape(1, d), sc, sh, w)


def _merge_kernel(ya_ref, yb_ref, yc_ref, ga_ref, gb_ref, gc_ref, wb_ref, wo_ref,
                  x_ref, gp_ref, gt_ref, o_ref):
    def branch(y_ref, g_ref, k):
        t = jnp.dot(y_ref[...], wb_ref[k], preferred_element_type=F32)
        return jax.nn.sigmoid(g_ref[...].astype(F32)) * t

    m = branch(ya_ref, ga_ref, 0) + branch(yb_ref, gb_ref, 1) + branch(yc_ref, gc_ref, 2)
    y = jnp.dot(m.astype(BF16), wo_ref[...], preferred_element_type=F32)
    o_ref[...] = x_ref[...] + gt_ref[0] * _rms(y, gp_ref[...])


def merge(ya, yb, yc, p, wb, wo, x2, gpost, gt, rows_per_mod, tm=512):
    m, d = x2.shape
    bw = ya.shape[1]
    tm = min(tm, m)
    tpb = rows_per_mod // tm
    gblk = OFF_GATES // d
    return pl.pallas_call(
        _merge_kernel,
        grid=(m // tm,),
        in_specs=[pl.BlockSpec((tm, bw), lambda i: (i, 0)),
                  pl.BlockSpec((tm, bw), lambda i: (i, 0)),
                  pl.BlockSpec((tm, bw), lambda i: (i, 0)),
                  pl.BlockSpec((tm, d), lambda i: (i, gblk)),
                  pl.BlockSpec((tm, d), lambda i: (i, gblk + 1)),
                  pl.BlockSpec((tm, d), lambda i: (i, gblk + 2)),
                  pl.BlockSpec((N_BRANCH, bw, d), lambda i: (0, 0, 0)),
                  pl.BlockSpec((d, d), lambda i: (0, 0)),
                  pl.BlockSpec((tm, d), lambda i: (i, 0)),
                  pl.BlockSpec((1, d), lambda i: (0, 0)),
                  pl.BlockSpec((1, 1, d), lambda i: (i // tpb, 0, 0))],
        out_specs=pl.BlockSpec((tm, d), lambda i: (i, 0)),
        out_shape=jax.ShapeDtypeStruct((m, d), F32),
        compiler_params=_cparams(("parallel",)),
        name="merge",
    )(ya, yb, yc, p, p, p, wb, wo, x2, gpost.reshape(1, d), gt)


def _ffn_kernel(x_ref, g_ref, sc_ref, sh_ref, w1_ref, w3_ref, w2_ref, gp_ref, gt_ref,
                o_ref, h_ref, acc_ref):
    f = pl.program_id(1)

    @pl.when(f == 0)
    def _():
        h_ref[...] = _norm_mod(x_ref[...], g_ref[...], sc_ref[0], sh_ref[0]).astype(BF16)
        acc_ref[...] = jnp.zeros_like(acc_ref)

    h = h_ref[...]
    a = jnp.dot(h, w1_ref[...], preferred_element_type=F32)
    b = jnp.dot(h, w3_ref[...], preferred_element_type=F32)
    act = (a * jax.nn.sigmoid(a) * b).astype(BF16)
    acc_ref[...] += jnp.dot(act, w2_ref[...], preferred_element_type=F32)

    @pl.when(f == pl.num_programs(1) - 1)
    def _():
        o_ref[...] = x_ref[...] + gt_ref[0] * _rms(acc_ref[...], gp_ref[...])


def ffn(x2, g, sc, sh, w1, w3, w2, gpost, gt, rows_per_mod, tm=512, tf=2816):
    m, d = x2.shape
    ff = w1.shape[1]
    tm = min(tm, m)
    tpb = rows_per_mod // tm
    return pl.pallas_call(
        _ffn_kernel,
        grid=(m // tm, ff // tf),
        in_specs=[pl.BlockSpec((tm, d), lambda i, f: (i, 0)),
                  pl.BlockSpec((1, d), lambda i, f: (0, 0)),
                  pl.BlockSpec((1, 1, d), lambda i, f: (i // tpb, 0, 0)),
                  pl.BlockSpec((1, 1, d), lambda i, f: (i // tpb, 0, 0)),
                  pl.BlockSpec((d, tf), lambda i, f: (0, f)),
                  pl.BlockSpec((d, tf), lambda i, f: (0, f)),
                  pl.BlockSpec((tf, d), lambda i, f: (f, 0)),
                  pl.BlockSpec((1, d), lambda i, f: (0, 0)),
                  pl.BlockSpec((1, 1, d), lambda i, f: (i // tpb, 0, 0))],
        out_specs=pl.BlockSpec((tm, d), lambda i, f: (i, 0)),
        out_shape=jax.ShapeDtypeStruct((m, d), F32),
        scratch_shapes=[pltpu.VMEM((tm, d), BF16), pltpu.VMEM((tm, d), F32)],
        compiler_params=_cparams(("parallel", "arbitrary")),
        name="ffn",
    )(x2, g.reshape(1, d), sc, sh, w1, w3, w2, gpost.reshape(1, d), gt)


def _router_kernel(x_ref, g_ref, sc_ref, sh_ref, rh_ref, rl_ref, h_ref, lg_ref):
    h = _norm_mod(x_ref[...], g_ref[...], sc_ref[0], sh_ref[0])
    hb = h.astype(BF16)
    h_ref[...] = hb
    hl = (h - hb.astype(F32)).astype(BF16)
    rh = rh_ref[...]
    lg_ref[...] = (jnp.dot(hb, rh, preferred_element_type=F32)
                   + jnp.dot(hl, rh, preferred_element_type=F32)
                   + jnp.dot(hb, rl_ref[...], preferred_element_type=F32))


def router(x2, g, sc, sh, r, rows_per_mod, tm=1024):
    m, d = x2.shape
    e = r.shape[1]
    tm = min(tm, m)
    tpb = rows_per_mod // tm
    rh = r.astype(BF16)
    rl = (r - rh.astype(F32)).astype(BF16)
    return pl.pallas_call(
        _router_kernel,
        grid=(m // tm,),
        in_specs=[pl.BlockSpec((tm, d), lambda i: (i, 0)),
                  pl.BlockSpec((1, d), lambda i: (0, 0)),
                  pl.BlockSpec((1, 1, d), lambda i: (i // tpb, 0, 0)),
                  pl.BlockSpec((1, 1, d), lambda i: (i // tpb, 0, 0)),
                  pl.BlockSpec((d, e), lambda i: (0, 0)),
                  pl.BlockSpec((d, e), lambda i: (0, 0))],
        out_specs=[pl.BlockSpec((tm, d), lambda i: (i, 0)),
                   pl.BlockSpec((tm, e), lambda i: (i, 0))],
        out_shape=[jax.ShapeDtypeStruct((m, d), BF16), jax.ShapeDtypeStruct((m, e), F32)],
        compiler_params=_cparams(("parallel",)),
        name="router",
    )(x2, g.reshape(1, d), sc, sh, rh, rl)


def _expert_kernel(te_ref, hs_ref, w1_ref, w3_ref, w2_ref, o_ref, acc_ref):
    i = pl.program_id(0)
    f = pl.program_id(1)
    nf = pl.num_programs(1)
    valid = te_ref[i] >= 0

    @pl.when(f == 0)
    def _():
        acc_ref[...] = jnp.zeros_like(acc_ref)

    @pl.when(valid)
    def _():
        h = hs_ref[...]
        a = jnp.dot(h, w1_ref[0], preferred_element_type=F32)
        b = jnp.dot(h, w3_ref[0], preferred_element_type=F32)
        act = (a * jax.nn.sigmoid(a) * b).astype(BF16)
        acc_ref[...] += jnp.dot(act, w2_ref[0], preferred_element_type=F32)

    @pl.when(f == nf - 1)
    def _():
        o_ref[...] = acc_ref[...].astype(o_ref.dtype)


def experts(tile_expert, hs, w1, w3, w2, tm=512, tf=1792):
    mp, d = hs.shape
    ff = w1.shape[2]

    def wmap(i, f, te):
        return (jnp.maximum(te[i], 0), 0, f)

    def w2map(i, f, te):
        return (jnp.maximum(te[i], 0), f, 0)

    return pl.pallas_call(
        _expert_kernel,
        grid_spec=pltpu.PrefetchScalarGridSpec(
            num_scalar_prefetch=1,
            grid=(mp // tm, ff // tf),
            in_specs=[pl.BlockSpec((tm, d), lambda i, f, te: (i, 0)),
                      pl.BlockSpec((1, d, tf), wmap),
                      pl.BlockSpec((1, d, tf), wmap),
                      pl.BlockSpec((1, tf, d), w2map)],
            out_specs=pl.BlockSpec((tm, d), lambda i, f, te: (i, 0)),
            scratch_shapes=[pltpu.VMEM((tm, d), F32)]),
        out_shape=jax.ShapeDtypeStruct((mp, d), BF16),
        compiler_params=_cparams(("parallel", "arbitrary")),
        name="experts",
    )(tile_expert, hs, w1, w3, w2)


def _combine_kernel(y1_ref, y2_ref, cw_ref, x_ref, gp_ref, gt_ref, o_ref):
    cw = cw_ref[...]
    y = cw[:, 0:1] * y1_ref[...].astype(F32) + cw[:, 1:2] * y2_ref[...].astype(F32)
    o_ref[...] = x_ref[...] + gt_ref[0] * _rms(y, gp_ref[...])


def combine(y1, y2, cw, x2, gpost, gt, rows_per_mod, tm=1024):
    m, d = x2.shape
    tm = min(tm, m)
    tpb = rows_per_mod // tm
    return pl.pallas_call(
        _combine_kernel,
        grid=(m // tm,),
        in_specs=[pl.BlockSpec((tm, d), lambda i: (i, 0)),
                  pl.BlockSpec((tm, d), lambda i: (i, 0)),
                  pl.BlockSpec((tm, TOP_K), lambda i: (i, 0)),
                  pl.BlockSpec((tm, d), lambda i: (i, 0)),
                  pl.BlockSpec((1, d), lambda i: (0, 0)),
                  pl.BlockSpec((1, 1, d), lambda i: (i // tpb, 0, 0))],
        out_specs=pl.BlockSpec((tm, d), lambda i: (i, 0)),
        out_shape=jax.ShapeDtypeStruct((m, d), F32),
        compiler_params=_cparams(("parallel",)),
        name="combine",
    )(y1, y2, cw, x2, gpost.reshape(1, d), gt)


def moe(x2, g, sc, sh, r, w1, w3, w2, gpost, gt, rows_per_mod, tm=1024):
    m, d = x2.shape
    e = r.shape[1]
    h, logits = router(x2, g, sc, sh, r, rows_per_mod)
    top_v, top_i = lax.top_k(logits, TOP_K)
    top_w = jax.nn.softmax(top_v, axis=-1)
    flat_e = top_i.reshape(-1)
    onehot = (flat_e[:, None] == jnp.arange(e)[None, :]).astype(jnp.int32)
    rank = jnp.cumsum(onehot, axis=0) - onehot
    counts = jnp.sum(onehot, axis=0)
    padded = ((counts + tm - 1) // tm) * tm
    pad_end = jnp.cumsum(padded)
    pad_start = pad_end - padded
    slot = pad_start[flat_e] + jnp.sum(rank * onehot, axis=1)
    mp = m * TOP_K + e * tm
    n_tiles = mp // tm
    tile_start = jnp.arange(n_tiles, dtype=jnp.int32) * tm
    tile_e = jnp.sum((tile_start[:, None] >= pad_end[None, :]).astype(jnp.int32), axis=1)
    te = jnp.minimum(tile_e, e - 1)
    tile_e = jnp.where(tile_start < pad_end[-1], tile_e, -1).astype(jnp.int32)
    order = jnp.argsort(flat_e, stable=True).astype(jnp.int32)
    first = jnp.cumsum(counts) - counts
    in_group = (tile_start - pad_start[te])[:, None] + jnp.arange(tm, dtype=jnp.int32)[None, :]
    live = (in_group < counts[te][:, None]) & (tile_e >= 0)[:, None]
    src = jnp.where(live, first[te][:, None] + in_group, 0)
    slot_token = order[src.reshape(-1)] // TOP_K
    hs = jnp.take(h, slot_token, axis=0, mode="clip")
    ys = experts(tile_e, hs, w1, w3, w2, tm=tm)
    slot2 = slot.reshape(m, TOP_K)
    y1 = jnp.take(ys, slot2[:, 0], axis=0, mode="clip")
    y2 = jnp.take(ys, slot2[:, 1], axis=0, mode="clip")
    return combine(y1, y2, top_w, x2, gpost, gt, rows_per_mod)


def rope_tables(L, width):
    t = np.arange(L)
    j = np.arange(width) % HEAD_DIM
    pos = np.where(j[None, :] < HEAD_DIM // 2, (t // GRID_W)[:, None], (t % GRID_W)[:, None])
    quarter = HEAD_DIM // 4
    inv = ROPE_BASE ** (-(j % quarter).astype(np.float32) / quarter)
    ang = jnp.asarray(pos.astype(np.float32)) * jnp.asarray(inv.astype(np.float32))[None, :]
    sign = np.where((j % (2 * quarter)) < quarter, -1.0, 1.0).astype(np.float32)
    return jnp.cos(ang), jnp.sin(ang) * jnp.asarray(sign)[None, :]


def _rope_kernel(aq_ref, aqr_ref, ak_ref, akr_ref, cq_ref, sq_ref, ck_ref, sk_ref, q_ref, k_ref):
    q = aq_ref[...].astype(F32) * cq_ref[...] + aqr_ref[...].astype(F32) * sq_ref[...]
    q_ref[...] = q.astype(BF16)
    k = ak_ref[...].astype(F32) * ck_ref[...] + akr_ref[...].astype(F32) * sk_ref[...]
    k_ref[...] = k.astype(BF16)


def rope_qk(p, L, tm=1024):
    m = p.shape[0]
    tm = min(tm, L)
    nt = L // tm
    cq, sq = rope_tables(L, A_Q_W)
    ck, sk = rope_tables(L, A_KV_W)
    return pl.pallas_call(
        _rope_kernel,
        grid=(m // tm,),
        in_specs=[pl.BlockSpec((tm, A_Q_W), lambda i: (i, OFF_AQ // A_Q_W)),
                  pl.BlockSpec((tm, A_Q_W), lambda i: (i, OFF_AQR // A_Q_W)),
                  pl.BlockSpec((tm, A_KV_W), lambda i: (i, OFF_AK // A_KV_W)),
                  pl.BlockSpec((tm, A_KV_W), lambda i: (i, OFF_AKR // A_KV_W)),
                  pl.BlockSpec((tm, A_Q_W), lambda i: (i % nt, 0)),
                  pl.BlockSpec((tm, A_Q_W), lambda i: (i % nt, 0)),
                  pl.BlockSpec((tm, A_KV_W), lambda i: (i % nt, 0)),
                  pl.BlockSpec((tm, A_KV_W), lambda i: (i % nt, 0))],
        out_specs=[pl.BlockSpec((tm, A_Q_W), lambda i: (i, 0)),
                   pl.BlockSpec((tm, A_KV_W), lambda i: (i, 0))],
        out_shape=[jax.ShapeDtypeStruct((m, A_Q_W), BF16), jax.ShapeDtypeStruct((m, A_KV_W), BF16)],
        compiler_params=_cparams(("parallel",)),
        name="rope_qk",
    )(p, p, p, p, cq, sq, ck, sk)


def _dot_nt(a, b):
    return lax.dot_general(a, b, (((1,), (1,)), ((), ())), preferred_element_type=F32)


A_SPAN = A_BLOCK + 2 * WINDOW


def _wa_kernel(sink_ref, q_ref, k_ref, v_ref, kc_ref, vc_ref, o_ref):
    i = pl.program_id(1)
    L = k_ref.shape[1]
    start = pl.multiple_of(jnp.clip(i * A_BLOCK - WINDOW, 0, L - A_SPAN), A_BLOCK)
    qpos = i * A_BLOCK + lax.broadcasted_iota(jnp.int32, (A_BLOCK, A_SPAN), 0)
    kpos = start + lax.broadcasted_iota(jnp.int32, (A_BLOCK, A_SPAN), 1)
    valid = jnp.abs(qpos - kpos) <= WINDOW
    q = q_ref[0]
    scores = []
    for h in range(A_HEADS):
        ks = slice((h // A_GROUP) * HEAD_DIM, (h // A_GROUP + 1) * HEAD_DIM)
        qh = q[:, h * HEAD_DIM:(h + 1) * HEAD_DIM]
        s_loc = jnp.where(valid, _dot_nt(qh, k_ref[0, pl.ds(start, A_SPAN), ks]), NEG)
        scores.append(jnp.concatenate([s_loc, _dot_nt(qh, kc_ref[0, :, ks])], axis=1))
    probs, sinks = [], []
    for h in range(A_HEADS):
        sink = sink_ref[h]
        m = jnp.maximum(jnp.max(scores[h], axis=-1, keepdims=True), sink)
        probs.append(jnp.exp2((scores[h] - m).astype(BF16)))
        sinks.append(jnp.exp2(sink - m))
    ones = jnp.ones((A_SPAN + kc_ref.shape[1], HEAD_DIM), BF16)
    for g in range(A_KV_HEADS):
        ks = slice(g * HEAD_DIM, (g + 1) * HEAD_DIM)
        v_ext = jnp.concatenate(
            [jnp.concatenate([v_ref[0, pl.ds(start, A_SPAN), ks], vc_ref[0, :, ks]], axis=0), ones], axis=1)
        for h in range(g * A_GROUP, (g + 1) * A_GROUP):
            oe = jnp.dot(probs[h], v_ext, preferred_element_type=F32)
            den = pltpu.roll(oe, HEAD_DIM, axis=1) + sinks[h]
            o_ref[0, :, h * HEAD_DIM:(h + 1) * HEAD_DIM] = (oe / den)[:, :HEAD_DIM].astype(BF16)


def window_attention(qr, kr, p3, pc3, coff, sink):
    B, L, _ = qr.shape
    Lc = pc3.shape[1]
    return pl.pallas_call(
        _wa_kernel,
        grid=(B, L // A_BLOCK),
        in_specs=[pl.BlockSpec(memory_space=pltpu.SMEM),
                  pl.BlockSpec((1, A_BLOCK, A_Q_W), lambda b, i: (b, i, 0)),
                  pl.BlockSpec((1, L, A_KV_W), lambda b, i: (b, 0, 0)),
                  pl.BlockSpec((1, L, A_KV_W), lambda b, i: (b, 0, OFF_AV // A_KV_W)),
                  pl.BlockSpec((1, Lc, A_KV_W), lambda b, i: (b, 0, (OFF_AK - coff) // A_KV_W)),
                  pl.BlockSpec((1, Lc, A_KV_W), lambda b, i: (b, 0, (OFF_AV - coff) // A_KV_W))],
        out_specs=pl.BlockSpec((1, A_BLOCK, A_Q_W), lambda b, i: (b, i, 0)),
        out_shape=jax.ShapeDtypeStruct((B, L, A_Q_W), BF16),
        compiler_params=_cparams(("parallel", "arbitrary")),
        name="window_attention",
    )(sink.astype(F32) * LOG2E, qr, kr, p3, pc3, pc3)


NA_QROWS = 4
NA_KROWS = NA_ROWS + NA_QROWS
NA_TQ = NA_QROWS * GRID_W
NA_TK = NA_KROWS * GRID_W


def na_bias_table(rpb, rows):
    col = np.arange(GRID_W)
    col_start = np.clip(col - NA_COLS // 2, 0, GRID_W - NA_COLS)
    col_ok = (col[None, :] >= col_start[:, None]) & (col[None, :] < col_start[:, None] + NA_COLS)
    dc = np.clip(col[None, :] - col[:, None] + NA_COLS - 1, 0, 2 * NA_COLS - 2)
    a = np.arange(NA_QROWS)
    j = np.arange(NA_KROWS)
    dr_all, ok_all = [], []
    for r0, u in ((0, 0), (NA_QROWS, 0), (rows - NA_QROWS, rows - NA_KROWS)):
        r = r0 + a
        rs = np.clip(r - NA_ROWS // 2, 0, rows - NA_ROWS)
        kr = u + j
        ok = (kr[None, :] >= rs[:, None]) & (kr[None, :] < rs[:, None] + NA_ROWS)
        dr = np.clip(kr[None, :] - r[:, None] + NA_ROWS - 1, 0, 2 * NA_ROWS - 2)
        dr_all.append(dr)
        ok_all.append(ok)
    n_dr, n_dc = 2 * NA_ROWS - 1, 2 * NA_COLS - 1
    H = rpb.shape[0]
    onehot = (dc.reshape(-1)[None, :] == np.arange(n_dc)[:, None]).astype(np.float32)
    tiles = jnp.dot(rpb.astype(F32).reshape(H * n_dr, n_dc), jnp.asarray(onehot),
                    precision=lax.Precision.HIGHEST).reshape(H, n_dr, GRID_W, GRID_W)
    tiles = jnp.where(jnp.asarray(col_ok)[None, None], tiles * LOG2E, NEG)
    tiles = jnp.concatenate([tiles, jnp.full((H, 1, GRID_W, GRID_W), NEG, F32)], axis=1)
    sel = np.where(np.stack(ok_all), np.stack(dr_all), n_dr)
    out = tiles[:, sel.reshape(-1)].reshape(H, 3, NA_QROWS, NA_KROWS, GRID_W, GRID_W)
    return out.transpose(1, 0, 2, 4, 3, 5).reshape(3, H, NA_TQ, NA_TK)


def _na_kernel(q_ref, k_ref, v_ref, kc_ref, vc_ref, bias_ref, o_ref):
    blk = pl.program_id(1)
    rows = k_ref.shape[1] // GRID_W
    u = jnp.clip(blk * NA_QROWS - NA_ROWS // 2, 0, rows - NA_KROWS)
    start = pl.multiple_of(u * GRID_W, GRID_W)
    q = q_ref[0]
    kt = jnp.concatenate([k_ref[0, pl.ds(start, NA_TK), :], kc_ref[0]], axis=0)
    vt = jnp.concatenate([v_ref[0, pl.ds(start, NA_TK), :], vc_ref[0]], axis=0)
    pair = 2 * HEAD_DIM
    lane = lax.broadcasted_iota(jnp.int32, (NA_TQ, pair), 1)
    ones = jnp.ones((kt.shape[0], pair), BF16)
    zero = jnp.zeros((), BF16)
    scores = []
    for h in range(B_HEADS):
        ps = slice((h // 2) * pair, (h // 2 + 1) * pair)
        qh = jnp.where((lane < HEAD_DIM) if h % 2 == 0 else (lane >= HEAD_DIM), q[:, ps], zero)
        s = _dot_nt(qh, kt[:, ps])
        scores.append(jnp.concatenate([s[:, :NA_TK] + bias_ref[0, h], s[:, NA_TK:]], axis=1))
    probs = [jnp.exp2((s - jnp.max(s, axis=-1, keepdims=True)).astype(BF16)) for s in scores]
    for i in range(B_HEADS // 2):
        ps = slice(i * pair, (i + 1) * pair)
        v_ext = jnp.concatenate([vt[:, ps], ones], axis=1)
        oe = [jnp.dot(probs[2 * i + t], v_ext, preferred_element_type=F32) for t in range(2)]
        o = [x[:, :pair] / x[:, pair:] for x in oe]
        o_ref[0, :, ps] = jnp.where(lane < HEAD_DIM, o[0], o[1]).astype(BF16)


def neighbourhood_attention(p3, pc3, coff, rpb):
    B, L, _ = p3.shape
    Lc = pc3.shape[1]
    nblk = L // NA_TQ
    bias = na_bias_table(rpb, L // GRID_W)

    def bias_map(b, i):
        return (jnp.where(i == 0, 0, jnp.where(i == nblk - 1, 2, 1)), 0, 0, 0)

    return pl.pallas_call(
        _na_kernel,
        grid=(B, nblk),
        in_specs=[pl.BlockSpec((1, NA_TQ, B_W), lambda b, i: (b, i, OFF_BQ // B_W)),
                  pl.BlockSpec((1, L, B_W), lambda b, i: (b, 0, OFF_BK // B_W)),
                  pl.BlockSpec((1, L, B_W), lambda b, i: (b, 0, OFF_BV // B_W)),
                  pl.BlockSpec((1, Lc, B_W), lambda b, i: (b, 0, (OFF_BK - coff) // B_W)),
                  pl.BlockSpec((1, Lc, B_W), lambda b, i: (b, 0, (OFF_BV - coff) // B_W)),
                  pl.BlockSpec((1, B_HEADS, NA_TQ, NA_TK), bias_map)],
        out_specs=pl.BlockSpec((1, NA_TQ, B_W), lambda b, i: (b, i, 0)),
        out_shape=jax.ShapeDtypeStruct((B, L, B_W), BF16),
        compiler_params=_cparams(("parallel", "arbitrary")),
        name="neighbourhood_attention",
    )(p3, p3, p3, pc3, pc3, bias)


def _ctx_attn_kernel(sink_ref, aq_ref, ak_ref, av_ref, bq_ref, bk_ref, bv_ref, oa_ref, ob_ref):
    aq = aq_ref[0]
    bq = bq_ref[0]
    for h in range(A_HEADS):
        hs = slice(h * HEAD_DIM, (h + 1) * HEAD_DIM)
        g = h // A_GROUP
        ks = slice(g * HEAD_DIM, (g + 1) * HEAD_DIM)
        s = _dot_nt(aq[:, hs], ak_ref[0, :, ks])
        sink = sink_ref[h]
        m = jnp.maximum(jnp.max(s, axis=-1, keepdims=True), sink)
        p = jnp.exp2(s - m)
        den = jnp.sum(p, axis=-1, keepdims=True) + jnp.exp2(sink - m)
        o = jnp.dot(p.astype(BF16), av_ref[0, :, ks], preferred_element_type=F32)
        oa_ref[0, :, hs] = (o / den).astype(BF16)
    for h in range(B_HEADS):
        hs = slice(h * HEAD_DIM, (h + 1) * HEAD_DIM)
        s = _dot_nt(bq[:, hs], bk_ref[0, :, hs])
        m = jnp.max(s, axis=-1, keepdims=True)
        p = jnp.exp2(s - m)
        den = jnp.sum(p, axis=-1, keepdims=True)
        o = jnp.dot(p.astype(BF16), bv_ref[0, :, hs], preferred_element_type=F32)
        ob_ref[0, :, hs] = (o / den).astype(BF16)


def ctx_attention(pc3, sink):
    B, Lc, _ = pc3.shape
    return pl.pallas_call(
        _ctx_attn_kernel,
        grid=(B,),
        in_specs=[pl.BlockSpec(memory_space=pltpu.SMEM),
                  pl.BlockSpec((1, Lc, A_Q_W), lambda b: (b, 0, OFF_AQ // A_Q_W)),
                  pl.BlockSpec((1, Lc, A_KV_W), lambda b: (b, 0, OFF_AK // A_KV_W)),
                  pl.BlockSpec((1, Lc, A_KV_W), lambda b: (b, 0, OFF_AV // A_KV_W)),
                  pl.BlockSpec((1, Lc, B_W), lambda b: (b, 0, OFF_BQ // B_W)),
                  pl.BlockSpec((1, Lc, B_W), lambda b: (b, 0, OFF_BK // B_W)),
                  pl.BlockSpec((1, Lc, B_W), lambda b: (b, 0, OFF_BV // B_W))],
        out_specs=[pl.BlockSpec((1, Lc, A_Q_W), lambda b: (b, 0, 0)),
                   pl.BlockSpec((1, Lc, B_W), lambda b: (b, 0, 0))],
        out_shape=[jax.ShapeDtypeStruct((B, Lc, A_Q_W), BF16), jax.ShapeDtypeStruct((B, Lc, B_W), BF16)],
        compiler_params=_cparams(("parallel",)),
        name="ctx_attention",
    )(sink.astype(F32) * LOG2E, pc3, pc3, pc3, pc3, pc3, pc3)


HY_N1 = 128
HY_SUB = 8
HY_CB = 256
HY_S_UNROLL = 16
HY_E_UNROLL = 32


def hyena_mats(L):
    nf = 2 * L
    n2f = nf // HY_N1
    n2h = n2f // 2
    eye = np.eye(HY_SUB)

    def real_block(m):
        return np.block([[m.real, -m.imag], [m.imag, m.real]])

    f = np.exp(-2j * np.pi * np.outer(np.arange(n2f), np.arange(n2h)) / n2f)
    ms1 = real_block(np.kron(f, eye))
    fi = np.exp(2j * np.pi * np.outer(np.arange(n2h), np.arange(n2f)) / n2f) / nf
    ms2 = real_block(np.kron(fi, eye))
    k1 = np.arange(HY_N1)
    n1 = np.arange(HY_N1)
    e1, e2 = [], []
    for k2 in range(n2f):
        e = np.exp(-2j * np.pi * np.outer(n2f * k1 + k2, n1) / nf)
        e1.append(real_block(e))
        e2.append(real_block(np.conj(e).T))
    to = lambda m: jnp.asarray(np.asarray(m, np.float32), dtype=BF16)
    return to(ms1), to(np.stack(e1)), to(np.stack(e2)), to(ms2)


def _short_conv(u, w_ref, b_ref):
    L = u.shape[0]
    row = lax.broadcasted_iota(jnp.int32, (HY_SUB, u.shape[1]), 0)
    prev = pltpu.roll(u, 1, axis=0)
    prev = jnp.concatenate([jnp.where(row == 0, 0.0, prev[:HY_SUB]), prev[HY_SUB:]], axis=0)
    nxt = pltpu.roll(u, L - 1, axis=0)
    nxt = jnp.concatenate([nxt[:L - HY_SUB], jnp.where(row == HY_SUB - 1, 0.0, nxt[L - HY_SUB:])], axis=0)
    return prev * w_ref[0:1, :] + u * w_ref[1:2, :] + nxt * w_ref[2:3, :] + b_ref[...]


def _hyena_kernel(u_ref, gate_ref, cwu_ref, cbu_ref, cwg_ref, cbg_ref,
                  g_ref, ms1_ref, e1_ref, e2_ref, ms2_ref, o_ref, in_ref, buf_ref, *, conv_u):
    n2h, _, C = u_ref.shape[1:]
    n2f = 2 * n2h
    L = n2h * HY_N1
    half = HY_N1

    for s in range(2):
        u = u_ref[s].reshape(L, C).astype(F32)
        if conv_u:
            u = _short_conv(u, cwu_ref, cbu_ref)
        in_ref[s] = u.reshape(n2h, HY_N1, C)

    def s1_body(j, carry):
        r0 = pl.multiple_of(j * HY_SUB, HY_SUB)
        sre = in_ref[0, :, pl.ds(r0, HY_SUB), :].reshape(n2h * HY_SUB, C)
        sim = in_ref[1, :, pl.ds(r0, HY_SUB), :].reshape(n2h * HY_SUB, C)
        slab = jnp.concatenate([sre, sim], axis=0).astype(BF16)
        out = jnp.dot(ms1_ref[...], slab, preferred_element_type=F32)
        buf_ref[:, pl.ds(r0, HY_SUB), :] = out[:n2f * HY_SUB].reshape(n2f, HY_SUB, C)
        buf_ref[:, pl.ds(half + r0, HY_SUB), :] = out[n2f * HY_SUB:].reshape(n2f, HY_SUB, C)
        return carry

    lax.fori_loop(0, HY_N1 // HY_SUB, s1_body, 0, unroll=HY_S_UNROLL)

    def e_body(k2, carry):
        blk = buf_ref[k2].astype(BF16)
        x = jnp.dot(e1_ref[k2], blk, preferred_element_type=F32)
        xr, xi = x[:half], x[half:]
        gr = g_ref[k2, :half, :]
        gi = g_ref[k2, half:, :]
        y = jnp.concatenate([xr * gr - xi * gi, xr * gi + xi * gr], axis=0).astype(BF16)
        buf_ref[k2] = jnp.dot(e2_ref[k2], y, preferred_element_type=F32)
        return carry

    lax.fori_loop(0, n2f, e_body, 0, unroll=min(n2f, HY_E_UNROLL))

    def s2_body(j, carry):
        r0 = pl.multiple_of(j * HY_SUB, HY_SUB)
        sre = buf_ref[:, pl.ds(r0, HY_SUB), :].reshape(n2f * HY_SUB, C)
        sim = buf_ref[:, pl.ds(half + r0, HY_SUB), :].reshape(n2f * HY_SUB, C)
        slab = jnp.concatenate([sre, sim], axis=0).astype(BF16)
        out = jnp.dot(ms2_ref[...], slab, preferred_element_type=F32)
        in_ref[0, :, pl.ds(r0, HY_SUB), :] = out[:n2h * HY_SUB].reshape(n2h, HY_SUB, C)
        in_ref[1, :, pl.ds(r0, HY_SUB), :] = out[n2h * HY_SUB:].reshape(n2h, HY_SUB, C)
        return carry

    lax.fori_loop(0, HY_N1 // HY_SUB, s2_body, 0, unroll=HY_S_UNROLL)

    for s in range(2):
        gate = _short_conv(gate_ref[s].reshape(L, C).astype(F32), cwg_ref, cbg_ref)
        o_ref[s] = (gate * in_ref[s].reshape(L, C)).reshape(n2h, HY_N1, C).astype(o_ref.dtype)


def _hyena_order(u4, u_blk, p4, gate_blk, cw, cb, g, mats, conv_u):
    B, n2h = u4.shape[0], u4.shape[1]
    n2f = 2 * n2h
    ncb = C_WIDTH // HY_CB
    ms1, e1, e2, ms2 = mats
    cin_blk = OFF_CIN // HY_CB
    one = pl.Buffered(1)

    def tok(col0):
        return pl.BlockSpec((2, n2h, HY_N1, HY_CB), lambda c, i: (i, 0, 0, col0 + c))

    kern = functools.partial(_hyena_kernel, conv_u=conv_u)
    return pl.pallas_call(
        kern,
        grid=(ncb, B // 2),
        in_specs=[tok(u_blk), tok(gate_blk),
                  pl.BlockSpec((3, HY_CB), lambda c, i: (0, c)),
                  pl.BlockSpec((1, HY_CB), lambda c, i: (0, c)),
                  pl.BlockSpec((3, HY_CB), lambda c, i: (0, (gate_blk - cin_blk) + c)),
                  pl.BlockSpec((1, HY_CB), lambda c, i: (0, (gate_blk - cin_blk) + c)),
                  pl.BlockSpec((n2f, 2 * HY_N1, HY_CB), lambda c, i: (0, 0, c), pipeline_mode=one),
                  pl.BlockSpec(ms1.shape, lambda c, i: (0, 0), pipeline_mode=one),
                  pl.BlockSpec(e1.shape, lambda c, i: (0, 0, 0), pipeline_mode=one),
                  pl.BlockSpec(e2.shape, lambda c, i: (0, 0, 0), pipeline_mode=one),
                  pl.BlockSpec(ms2.shape, lambda c, i: (0, 0), pipeline_mode=one)],
        out_specs=pl.BlockSpec((2, n2h, HY_N1, HY_CB), lambda c, i: (i, 0, 0, c)),
        out_shape=jax.ShapeDtypeStruct((B, n2h, HY_N1, C_WIDTH), BF16),
        scratch_shapes=[pltpu.VMEM((2, n2h, HY_N1, HY_CB), F32),
                        pltpu.VMEM((n2f, 2 * HY_N1, HY_CB), F32)],
        compiler_params=_cparams(("arbitrary", "arbitrary")),
        name="hyena_order",
    )(u4, p4, cw, cb.reshape(1, -1), cw, cb.reshape(1, -1), g, ms1, e1, e2, ms2)


def hyena_conv(p3, conv_w, conv_b, g0, g1):
    B, L, W = p3.shape
    n2h = L // HY_N1
    p4 = p3.reshape(B, n2h, HY_N1, W)
    mats = hyena_mats(L)
    cin_blk = OFF_CIN // HY_CB
    ncb = C_WIDTH // HY_CB
    cw, cb = conv_w.astype(F32), conv_b.astype(F32)
    z1 = _hyena_order(p4, cin_blk, p4, cin_blk + ncb, cw, cb, g0, mats, True)
    z2 = _hyena_order(z1, 0, p4, cin_blk + 2 * ncb, cw, cb, g1, mats, False)
    return z2.reshape(B, L, C_WIDTH)


HY_HIDDEN = 64
HY_EMB_PAD = 128
HIGHEST = lax.Precision.HIGHEST


def hyena_features(L):
    t = np.linspace(0.0, 1.0, L, dtype=np.float32)[:, None]
    omega = (2.0 * math.pi * np.arange(L, dtype=np.float32)[:, None] / L).astype(np.float32)
    f = np.linspace(1e-4, HY_BANDS - 1, HY_BANDS, dtype=np.float32)[None, :]
    z = np.concatenate([t, np.cos(f * omega), -np.sin(f * omega)], axis=-1).astype(np.float32)
    zp = np.zeros((L, HY_EMB_PAD), np.float32)
    zp[:, :z.shape[1]] = z
    deltas = np.abs(np.linspace(math.log(HY_TARGET) / HY_SLOW, math.log(HY_TARGET) / HY_FAST,
                                C_WIDTH, dtype=np.float32))
    return jnp.asarray(zp), jnp.asarray(t), jnp.asarray(deltas)


def _filter_kernel(z_ref, t_ref, w1_ref, b1_ref, w2_ref, b2_ref, w3_ref, fr_ref, dl_ref, o_ref):
    fr = fr_ref[...]
    h = jnp.sin(fr * (jnp.dot(z_ref[...], w1_ref[...], preferred_element_type=F32, precision=HIGHEST)
                      + b1_ref[...]))
    h = jnp.sin(fr * (jnp.dot(h, w2_ref[...], preferred_element_type=F32, precision=HIGHEST) + b2_ref[...]))
    h = jnp.dot(h, w3_ref[...], preferred_element_type=F32, precision=HIGHEST)
    h = h * jnp.exp(-t_ref[...] * dl_ref[...])
    o_ref[...] = h * lax.rsqrt(jnp.sum(h * h, axis=0, keepdims=True) + EPS)


def hyena_filters(L, w1, b1, w2, b2, w3, freq):
    z, t, deltas = hyena_features(L)
    n = w3.shape[1]
    tn = C_WIDTH
    w1p = jnp.zeros((HY_EMB_PAD, HY_HIDDEN), F32).at[:w1.shape[0]].set(w1.astype(F32))

    def row(v):
        return v.astype(F32).reshape(1, -1)

    return pl.pallas_call(
        _filter_kernel,
        grid=(n // tn,),
        in_specs=[pl.BlockSpec((L, HY_EMB_PAD), lambda j: (0, 0)),
                  pl.BlockSpec((L, 1), lambda j: (0, 0)),
                  pl.BlockSpec((HY_EMB_PAD, HY_HIDDEN), lambda j: (0, 0)),
                  pl.BlockSpec((1, HY_HIDDEN), lambda j: (0, 0)),
                  pl.BlockSpec((HY_HIDDEN, HY_HIDDEN), lambda j: (0, 0)),
                  pl.BlockSpec((1, HY_HIDDEN), lambda j: (0, 0)),
                  pl.BlockSpec((HY_HIDDEN, tn), lambda j: (0, j)),
                  pl.BlockSpec((1, HY_HIDDEN), lambda j: (0, 0)),
                  pl.BlockSpec((1, tn), lambda j: (0, 0))],
        out_specs=pl.BlockSpec((L, tn), lambda j: (0, j)),
        out_shape=jax.ShapeDtypeStruct((L, n), F32),
        compiler_params=_cparams(("parallel",)),
        name="hyena_filters",
    )(z, t, w1p, row(b1), w2.astype(F32), row(b2), w3.astype(F32), row(freq), deltas.reshape(1, -1))


def spectrum_mats(L):
    nf = 2 * L
    n2f = nf // HY_N1
    n2h = n2f // 2
    f = np.exp(-2j * np.pi * np.outer(np.arange(n2f), np.arange(n2h)) / n2f)
    fk = np.kron(f, np.eye(HY_SUB))
    ms1 = np.concatenate([fk.real, fk.imag], axis=0)
    k1 = np.arange(HY_N1)
    e1 = []
    for k2 in range(n2f):
        e = np.exp(-2j * np.pi * np.outer(n2f * k1 + k2, k1) / nf)
        e1.append(np.block([[e.real, -e.imag], [e.imag, e.real]]))
    return jnp.asarray(ms1, F32), jnp.asarray(np.stack(e1), F32).astype(BF16)


def _spectrum_kernel(h_ref, ms1_ref, e1_ref, o_ref, buf_ref):
    n2h, _, C = h_ref.shape
    n2f = 2 * n2h

    def s1_body(j, carry):
        r0 = pl.multiple_of(j * HY_SUB, HY_SUB)
        slab = h_ref[:, pl.ds(r0, HY_SUB), :].reshape(n2h * HY_SUB, C)
        out = jnp.dot(ms1_ref[...], slab, preferred_element_type=F32, precision=HIGHEST)
        buf_ref[:, pl.ds(r0, HY_SUB), :] = out[:n2f * HY_SUB].reshape(n2f, HY_SUB, C)
        buf_ref[:, pl.ds(HY_N1 + r0, HY_SUB), :] = out[n2f * HY_SUB:].reshape(n2f, HY_SUB, C)
        return carry

    lax.fori_loop(0, HY_N1 // HY_SUB, s1_body, 0)

    def e_body(k2, carry):
        o_ref[0, k2] = jnp.dot(e1_ref[k2], buf_ref[k2].astype(BF16), preferred_element_type=F32)
        return carry

    lax.fori_loop(0, n2f, e_body, 0)


def hyena_spectrum(filt, hy_bias):
    L = filt.shape[0]
    n2h = L // HY_N1
    n2f = 2 * n2h
    nseq = filt.shape[1] // C_WIDTH
    ncb = C_WIDTH // HY_CB
    ms1, e1 = spectrum_mats(L)
    spec = pl.pallas_call(
        _spectrum_kernel,
        grid=(nseq * ncb,),
        in_specs=[pl.BlockSpec((n2h, HY_N1, HY_CB), lambda i: (0, 0, i)),
                  pl.BlockSpec(ms1.shape, lambda i: (0, 0)),
                  pl.BlockSpec(e1.shape, lambda i: (0, 0, 0))],
        out_specs=pl.BlockSpec((1, n2f, 2 * HY_N1, HY_CB), lambda i: (i // ncb, 0, 0, i % ncb)),
        out_shape=jax.ShapeDtypeStruct((nseq, n2f, 2 * HY_N1, C_WIDTH), F32),
        scratch_shapes=[pltpu.VMEM((n2f, 2 * HY_N1, HY_CB), F32)],
        compiler_params=_cparams(("arbitrary",)),
        name="hyena_spectrum",
    )(filt.reshape(n2h, HY_N1, nseq * C_WIDTH), ms1, e1)
    spec = spec.reshape(HY_DIRS, HY_ORDER, n2f, 2, HY_N1, C_WIDTH)
    out = []
    for o in range(HY_ORDER):
        re = spec[0, o, :, 0] + spec[1, o, :, 0] + hy_bias[o].astype(F32)[None, None, :]
        im = spec[0, o, :, 1] - spec[1, o, :, 1]
        out.append(jnp.concatenate([re, im], axis=1))
    return out


def _rope_partner(width):
    j = np.arange(width)
    return np.where((j % 32) < 16, j + 16, j - 16)


def prep_w_in(w):
    ak, av, bk, bv, aq, bq, cin, gates = jnp.split(w, SPLITS, axis=-1)
    aq, bq = aq * Q_SCALE, bq * Q_SCALE
    aqr = aq[:, _rope_partner(A_Q_W)]
    akr = ak[:, _rope_partner(A_KV_W)]
    pad = jnp.zeros((w.shape[0], P_COLS - OFF_AV - A_KV_W), w.dtype)
    return jnp.concatenate([gates, cin, aq, aqr, bq, bk, bv, ak, akr, av, pad], axis=-1).astype(BF16)


def kernel(x, c, ctx, c_ctx, w_mod, b_mod, g_mix_pre, g_mix_post, g_ffn_pre, g_ffn_post,
           w_in, sink_a, rpb_b, conv_c_w, conv_c_b, hy_w1, hy_b1, hy_w2, hy_b2, hy_w3,
           hy_freq, hy_bias, w_branch, w_out, ffn_w1, ffn_w3, ffn_w2,
           moe_router, moe_w1, moe_w3, moe_w2):
    B, L, D = x.shape
    Lc = ctx.shape[1]
    N, Nc = B * L, B * Lc
    x2 = x.reshape(N, D)
    ctx2 = ctx.reshape(Nc, D)
    cs = jnp.concatenate([c, c_ctx[None, :], jnp.zeros((7, D), F32)], axis=0)

    for l in range(DEPTH):
        last = l == DEPTH - 1
        mod = modulation(cs, w_mod[l], b_mod[l])
        mods = [mod[:B, k * D:(k + 1) * D].reshape(B, 1, D) for k in range(6)]
        modc = [mod[B:B + 1, k * D:(k + 1) * D].reshape(1, 1, D) for k in range(6)]
        sh_m, sc_m, gt_m, sh_f, sc_f, gt_f = mods
        shc_m, scc_m, gtc_m, shc_f, scc_f, gtc_f = modc

        w_all = prep_w_in(w_in[l])
        p = proj(x2, g_mix_pre[l], sc_m, sh_m, w_all, L)
        if last:
            pc = proj(ctx2, g_mix_pre[l], scc_m, shc_m, w_all[:, KV_OFF:], Nc)
            coff = KV_OFF
        else:
            pc = proj(ctx2, g_mix_pre[l], scc_m, shc_m, w_all, Nc)
            coff = 0

        p3 = p.reshape(B, L, -1)
        pc3 = pc.reshape(B, Lc, -1)
        qr, kr = rope_qk(p, L)
        ya = window_attention(qr.reshape(B, L, -1), kr.reshape(B, L, -1), p3, pc3, coff, sink_a[l])
        yb = neighbourhood_attention(p3, pc3, coff, rpb_b[l])
        filt = hyena_filters(L, hy_w1[l], hy_b1[l], hy_w2[l], hy_b2[l], hy_w3[l], hy_freq[l])
        g0, g1 = hyena_spectrum(filt, hy_bias[l])
        yc = hyena_conv(p3, conv_c_w[l], conv_c_b[l], g0, g1)
        wb = w_branch[l].astype(BF16)
        wo = w_out[l].astype(BF16)
        x2 = merge(ya.reshape(N, -1).astype(BF16), yb.reshape(N, -1).astype(BF16),
                   yc.reshape(N, -1).astype(BF16), p, wb, wo, x2, g_mix_post[l], gt_m, L)
        if not last:
            yac, ybc = ctx_attention(pc3, sink_a[l])
            filt_c = hyena_filters(Lc, hy_w1[l], hy_b1[l], hy_w2[l], hy_b2[l], hy_w3[l], hy_freq[l])
            gc0, gc1 = hyena_spectrum(filt_c, hy_bias[l])
            ycc = hyena_conv(pc3, conv_c_w[l], conv_c_b[l], gc0, gc1)
            ctx2 = merge(yac.reshape(Nc, -1).astype(BF16), ybc.reshape(Nc, -1).astype(BF16),
                         ycc.reshape(Nc, -1).astype(BF16), pc, wb, wo, ctx2, g_mix_post[l], gtc_m, Nc)

        j = l // 2
        if l % 2 == 0:
            w1, w3, w2 = ffn_w1[j].astype(BF16), ffn_w3[j].astype(BF16), ffn_w2[j].astype(BF16)
            x2 = ffn(x2, g_ffn_pre[l], sc_f, sh_f, w1, w3, w2, g_ffn_post[l], gt_f, L)
            if not last:
                ctx2 = ffn(ctx2, g_ffn_pre[l], scc_f, shc_f, w1, w3, w2, g_ffn_post[l], gtc_f, Nc)
        else:
            w1, w3, w2 = moe_w1[j].astype(BF16), moe_w3[j].astype(BF16), moe_w2[j].astype(BF16)
            x2 = moe(x2, g_ffn_pre[l], sc_f, sh_f, moe_router[j], w1, w3, w2, g_ffn_post[l], gt_f, L)
            if not last:
                ctx2 = moe(ctx2, g_ffn_pre[l], scc_f, shc_f, moe_router[j], w1, w3, w2,
                           g_ffn_post[l], gtc_f, Nc)
    return x2.reshape(B, L, D)
```

```python
import functools
import math

import jax
import jax.numpy as jnp
import numpy as np
from jax import lax
from jax.experimental import pallas as pl
from jax.experimental.pallas import tpu as pltpu

F32 = jnp.float32
BF16 = jnp.bfloat16

D_MODEL = 1024
DEPTH = 2
GRID_W = 64
HEAD_DIM = 64
A_HEADS = 8
A_KV_HEADS = 2
A_GROUP = A_HEADS // A_KV_HEADS
WINDOW = 128
A_BLOCK = 128
B_HEADS = 8
NA_ROWS = 8
NA_COLS = 16
C_WIDTH = 512
HY_ORDER = 2
HY_DIRS = 2
HY_BANDS = 16
HY_TARGET = 1e-2
HY_FAST = 0.3
HY_SLOW = 1.5
N_BRANCH = 3
N_EXPERTS = 8
TOP_K = 2
ROPE_BASE = 10000.0
EPS = 1e-6
NEG = -1e30

A_KV_W = A_KV_HEADS * HEAD_DIM
A_Q_W = A_HEADS * HEAD_DIM
B_W = B_HEADS * HEAD_DIM
KV_COLS = 2 * A_KV_W + 2 * B_W
SPLITS = (A_KV_W, 2 * A_KV_W, 2 * A_KV_W + B_W, KV_COLS,
          KV_COLS + A_Q_W, KV_COLS + A_Q_W + B_W,
          KV_COLS + A_Q_W + B_W + 3 * C_WIDTH)

OFF_GATES = 0
OFF_CIN = 3072
OFF_AQ = 4608
OFF_AQR = 5120
OFF_BQ = 5632
OFF_BK = 6144
OFF_BV = 6656
OFF_AK = 7168
OFF_AKR = 7296
OFF_AV = 7424
P_COLS = 7680
KV_OFF = OFF_BK
KV_W = P_COLS - KV_OFF

VMEM_LIMIT = 48 * 1024 * 1024
LOG2E = math.log2(math.e)
Q_SCALE = HEAD_DIM ** -0.5 * LOG2E


def _cparams(sem):
    return pltpu.CompilerParams(dimension_semantics=sem, vmem_limit_bytes=VMEM_LIMIT)


def _rms(y, g):
    return y * lax.rsqrt(jnp.mean(y * y, axis=-1, keepdims=True) + EPS) * g


def _norm_mod(x, g, sc, sh):
    return _rms(x, g) * (1.0 + sc) + sh


def _mod_kernel(c_ref, w_ref, b_ref, o_ref):
    c = c_ref[...]
    s = c * jax.nn.sigmoid(c)
    o_ref[...] = jnp.dot(s, w_ref[...], preferred_element_type=F32,
                         precision=lax.Precision.HIGHEST) + b_ref[...]


def modulation(cs, w, b, tn=512):
    m, d = cs.shape
    n = w.shape[1]
    return pl.pallas_call(
        _mod_kernel,
        grid=(n // tn,),
        in_specs=[pl.BlockSpec((m, d), lambda j: (0, 0)),
                  pl.BlockSpec((d, tn), lambda j: (0, j)),
                  pl.BlockSpec((1, tn), lambda j: (0, j))],
        out_specs=pl.BlockSpec((m, tn), lambda j: (0, j)),
        out_shape=jax.ShapeDtypeStruct((m, n), F32),
        compiler_params=_cparams(("parallel",)),
        name="modulation",
    )(cs, w, b.reshape(1, n))


def _proj_kernel(x_ref, g_ref, sc_ref, sh_ref, w_ref, o_ref, h_ref):
    @pl.when(pl.program_id(1) == 0)
    def _():
        h_ref[...] = _norm_mod(x_ref[...], g_ref[...], sc_ref[0], sh_ref[0]).astype(BF16)

    o_ref[...] = jnp.dot(h_ref[...], w_ref[...], preferred_element_type=F32).astype(o_ref.dtype)


def proj(x2, g, sc, sh, w, rows_per_mod, tm=1024, tn=3840):
    m, d = x2.shape
    n = w.shape[1]
    tm = min(tm, m)
    tn = min(tn, n)
    tpb = rows_per_mod // tm
    return pl.pallas_call(
        _proj_kernel,
        grid=(m // tm, n // tn),
        in_specs=[pl.BlockSpec((tm, d), lambda i, j: (i, 0)),
                  pl.BlockSpec((1, d), lambda i, j: (0, 0)),
                  pl.BlockSpec((1, 1, d), lambda i, j: (i // tpb, 0, 0)),
                  pl.BlockSpec((1, 1, d), lambda i, j: (i // tpb, 0, 0)),
                  pl.BlockSpec((d, tn), lambda i, j: (0, j))],
        out_specs=pl.BlockSpec((tm, tn), lambda i, j: (i, j)),
        out_shape=jax.ShapeDtypeStruct((m, n), BF16),
        scratch_shapes=[pltpu.VMEM((tm, d), BF16)],
        compiler_params=_cparams(("parallel", "arbitrary")),
        name="proj",
    )(x2, g.reshape(1, d), sc, sh, w)


def _merge_kernel(ya_ref, yb_ref, yc_ref, ga_ref, gb_ref, gc_ref, wb_ref, wo_ref,
                  x_ref, gp_ref, gt_ref, o_ref):
    def branch(y_ref, g_ref, k):
        t = jnp.dot(y_ref[...], wb_ref[k], preferred_element_type=F32)
        return jax.nn.sigmoid(g_ref[...].astype(F32)) * t

    m = branch(ya_ref, ga_ref, 0) + branch(yb_ref, gb_ref, 1) + branch(yc_ref, gc_ref, 2)
    y = jnp.dot(m.astype(BF16), wo_ref[...], preferred_element_type=F32)
    o_ref[...] = x_ref[...] + gt_ref[0] * _rms(y, gp_ref[...])


def merge(ya, yb, yc, p, wb, wo, x2, gpost, gt, rows_per_mod, tm=512):
    m, d = x2.shape
    bw = ya.shape[1]
    tm = min(tm, m)
    tpb = rows_per_mod // tm
    gblk = OFF_GATES // d
    return pl.pallas_call(
        _merge_kernel,
        grid=(m // tm,),
        in_specs=[pl.BlockSpec((tm, bw), lambda i: (i, 0)),
                  pl.BlockSpec((tm, bw), lambda i: (i, 0)),
                  pl.BlockSpec((tm, bw), lambda i: (i, 0)),
                  pl.BlockSpec((tm, d), lambda i: (i, gblk)),
                  pl.BlockSpec((tm, d), lambda i: (i, gblk + 1)),
                  pl.BlockSpec((tm, d), lambda i: (i, gblk + 2)),
                  pl.BlockSpec((N_BRANCH, bw, d), lambda i: (0, 0, 0)),
                  pl.BlockSpec((d, d), lambda i: (0, 0)),
                  pl.BlockSpec((tm, d), lambda i: (i, 0)),
                  pl.BlockSpec((1, d), lambda i: (0, 0)),
                  pl.BlockSpec((1, 1, d), lambda i: (i // tpb, 0, 0))],
        out_specs=pl.BlockSpec((tm, d), lambda i: (i, 0)),
        out_shape=jax.ShapeDtypeStruct((m, d), F32),
        compiler_params=_cparams(("parallel",)),
        name="merge",
    )(ya, yb, yc, p, p, p, wb, wo, x2, gpost.reshape(1, d), gt)


def _ffn_kernel(x_ref, g_ref, sc_ref, sh_ref, w1_ref, w3_ref, w2_ref, gp_ref, gt_ref,
                o_ref, h_ref, acc_ref):
    f = pl.program_id(1)

    @pl.when(f == 0)
    def _():
        h_ref[...] = _norm_mod(x_ref[...], g_ref[...], sc_ref[0], sh_ref[0]).astype(BF16)
        acc_ref[...] = jnp.zeros_like(acc_ref)

    h = h_ref[...]
    a = jnp.dot(h, w1_ref[...], preferred_element_type=F32)
    b = jnp.dot(h, w3_ref[...], preferred_element_type=F32)
    act = (a * jax.nn.sigmoid(a) * b).astype(BF16)
    acc_ref[...] += jnp.dot(act, w2_ref[...], preferred_element_type=F32)

    @pl.when(f == pl.num_programs(1) - 1)
    def _():
        o_ref[...] = x_ref[...] + gt_ref[0] * _rms(acc_ref[...], gp_ref[...])


def ffn(x2, g, sc, sh, w1, w3, w2, gpost, gt, rows_per_mod, tm=512, tf=2816):
    m, d = x2.shape
    ff = w1.shape[1]
    tm = min(tm, m)
    tpb = rows_per_mod // tm
    return pl.pallas_call(
        _ffn_kernel,
        grid=(m // tm, ff // tf),
        in_specs=[pl.BlockSpec((tm, d), lambda i, f: (i, 0)),
                  pl.BlockSpec((1, d), lambda i, f: (0, 0)),
                  pl.BlockSpec((1, 1, d), lambda i, f: (i // tpb, 0, 0)),
                  pl.BlockSpec((1, 1, d), lambda i, f: (i // tpb, 0, 0)),
                  pl.BlockSpec((d, tf), lambda i, f: (0, f)),
                  pl.BlockSpec((d, tf), lambda i, f: (0, f)),
                  pl.BlockSpec((tf, d), lambda i, f: (f, 0)),
                  pl.BlockSpec((1, d), lambda i, f: (0, 0)),
                  pl.BlockSpec((1, 1, d), lambda i, f: (i // tpb, 0, 0))],
        out_specs=pl.BlockSpec((tm, d), lambda i, f: (i, 0)),
        out_shape=jax.ShapeDtypeStruct((m, d), F32),
        scratch_shapes=[pltpu.VMEM((tm, d), BF16), pltpu.VMEM((tm, d), F32)],
        compiler_params=_cparams(("parallel", "arbitrary")),
        name="ffn",
    )(x2, g.reshape(1, d), sc, sh, w1, w3, w2, gpost.reshape(1, d), gt)


def _router_kernel(x_ref, g_ref, sc_ref, sh_ref, rh_ref, rl_ref, h_ref, lg_ref):
    h = _norm_mod(x_ref[...], g_ref[...], sc_ref[0], sh_ref[0])
    hb = h.astype(BF16)
    h_ref[...] = hb
    hl = (h - hb.astype(F32)).astype(BF16)
    rh = rh_ref[...]
    lg_ref[...] = (jnp.dot(hb, rh, preferred_element_type=F32)
                   + jnp.dot(hl, rh, preferred_element_type=F32)
                   + jnp.dot(hb, rl_ref[...], preferred_element_type=F32))


def router(x2, g, sc, sh, r, rows_per_mod, tm=1024):
    m, d = x2.shape
    e = r.shape[1]
    tm = min(tm, m)
    tpb = rows_per_mod // tm
    rh = r.astype(BF16)
    rl = (r - rh.astype(F32)).astype(BF16)
    return pl.pallas_call(
        _router_kernel,
        grid=(m // tm,),
        in_specs=[pl.BlockSpec((tm, d), lambda i: (i, 0)),
                  pl.BlockSpec((1, d), lambda i: (0, 0)),
                  pl.BlockSpec((1, 1, d), lambda i: (i // tpb, 0, 0)),
                  pl.BlockSpec((1, 1, d), lambda i: (i // tpb, 0, 0)),
                  pl.BlockSpec((d, e), lambda i: (0, 0)),
                  pl.BlockSpec((d, e), lambda i: (0, 0))],
        out_specs=[pl.BlockSpec((tm, d), lambda i: (i, 0)),
                   pl.BlockSpec((tm, e), lambda i: (i, 0))],
        out_shape=[jax.ShapeDtypeStruct((m, d), BF16), jax.ShapeDtypeStruct((m, e), F32)],
        compiler_params=_cparams(("parallel",)),
        name="router",
    )(x2, g.reshape(1, d), sc, sh, rh, rl)


def _expert_kernel(te_ref, hs_ref, w1_ref, w3_ref, w2_ref, o_ref, acc_ref):
    i = pl.program_id(0)
    f = pl.program_id(1)
    nf = pl.num_programs(1)
    valid = te_ref[i] >= 0

    @pl.when(f == 0)
    def _():
        acc_ref[...] = jnp.zeros_like(acc_ref)

    @pl.when(valid)
    def _():
        h = hs_ref[...]
        a = jnp.dot(h, w1_ref[0], preferred_element_type=F32)
        b = jnp.dot(h, w3_ref[0], preferred_element_type=F32)
        act = (a * jax.nn.sigmoid(a) * b).astype(BF16)
        acc_ref[...] += jnp.dot(act, w2_ref[0], preferred_element_type=F32)

    @pl.when(f == nf - 1)
    def _():
        o_ref[...] = acc_ref[...].astype(o_ref.dtype)


def experts(tile_expert, hs, w1, w3, w2, tm=512, tf=1792):
    mp, d = hs.shape
    ff = w1.shape[2]

    def wmap(i, f, te):
        return (jnp.maximum(te[i], 0), 0, f)

    def w2map(i, f, te):
        return (jnp.maximum(te[i], 0), f, 0)

    return pl.pallas_call(
        _expert_kernel,
        grid_spec=pltpu.PrefetchScalarGridSpec(
            num_scalar_prefetch=1,
            grid=(mp // tm, ff // tf),
            in_specs=[pl.BlockSpec((tm, d), lambda i, f, te: (i, 0)),
                      pl.BlockSpec((1, d, tf), wmap),
                      pl.BlockSpec((1, d, tf), wmap),
                      pl.BlockSpec((1, tf, d), w2map)],
            out_specs=pl.BlockSpec((tm, d), lambda i, f, te: (i, 0)),
            scratch_shapes=[pltpu.VMEM((tm, d), F32)]),
        out_shape=jax.ShapeDtypeStruct((mp, d), BF16),
        compiler_params=_cparams(("parallel", "arbitrary")),
        name="experts",
    )(tile_expert, hs, w1, w3, w2)


def _combine_kernel(y1_ref, y2_ref, cw_ref, x_ref, gp_ref, gt_ref, o_ref):
    cw = cw_ref[...]
    y = cw[:, 0:1] * y1_ref[...].astype(F32) + cw[:, 1:2] * y2_ref[...].astype(F32)
    o_ref[...] = x_ref[...] + gt_ref[0] * _rms(y, gp_ref[...])


def combine(y1, y2, cw, x2, gpost, gt, rows_per_mod, tm=1024):
    m, d = x2.shape
    tm = min(tm, m)
    tpb = rows_per_mod // tm
    return pl.pallas_call(
        _combine_kernel,
        grid=(m // tm,),
        in_specs=[pl.BlockSpec((tm, d), lambda i: (i, 0)),
                  pl.BlockSpec((tm, d), lambda i: (i, 0)),
                  pl.BlockSpec((tm, TOP_K), lambda i: (i, 0)),
                  pl.BlockSpec((tm, d), lambda i: (i, 0)),
                  pl.BlockSpec((1, d), lambda i: (0, 0)),
                  pl.BlockSpec((1, 1, d), lambda i: (i // tpb, 0, 0))],
        out_specs=pl.BlockSpec((tm, d), lambda i: (i, 0)),
        out_shape=jax.ShapeDtypeStruct((m, d), F32),
        compiler_params=_cparams(("parallel",)),
        name="combine",
    )(y1, y2, cw, x2, gpost.reshape(1, d), gt)


def moe(x2, g, sc, sh, r, w1, w3, w2, gpost, gt, rows_per_mod, tm=1024):
    m, d = x2.shape
    e = r.shape[1]
    h, logits = router(x2, g, sc, sh, r, rows_per_mod)
    top_v, top_i = lax.top_k(logits, TOP_K)
    top_w = jax.nn.softmax(top_v, axis=-1)
    flat_e = top_i.reshape(-1)
    onehot = (flat_e[:, None] == jnp.arange(e)[None, :]).astype(jnp.int32)
    rank = jnp.cumsum(onehot, axis=0) - onehot
    counts = jnp.sum(onehot, axis=0)
    padded = ((counts + tm - 1) // tm) * tm
    pad_end = jnp.cumsum(padded)
    pad_start = pad_end - padded
    slot = pad_start[flat_e] + jnp.sum(rank * onehot, axis=1)
    mp = m * TOP_K + e * tm
    n_tiles = mp // tm
    tile_start = jnp.arange(n_tiles, dtype=jnp.int32) * tm
    tile_e = jnp.sum((tile_start[:, None] >= pad_end[None, :]).astype(jnp.int32), axis=1)
    te = jnp.minimum(tile_e, e - 1)
    tile_e = jnp.where(tile_start < pad_end[-1], tile_e, -1).astype(jnp.int32)
    order = jnp.argsort(flat_e, stable=True).astype(jnp.int32)
    first = jnp.cumsum(counts) - counts
    in_group = (tile_start - pad_start[te])[:, None] + jnp.arange(tm, dtype=jnp.int32)[None, :]
    live = (in_group < counts[te][:, None]) & (tile_e >= 0)[:, None]
    src = jnp.where(live, first[te][:, None] + in_group, 0)
    slot_token = order[src.reshape(-1)] // TOP_K
    hs = h.at[slot_token].get(mode="promise_in_bounds")
    ys = experts(tile_e, hs, w1, w3, w2, tm=tm)
    slot2 = slot.reshape(m, TOP_K)
    y1 = ys.at[slot2[:, 0]].get(mode="promise_in_bounds")
    y2 = ys.at[slot2[:, 1]].get(mode="promise_in_bounds")
    return combine(y1, y2, top_w, x2, gpost, gt, rows_per_mod)


def rope_tables(L, width):
    t = np.arange(L)
    j = np.arange(width) % HEAD_DIM
    pos = np.where(j[None, :] < HEAD_DIM // 2, (t // GRID_W)[:, None], (t % GRID_W)[:, None])
    quarter = HEAD_DIM // 4
    inv = ROPE_BASE ** (-(j % quarter).astype(np.float32) / quarter)
    ang = jnp.asarray(pos.astype(np.float32)) * jnp.asarray(inv.astype(np.float32))[None, :]
    sign = np.where((j % (2 * quarter)) < quarter, -1.0, 1.0).astype(np.float32)
    return jnp.cos(ang), jnp.sin(ang) * jnp.asarray(sign)[None, :]


def _rope_kernel(aq_ref, aqr_ref, ak_ref, akr_ref, cq_ref, sq_ref, ck_ref, sk_ref, q_ref, k_ref):
    q = aq_ref[...].astype(F32) * cq_ref[...] + aqr_ref[...].astype(F32) * sq_ref[...]
    q_ref[...] = q.astype(BF16)
    k = ak_ref[...].astype(F32) * ck_ref[...] + akr_ref[...].astype(F32) * sk_ref[...]
    k_ref[...] = k.astype(BF16)


def rope_qk(p, L, tm=1024):
    m = p.shape[0]
    tm = min(tm, L)
    nt = L // tm
    cq, sq = rope_tables(L, A_Q_W)
    ck, sk = rope_tables(L, A_KV_W)
    return pl.pallas_call(
        _rope_kernel,
        grid=(m // tm,),
        in_specs=[pl.BlockSpec((tm, A_Q_W), lambda i: (i, OFF_AQ // A_Q_W)),
                  pl.BlockSpec((tm, A_Q_W), lambda i: (i, OFF_AQR // A_Q_W)),
                  pl.BlockSpec((tm, A_KV_W), lambda i: (i, OFF_AK // A_KV_W)),
                  pl.BlockSpec((tm, A_KV_W), lambda i: (i, OFF_AKR // A_KV_W)),
                  pl.BlockSpec((tm, A_Q_W), lambda i: (i % nt, 0)),
                  pl.BlockSpec((tm, A_Q_W), lambda i: (i % nt, 0)),
                  pl.BlockSpec((tm, A_KV_W), lambda i: (i % nt, 0)),
                  pl.BlockSpec((tm, A_KV_W), lambda i: (i % nt, 0))],
        out_specs=[pl.BlockSpec((tm, A_Q_W), lambda i: (i, 0)),
                   pl.BlockSpec((tm, A_KV_W), lambda i: (i, 0))],
        out_shape=[jax.ShapeDtypeStruct((m, A_Q_W), BF16), jax.ShapeDtypeStruct((m, A_KV_W), BF16)],
        compiler_params=_cparams(("parallel",)),
        name="rope_qk",
    )(p, p, p, p, cq, sq, ck, sk)


def _dot_nt(a, b):
    return lax.dot_general(a, b, (((1,), (1,)), ((), ())), preferred_element_type=F32)


A_SPAN = A_BLOCK + 2 * WINDOW


def _wa_kernel(sink_ref, q_ref, k_ref, v_ref, kc_ref, vc_ref, o_ref):
    i = pl.program_id(1)
    L = k_ref.shape[1]
    start = pl.multiple_of(jnp.clip(i * A_BLOCK - WINDOW, 0, L - A_SPAN), A_BLOCK)
    qpos = i * A_BLOCK + lax.broadcasted_iota(jnp.int32, (A_BLOCK, A_SPAN), 0)
    kpos = start + lax.broadcasted_iota(jnp.int32, (A_BLOCK, A_SPAN), 1)
    valid = jnp.abs(qpos - kpos) <= WINDOW
    q = q_ref[0]
    scores = []
    for h in range(A_HEADS):
        ks = slice((h // A_GROUP) * HEAD_DIM, (h // A_GROUP + 1) * HEAD_DIM)
        qh = q[:, h * HEAD_DIM:(h + 1) * HEAD_DIM]
        s_loc = jnp.where(valid, _dot_nt(qh, k_ref[0, pl.ds(start, A_SPAN), ks]), NEG)
        scores.append(jnp.concatenate([s_loc, _dot_nt(qh, kc_ref[0, :, ks])], axis=1))
    probs, sinks = [], []
    for h in range(A_HEADS):
        sink = sink_ref[h]
        m = jnp.maximum(jnp.max(scores[h], axis=-1, keepdims=True), sink)
        probs.append(jnp.exp2((scores[h] - m).astype(BF16)))
        sinks.append(jnp.exp2(sink - m))
    ones = jnp.ones((A_SPAN + kc_ref.shape[1], HEAD_DIM), BF16)
    for g in range(A_KV_HEADS):
        ks = slice(g * HEAD_DIM, (g + 1) * HEAD_DIM)
        v_ext = jnp.concatenate(
            [jnp.concatenate([v_ref[0, pl.ds(start, A_SPAN), ks], vc_ref[0, :, ks]], axis=0), ones], axis=1)
        for h in range(g * A_GROUP, (g + 1) * A_GROUP):
            oe = jnp.dot(probs[h], v_ext, preferred_element_type=F32)
            den = pltpu.roll(oe, HEAD_DIM, axis=1) + sinks[h]
            o_ref[0, :, h * HEAD_DIM:(h + 1) * HEAD_DIM] = (oe / den)[:, :HEAD_DIM].astype(BF16)


def window_attention(qr, kr, p3, pc3, coff, sink):
    B, L, _ = qr.shape
    Lc = pc3.shape[1]
    return pl.pallas_call(
        _wa_kernel,
        grid=(B, L // A_BLOCK),
        in_specs=[pl.BlockSpec(memory_space=pltpu.SMEM),
                  pl.BlockSpec((1, A_BLOCK, A_Q_W), lambda b, i: (b, i, 0)),
                  pl.BlockSpec((1, L, A_KV_W), lambda b, i: (b, 0, 0)),
                  pl.BlockSpec((1, L, A_KV_W), lambda b, i: (b, 0, OFF_AV // A_KV_W)),
                  pl.BlockSpec((1, Lc, A_KV_W), lambda b, i: (b, 0, (OFF_AK - coff) // A_KV_W)),
                  pl.BlockSpec((1, Lc, A_KV_W), lambda b, i: (b, 0, (OFF_AV - coff) // A_KV_W))],
        out_specs=pl.BlockSpec((1, A_BLOCK, A_Q_W), lambda b, i: (b, i, 0)),
        out_shape=jax.ShapeDtypeStruct((B, L, A_Q_W), BF16),
        compiler_params=_cparams(("parallel", "arbitrary")),
        name="window_attention",
    )(sink.astype(F32) * LOG2E, qr, kr, p3, pc3, pc3)


NA_QROWS = 4
NA_KROWS = NA_ROWS + NA_QROWS
NA_TQ = NA_QROWS * GRID_W
NA_TK = NA_KROWS * GRID_W


def na_bias_table(rpb, rows):
    col = np.arange(GRID_W)
    col_start = np.clip(col - NA_COLS // 2, 0, GRID_W - NA_COLS)
    col_ok = (col[None, :] >= col_start[:, None]) & (col[None, :] < col_start[:, None] + NA_COLS)
    dc = np.clip(col[None, :] - col[:, None] + NA_COLS - 1, 0, 2 * NA_COLS - 2)
    a = np.arange(NA_QROWS)
    j = np.arange(NA_KROWS)
    dr_all, ok_all = [], []
    for r0, u in ((0, 0), (NA_QROWS, 0), (rows - NA_QROWS, rows - NA_KROWS)):
        r = r0 + a
        rs = np.clip(r - NA_ROWS // 2, 0, rows - NA_ROWS)
        kr = u + j
        ok = (kr[None, :] >= rs[:, None]) & (kr[None, :] < rs[:, None] + NA_ROWS)
        dr = np.clip(kr[None, :] - r[:, None] + NA_ROWS - 1, 0, 2 * NA_ROWS - 2)
        dr_all.append(dr)
        ok_all.append(ok)
    n_dr, n_dc = 2 * NA_ROWS - 1, 2 * NA_COLS - 1
    H = rpb.shape[0]
    onehot = (dc.reshape(-1)[None, :] == np.arange(n_dc)[:, None]).astype(np.float32)
    tiles = jnp.dot(rpb.astype(F32).reshape(H * n_dr, n_dc), jnp.asarray(onehot),
                    precision=lax.Precision.HIGHEST).reshape(H, n_dr, GRID_W, GRID_W)
    tiles = jnp.where(jnp.asarray(col_ok)[None, None], tiles * LOG2E, NEG)
    tiles = jnp.concatenate([tiles, jnp.full((H, 1, GRID_W, GRID_W), NEG, F32)], axis=1)
    sel = np.where(np.stack(ok_all), np.stack(dr_all), n_dr)
    out = tiles[:, sel.reshape(-1)].reshape(H, 3, NA_QROWS, NA_KROWS, GRID_W, GRID_W)
    return out.transpose(1, 0, 2, 4, 3, 5).reshape(3, H, NA_TQ, NA_TK)


def _na_kernel(q_ref, k_ref, v_ref, kc_ref, vc_ref, bias_ref, o_ref):
    blk = pl.program_id(1)
    rows = k_ref.shape[1] // GRID_W
    u = jnp.clip(blk * NA_QROWS - NA_ROWS // 2, 0, rows - NA_KROWS)
    start = pl.multiple_of(u * GRID_W, GRID_W)
    q = q_ref[0]
    kt = jnp.concatenate([k_ref[0, pl.ds(start, NA_TK), :], kc_ref[0]], axis=0)
    vt = jnp.concatenate([v_ref[0, pl.ds(start, NA_TK), :], vc_ref[0]], axis=0)
    pair = 2 * HEAD_DIM
    lane = lax.broadcasted_iota(jnp.int32, (NA_TQ, pair), 1)
    ones = jnp.ones((kt.shape[0], pair), BF16)
    zero = jnp.zeros((), BF16)
    scores = []
    for h in range(B_HEADS):
        ps = slice((h // 2) * pair, (h // 2 + 1) * pair)
        qh = jnp.where((lane < HEAD_DIM) if h % 2 == 0 else (lane >= HEAD_DIM), q[:, ps], zero)
        s = _dot_nt(qh, kt[:, ps])
        scores.append(jnp.concatenate([s[:, :NA_TK] + bias_ref[0, h], s[:, NA_TK:]], axis=1))
    probs = [jnp.exp2((s - jnp.max(s, axis=-1, keepdims=True)).astype(BF16)) for s in scores]
    for i in range(B_HEADS // 2):
        ps = slice(i * pair, (i + 1) * pair)
        v_ext = jnp.concatenate([vt[:, ps], ones], axis=1)
        oe = [jnp.dot(probs[2 * i + t], v_ext, preferred_element_type=F32) for t in range(2)]
        o = [x[:, :pair] / x[:, pair:] for x in oe]
        o_ref[0, :, ps] = jnp.where(lane < HEAD_DIM, o[0], o[1]).astype(BF16)


def neighbourhood_attention(p3, pc3, coff, rpb):
    B, L, _ = p3.shape
    Lc = pc3.shape[1]
    nblk = L // NA_TQ
    bias = na_bias_table(rpb, L // GRID_W)

    def bias_map(b, i):
        return (jnp.where(i == 0, 0, jnp.where(i == nblk - 1, 2, 1)), 0, 0, 0)

    return pl.pallas_call(
        _na_kernel,
        grid=(B, nblk),
        in_specs=[pl.BlockSpec((1, NA_TQ, B_W), lambda b, i: (b, i, OFF_BQ // B_W)),
                  pl.BlockSpec((1, L, B_W), lambda b, i: (b, 0, OFF_BK // B_W)),
                  pl.BlockSpec((1, L, B_W), lambda b, i: (b, 0, OFF_BV // B_W)),
                  pl.BlockSpec((1, Lc, B_W), lambda b, i: (b, 0, (OFF_BK - coff) // B_W)),
                  pl.BlockSpec((1, Lc, B_W), lambda b, i: (b, 0, (OFF_BV - coff) // B_W)),
                  pl.BlockSpec((1, B_HEADS, NA_TQ, NA_TK), bias_map)],
        out_specs=pl.BlockSpec((1, NA_TQ, B_W), lambda b, i: (b, i, 0)),
        out_shape=jax.ShapeDtypeStruct((B, L, B_W), BF16),
        compiler_params=_cparams(("parallel", "arbitrary")),
        name="neighbourhood_attention",
    )(p3, p3, p3, pc3, pc3, bias)


def _ctx_attn_kernel(sink_ref, aq_ref, ak_ref, av_ref, bq_ref, bk_ref, bv_ref, oa_ref, ob_ref):
    aq = aq_ref[0]
    bq = bq_ref[0]

    def head(h):
        return slice(h * HEAD_DIM, (h + 1) * HEAD_DIM)

    s_a = [_dot_nt(aq[:, head(h)], ak_ref[0, :, head(h // A_GROUP)]) for h in range(A_HEADS)]
    s_b = [_dot_nt(bq[:, head(h)], bk_ref[0, :, head(h)]) for h in range(B_HEADS)]
    p_a, r_a = [], []
    for h in range(A_HEADS):
        sink = sink_ref[h]
        m = jnp.maximum(jnp.max(s_a[h], axis=-1, keepdims=True), sink)
        p = jnp.exp2(s_a[h] - m)
        r_a.append(1.0 / (jnp.sum(p, axis=-1, keepdims=True) + jnp.exp2(sink - m)))
        p_a.append(p.astype(BF16))
    p_b, r_b = [], []
    for h in range(B_HEADS):
        p = jnp.exp2(s_b[h] - jnp.max(s_b[h], axis=-1, keepdims=True))
        r_b.append(1.0 / jnp.sum(p, axis=-1, keepdims=True))
        p_b.append(p.astype(BF16))
    for h in range(A_HEADS):
        o = jnp.dot(p_a[h], av_ref[0, :, head(h // A_GROUP)], preferred_element_type=F32)
        oa_ref[0, :, head(h)] = (o * r_a[h]).astype(BF16)
    for h in range(B_HEADS):
        o = jnp.dot(p_b[h], bv_ref[0, :, head(h)], preferred_element_type=F32)
        ob_ref[0, :, head(h)] = (o * r_b[h]).astype(BF16)


def ctx_attention(pc3, sink):
    B, Lc, _ = pc3.shape
    return pl.pallas_call(
        _ctx_attn_kernel,
        grid=(B,),
        in_specs=[pl.BlockSpec(memory_space=pltpu.SMEM),
                  pl.BlockSpec((1, Lc, A_Q_W), lambda b: (b, 0, OFF_AQ // A_Q_W)),
                  pl.BlockSpec((1, Lc, A_KV_W), lambda b: (b, 0, OFF_AK // A_KV_W)),
                  pl.BlockSpec((1, Lc, A_KV_W), lambda b: (b, 0, OFF_AV // A_KV_W)),
                  pl.BlockSpec((1, Lc, B_W), lambda b: (b, 0, OFF_BQ // B_W)),
                  pl.BlockSpec((1, Lc, B_W), lambda b: (b, 0, OFF_BK // B_W)),
                  pl.BlockSpec((1, Lc, B_W), lambda b: (b, 0, OFF_BV // B_W))],
        out_specs=[pl.BlockSpec((1, Lc, A_Q_W), lambda b: (b, 0, 0)),
                   pl.BlockSpec((1, Lc, B_W), lambda b: (b, 0, 0))],
        out_shape=[jax.ShapeDtypeStruct((B, Lc, A_Q_W), BF16), jax.ShapeDtypeStruct((B, Lc, B_W), BF16)],
        compiler_params=_cparams(("parallel",)),
        name="ctx_attention",
    )(sink.astype(F32) * LOG2E, pc3, pc3, pc3, pc3, pc3, pc3)


HY_N1 = 128
HY_SUB = 8
HY_CB = 256
HY_S_UNROLL = 16
HY_E_UNROLL = 32


def hyena_mats(L):
    nf = 2 * L
    n2f = nf // HY_N1
    n2h = n2f // 2
    eye = np.eye(HY_SUB)

    def real_block(m):
        return np.block([[m.real, -m.imag], [m.imag, m.real]])

    f = np.exp(-2j * np.pi * np.outer(np.arange(n2f), np.arange(n2h)) / n2f)
    ms1 = real_block(np.kron(f, eye))
    fi = np.exp(2j * np.pi * np.outer(np.arange(n2h), np.arange(n2f)) / n2f) / nf
    ms2 = real_block(np.kron(fi, eye))
    k1 = np.arange(HY_N1)
    n1 = np.arange(HY_N1)
    e1, e2 = [], []
    for k2 in range(n2f):
        e = np.exp(-2j * np.pi * np.outer(n2f * k1 + k2, n1) / nf)
        e1.append(real_block(e))
        e2.append(real_block(np.conj(e).T))
    to = lambda m: jnp.asarray(np.asarray(m, np.float32), dtype=BF16)
    return to(ms1), to(np.stack(e1)), to(np.stack(e2)), to(ms2)


def _short_conv(u, w_ref, b_ref):
    L = u.shape[0]
    row = lax.broadcasted_iota(jnp.int32, (HY_SUB, u.shape[1]), 0)
    prev = pltpu.roll(u, 1, axis=0)
    prev = jnp.concatenate([jnp.where(row == 0, 0.0, prev[:HY_SUB]), prev[HY_SUB:]], axis=0)
    nxt = pltpu.roll(u, L - 1, axis=0)
    nxt = jnp.concatenate([nxt[:L - HY_SUB], jnp.where(row == HY_SUB - 1, 0.0, nxt[L - HY_SUB:])], axis=0)
    return prev * w_ref[0:1, :] + u * w_ref[1:2, :] + nxt * w_ref[2:3, :] + b_ref[...]


def _hyena_kernel(u_ref, gate_ref, cwu_ref, cbu_ref, cwg_ref, cbg_ref,
                  g_ref, ms1_ref, e1_ref, e2_ref, ms2_ref, o_ref, in_ref, buf_ref, *, conv_u):
    n2h, _, C = u_ref.shape[1:]
    n2f = 2 * n2h
    L = n2h * HY_N1
    half = HY_N1

    for s in range(2):
        u = u_ref[s].reshape(L, C).astype(F32)
        if conv_u:
            u = _short_conv(u, cwu_ref, cbu_ref)
        in_ref[s] = u.reshape(n2h, HY_N1, C)

    def s1_body(j, carry):
        r0 = pl.multiple_of(j * HY_SUB, HY_SUB)
        sre = in_ref[0, :, pl.ds(r0, HY_SUB), :].reshape(n2h * HY_SUB, C)
        sim = in_ref[1, :, pl.ds(r0, HY_SUB), :].reshape(n2h * HY_SUB, C)
        slab = jnp.concatenate([sre, sim], axis=0).astype(BF16)
        out = jnp.dot(ms1_ref[...], slab, preferred_element_type=F32)
        buf_ref[:, pl.ds(r0, HY_SUB), :] = out[:n2f * HY_SUB].reshape(n2f, HY_SUB, C)
        buf_ref[:, pl.ds(half + r0, HY_SUB), :] = out[n2f * HY_SUB:].reshape(n2f, HY_SUB, C)
        return carry

    lax.fori_loop(0, HY_N1 // HY_SUB, s1_body, 0, unroll=HY_S_UNROLL)

    def e_body(k2, carry):
        blk = buf_ref[k2].astype(BF16)
        x = jnp.dot(e1_ref[k2], blk, preferred_element_type=F32)
        xr, xi = x[:half], x[half:]
        gr = g_ref[k2, :half, :]
        gi = g_ref[k2, half:, :]
        y = jnp.concatenate([xr * gr - xi * gi, xr * gi + xi * gr], axis=0).astype(BF16)
        buf_ref[k2] = jnp.dot(e2_ref[k2], y, preferred_element_type=F32)
        return carry

    lax.fori_loop(0, n2f, e_body, 0, unroll=min(n2f, HY_E_UNROLL))

    def s2_body(j, carry):
        r0 = pl.multiple_of(j * HY_SUB, HY_SUB)
        sre = buf_ref[:, pl.ds(r0, HY_SUB), :].reshape(n2f * HY_SUB, C)
        sim = buf_ref[:, pl.ds(half + r0, HY_SUB), :].reshape(n2f * HY_SUB, C)
        slab = jnp.concatenate([sre, sim], axis=0).astype(BF16)
        out = jnp.dot(ms2_ref[...], slab, preferred_element_type=F32)
        in_ref[0, :, pl.ds(r0, HY_SUB), :] = out[:n2h * HY_SUB].reshape(n2h, HY_SUB, C)
        in_ref[1, :, pl.ds(r0, HY_SUB), :] = out[n2h * HY_SUB:].reshape(n2h, HY_SUB, C)
        return carry

    lax.fori_loop(0, HY_N1 // HY_SUB, s2_body, 0, unroll=HY_S_UNROLL)

    for s in range(2):
        gate = _short_conv(gate_ref[s].reshape(L, C).astype(F32), cwg_ref, cbg_ref)
        o_ref[s] = (gate * in_ref[s].reshape(L, C)).reshape(n2h, HY_N1, C).astype(o_ref.dtype)


def _hyena_order(u4, u_blk, p4, gate_blk, cw, cb, g, mats, conv_u):
    B, n2h = u4.shape[0], u4.shape[1]
    n2f = 2 * n2h
    ncb = C_WIDTH // HY_CB
    ms1, e1, e2, ms2 = mats
    cin_blk = OFF_CIN // HY_CB
    one = pl.Buffered(1)

    def tok(col0):
        return pl.BlockSpec((2, n2h, HY_N1, HY_CB), lambda c, i: (i, 0, 0, col0 + c))

    kern = functools.partial(_hyena_kernel, conv_u=conv_u)
    return pl.pallas_call(
        kern,
        grid=(ncb, B // 2),
        in_specs=[tok(u_blk), tok(gate_blk),
                  pl.BlockSpec((3, HY_CB), lambda c, i: (0, c)),
                  pl.BlockSpec((1, HY_CB), lambda c, i: (0, c)),
                  pl.BlockSpec((3, HY_CB), lambda c, i: (0, (gate_blk - cin_blk) + c)),
                  pl.BlockSpec((1, HY_CB), lambda c, i: (0, (gate_blk - cin_blk) + c)),
                  pl.BlockSpec((n2f, 2 * HY_N1, HY_CB), lambda c, i: (0, 0, c), pipeline_mode=one),
                  pl.BlockSpec(ms1.shape, lambda c, i: (0, 0), pipeline_mode=one),
                  pl.BlockSpec(e1.shape, lambda c, i: (0, 0, 0), pipeline_mode=one),
                  pl.BlockSpec(e2.shape, lambda c, i: (0, 0, 0), pipeline_mode=one),
                  pl.BlockSpec(ms2.shape, lambda c, i: (0, 0), pipeline_mode=one)],
        out_specs=pl.BlockSpec((2, n2h, HY_N1, HY_CB), lambda c, i: (i, 0, 0, c)),
        out_shape=jax.ShapeDtypeStruct((B, n2h, HY_N1, C_WIDTH), BF16),
        scratch_shapes=[pltpu.VMEM((2, n2h, HY_N1, HY_CB), F32),
                        pltpu.VMEM((n2f, 2 * HY_N1, HY_CB), F32)],
        compiler_params=_cparams(("arbitrary", "arbitrary")),
        name="hyena_order",
    )(u4, p4, cw, cb.reshape(1, -1), cw, cb.reshape(1, -1), g, ms1, e1, e2, ms2)


def hyena_conv(p3, conv_w, conv_b, g0, g1):
    B, L, W = p3.shape
    n2h = L // HY_N1
    p4 = p3.reshape(B, n2h, HY_N1, W)
    mats = hyena_mats(L)
    cin_blk = OFF_CIN // HY_CB
    ncb = C_WIDTH // HY_CB
    cw, cb = conv_w.astype(F32), conv_b.astype(F32)
    z1 = _hyena_order(p4, cin_blk, p4, cin_blk + ncb, cw, cb, g0, mats, True)
    z2 = _hyena_order(z1, 0, p4, cin_blk + 2 * ncb, cw, cb, g1, mats, False)
    return z2.reshape(B, L, C_WIDTH)


HY_HIDDEN = 64
HY_EMB_PAD = 128
HIGHEST = lax.Precision.HIGHEST


def hyena_features(L):
    t = np.linspace(0.0, 1.0, L, dtype=np.float32)[:, None]
    omega = (2.0 * math.pi * np.arange(L, dtype=np.float32)[:, None] / L).astype(np.float32)
    f = np.linspace(1e-4, HY_BANDS - 1, HY_BANDS, dtype=np.float32)[None, :]
    z = np.concatenate([t, np.cos(f * omega), -np.sin(f * omega)], axis=-1).astype(np.float32)
    zp = np.zeros((L, HY_EMB_PAD), np.float32)
    zp[:, :z.shape[1]] = z
    deltas = np.abs(np.linspace(math.log(HY_TARGET) / HY_SLOW, math.log(HY_TARGET) / HY_FAST,
                                C_WIDTH, dtype=np.float32))
    return jnp.asarray(zp), jnp.asarray(t), jnp.asarray(deltas)


def _filter_kernel(z_ref, t_ref, w1_ref, b1_ref, w2_ref, b2_ref, w3_ref, fr_ref, dl_ref, o_ref):
    fr = fr_ref[...]
    h = jnp.sin(fr * (jnp.dot(z_ref[...], w1_ref[...], preferred_element_type=F32, precision=HIGHEST)
                      + b1_ref[...]))
    h = jnp.sin(fr * (jnp.dot(h, w2_ref[...], preferred_element_type=F32, precision=HIGHEST) + b2_ref[...]))
    h = jnp.dot(h, w3_ref[...], preferred_element_type=F32, precision=HIGHEST)
    h = h * jnp.exp(-t_ref[...] * dl_ref[...])
    o_ref[...] = h * lax.rsqrt(jnp.sum(h * h, axis=0, keepdims=True) + EPS)


def hyena_filters(L, w1, b1, w2, b2, w3, freq):
    z, t, deltas = hyena_features(L)
    n = w3.shape[1]
    tn = C_WIDTH
    w1p = jnp.zeros((HY_EMB_PAD, HY_HIDDEN), F32).at[:w1.shape[0]].set(w1.astype(F32))

    def row(v):
        return v.astype(F32).reshape(1, -1)

    return pl.pallas_call(
        _filter_kernel,
        grid=(n // tn,),
        in_specs=[pl.BlockSpec((L, HY_EMB_PAD), lambda j: (0, 0)),
                  pl.BlockSpec((L, 1), lambda j: (0, 0)),
                  pl.BlockSpec((HY_EMB_PAD, HY_HIDDEN), lambda j: (0, 0)),
                  pl.BlockSpec((1, HY_HIDDEN), lambda j: (0, 0)),
                  pl.BlockSpec((HY_HIDDEN, HY_HIDDEN), lambda j: (0, 0)),
                  pl.BlockSpec((1, HY_HIDDEN), lambda j: (0, 0)),
                  pl.BlockSpec((HY_HIDDEN, tn), lambda j: (0, j)),
                  pl.BlockSpec((1, HY_HIDDEN), lambda j: (0, 0)),
                  pl.BlockSpec((1, tn), lambda j: (0, 0))],
        out_specs=pl.BlockSpec((L, tn), lambda j: (0, j)),
        out_shape=jax.ShapeDtypeStruct((L, n), F32),
        compiler_params=_cparams(("parallel",)),
        name="hyena_filters",
    )(z, t, w1p, row(b1), w2.astype(F32), row(b2), w3.astype(F32), row(freq), deltas.reshape(1, -1))


def spectrum_mats(L):
    nf = 2 * L
    n2f = nf // HY_N1
    n2h = n2f // 2
    f = np.exp(-2j * np.pi * np.outer(np.arange(n2f), np.arange(n2h)) / n2f)
    fk = np.kron(f, np.eye(HY_SUB))
    ms1 = np.concatenate([fk.real, fk.imag], axis=0)
    k1 = np.arange(HY_N1)
    e1 = []
    for k2 in range(n2f):
        e = np.exp(-2j * np.pi * np.outer(n2f * k1 + k2, k1) / nf)
        e1.append(np.block([[e.real, -e.imag], [e.imag, e.real]]))
    return jnp.asarray(ms1, F32), jnp.asarray(np.stack(e1), F32).astype(BF16)


def _spectrum_kernel(h_ref, ms1_ref, e1_ref, o_ref, buf_ref):
    n2h, _, C = h_ref.shape
    n2f = 2 * n2h

    def s1_body(j, carry):
        r0 = pl.multiple_of(j * HY_SUB, HY_SUB)
        slab = h_ref[:, pl.ds(r0, HY_SUB), :].reshape(n2h * HY_SUB, C)
        out = jnp.dot(ms1_ref[...], slab, preferred_element_type=F32, precision=HIGHEST)
        buf_ref[:, pl.ds(r0, HY_SUB), :] = out[:n2f * HY_SUB].reshape(n2f, HY_SUB, C)
        buf_ref[:, pl.ds(HY_N1 + r0, HY_SUB), :] = out[n2f * HY_SUB:].reshape(n2f, HY_SUB, C)
        return carry

    lax.fori_loop(0, HY_N1 // HY_SUB, s1_body, 0)

    def e_body(k2, carry):
        o_ref[0, k2] = jnp.dot(e1_ref[k2], buf_ref[k2].astype(BF16), preferred_element_type=F32)
        return carry

    lax.fori_loop(0, n2f, e_body, 0)


def hyena_spectrum(filt, hy_bias):
    L = filt.shape[0]
    n2h = L // HY_N1
    n2f = 2 * n2h
    nseq = filt.shape[1] // C_WIDTH
    ncb = C_WIDTH // HY_CB
    ms1, e1 = spectrum_mats(L)
    spec = pl.pallas_call(
        _spectrum_kernel,
        grid=(nseq * ncb,),
        in_specs=[pl.BlockSpec((n2h, HY_N1, HY_CB), lambda i: (0, 0, i)),
                  pl.BlockSpec(ms1.shape, lambda i: (0, 0)),
                  pl.BlockSpec(e1.shape, lambda i: (0, 0, 0))],
        out_specs=pl.BlockSpec((1, n2f, 2 * HY_N1, HY_CB), lambda i: (i // ncb, 0, 0, i % ncb)),
        out_shape=jax.ShapeDtypeStruct((nseq, n2f, 2 * HY_N1, C_WIDTH), F32),
        scratch_shapes=[pltpu.VMEM((n2f, 2 * HY_N1, HY_CB), F32)],
        compiler_params=_cparams(("arbitrary",)),
        name="hyena_spectrum",
    )(filt.reshape(n2h, HY_N1, nseq * C_WIDTH), ms1, e1)
    spec = spec.reshape(HY_DIRS, HY_ORDER, n2f, 2, HY_N1, C_WIDTH)
    out = []
    for o in range(HY_ORDER):
        re = spec[0, o, :, 0] + spec[1, o, :, 0] + hy_bias[o].astype(F32)[None, None, :]
        im = spec[0, o, :, 1] - spec[1, o, :, 1]
        out.append(jnp.concatenate([re, im], axis=1))
    return out


def _rope_partner(width):
    j = np.arange(width)
    return np.where((j % 32) < 16, j + 16, j - 16)


def prep_w_in(w):
    ak, av, bk, bv, aq, bq, cin, gates = jnp.split(w, SPLITS, axis=-1)
    aq, bq = aq * Q_SCALE, bq * Q_SCALE
    aqr = aq[:, _rope_partner(A_Q_W)]
    akr = ak[:, _rope_partner(A_KV_W)]
    pad = jnp.zeros((w.shape[0], P_COLS - OFF_AV - A_KV_W), w.dtype)
    return jnp.concatenate([gates, cin, aq, aqr, bq, bk, bv, ak, akr, av, pad], axis=-1).astype(BF16)


def kernel(x, c, ctx, c_ctx, w_mod, b_mod, g_mix_pre, g_mix_post, g_ffn_pre, g_ffn_post,
           w_in, sink_a, rpb_b, conv_c_w, conv_c_b, hy_w1, hy_b1, hy_w2, hy_b2, hy_w3,
           hy_freq, hy_bias, w_branch, w_out, ffn_w1, ffn_w3, ffn_w2,
           moe_router, moe_w1, moe_w3, moe_w2):
    B, L, D = x.shape
    Lc = ctx.shape[1]
    N, Nc = B * L, B * Lc
    x2 = x.reshape(N, D)
    ctx2 = ctx.reshape(Nc, D)
    cs = jnp.concatenate([c, c_ctx[None, :], jnp.zeros((7, D), F32)], axis=0)

    for l in range(DEPTH):
        last = l == DEPTH - 1
        mod = modulation(cs, w_mod[l], b_mod[l])
        mods = [mod[:B, k * D:(k + 1) * D].reshape(B, 1, D) for k in range(6)]
        modc = [mod[B:B + 1, k * D:(k + 1) * D].reshape(1, 1, D) for k in range(6)]
        sh_m, sc_m, gt_m, sh_f, sc_f, gt_f = mods
        shc_m, scc_m, gtc_m, shc_f, scc_f, gtc_f = modc

        w_all = prep_w_in(w_in[l])
        p = proj(x2, g_mix_pre[l], sc_m, sh_m, w_all, L)
        if last:
            pc = proj(ctx2, g_mix_pre[l], scc_m, shc_m, w_all[:, KV_OFF:], Nc)
            coff = KV_OFF
        else:
            pc = proj(ctx2, g_mix_pre[l], scc_m, shc_m, w_all, Nc)
            coff = 0

        p3 = p.reshape(B, L, -1)
        pc3 = pc.reshape(B, Lc, -1)
        qr, kr = rope_qk(p, L)
        ya = window_attention(qr.reshape(B, L, -1), kr.reshape(B, L, -1), p3, pc3, coff, sink_a[l])
        yb = neighbourhood_attention(p3, pc3, coff, rpb_b[l])
        filt = hyena_filters(L, hy_w1[l], hy_b1[l], hy_w2[l], hy_b2[l], hy_w3[l], hy_freq[l])
        g0, g1 = hyena_spectrum(filt, hy_bias[l])
        yc = hyena_conv(p3, conv_c_w[l], conv_c_b[l], g0, g1)
        wb = w_branch[l].astype(BF16)
        wo = w_out[l].astype(BF16)
        x2 = merge(ya.reshape(N, -1).astype(BF16), yb.reshape(N, -1).astype(BF16),
                   yc.reshape(N, -1).astype(BF16), p, wb, wo, x2, g_mix_post[l], gt_m, L)
        if not last:
            yac, ybc = ctx_attention(pc3, sink_a[l])
            filt_c = hyena_filters(Lc, hy_w1[l], hy_b1[l], hy_w2[l], hy_b2[l], hy_w3[l], hy_freq[l])
            gc0, gc1 = hyena_spectrum(filt_c, hy_bias[l])
            ycc = hyena_conv(pc3, conv_c_w[l], conv_c_b[l], gc0, gc1)
            ctx2 = merge(yac.reshape(Nc, -1).astype(BF16), ybc.reshape(Nc, -1).astype(BF16),
                         ycc.reshape(Nc, -1).astype(BF16), pc, wb, wo, ctx2, g_mix_post[l], gtc_m, Nc)

        j = l // 2
        if l % 2 == 0:
            w1, w3, w2 = ffn_w1[j].astype(BF16), ffn_w3[j].astype(BF16), ffn_w2[j].astype(BF16)
            x2 = ffn(x2, g_ffn_pre[l], sc_f, sh_f, w1, w3, w2, g_ffn_post[l], gt_f, L)
            if not last:
                ctx2 = ffn(ctx2, g_ffn_pre[l], scc_f, shc_f, w1, w3, w2, g_ffn_post[l], gtc_f, Nc)
        else:
            w1, w3, w2 = moe_w1[j].astype(BF16), moe_w3[j].astype(BF16), moe_w2[j].astype(BF16)
            x2 = moe(x2, g_ffn_pre[l], sc_f, sh_f, moe_router[j], w1, w3, w2, g_ffn_post[l], gt_f, L)
            if not last:
                ctx2 = moe(ctx2, g_ffn_pre[l], scc_f, shc_f, moe_router[j], w1, w3, w2,
                           g_ffn_post[l], gtc_f, Nc)
    return x2.reshape(B, L, D)
```
